```python
import math
import jax
import jax.numpy as jnp
from jax import lax
import numpy as np

D_MODEL = 1024
BATCH = 2
SEQ = 8192
DEPTH = 4
DEC_BATCH = 32
DEC_SEQ = 8
PAST_LEN = 8192
PAGE_SIZE = 128

N_A_LAYERS = DEPTH // 2
N_B_LAYERS = DEPTH - N_A_LAYERS

M_EXPAND = 2
M_D_INNER = M_EXPAND * D_MODEL
M_HEADDIM = 64
M_HEADS = M_D_INNER // M_HEADDIM
M_NGROUPS = 4
M_D_STATE = 128
M_CONV = 4
M_GN = M_NGROUPS * M_D_STATE
M_CONV_DIM = M_D_INNER + 2 * M_GN
M_IN_DIM = 2 * M_D_INNER + 2 * M_GN + M_HEADS
SSD_CHUNK = 128

HEAD_DIM = 64
HEADS_PER_GROUP = D_MODEL // HEAD_DIM
DIL_PAIRS = ((128, 1), (512, 4), (2048, 16))
N_DIL = len(DIL_PAIRS)
ATT_HEADS = N_DIL * HEADS_PER_GROUP
ATT_WIDTH = ATT_HEADS * HEAD_DIM
ATT_OUT = HEADS_PER_GROUP * HEAD_DIM
WINDOW_MAX = 2048
ATT_BLOCK = 128
ROPE_THETA = 10000.0

D_FF = 2816
FFN_CONV = 3

EPS = 1e-6

kernel_name = 'yoco_ssd_dilated_convffn_step'


def rmsnorm(x, g):
    xf = x.astype(jnp.float32)
    y = xf * lax.rsqrt(jnp.mean(xf * xf, axis=-1, keepdims=True) + EPS)
    return (y * g.astype(jnp.float32)).astype(x.dtype)


def modulate(x, shift, scale):
    return x * (1 + scale[:, None]) + shift[:, None]


def causal_dwconv(x, prev, w, b):
    K = w.shape[0]
    T = x.shape[1]
    xc = jnp.concatenate([prev.astype(x.dtype), x], axis=1)
    y = sum(xc[:, k:k + T] * w[k] for k in range(K)) + b
    return y, xc[:, T:]


def rope(x, pos):
    dh = x.shape[-1]
    half = dh // 2
    inv = ROPE_THETA ** (-jnp.arange(half, dtype=jnp.float32) * (2.0 / dh))
    ang = pos.astype(jnp.float32)[:, None] * inv[None, :]
    cos = jnp.cos(ang)[None, :, None, :]
    sin = jnp.sin(ang)[None, :, None, :]
    xf = x.astype(jnp.float32)
    x1, x2 = xf[..., :half], xf[..., half:]
    return jnp.concatenate([x1 * cos - x2 * sin, x2 * cos + x1 * sin], axis=-1).astype(x.dtype)


def ssd_scan(x, dt, A, Bm, Cm, h0):
    f32 = jnp.float32
    b, L, H, P = x.shape
    G, N = Bm.shape[2], Bm.shape[3]
    E = H // G
    Q = min(SSD_CHUNK, L)
    nc = -(-L // Q)
    pad = nc * Q - L

    def padt(t):
        return jnp.pad(t.astype(f32), [(0, 0), (0, pad)] + [(0, 0)] * (t.ndim - 2))

    xc = padt(x).reshape(b, nc, Q, G, E, P)
    dtc = padt(dt).reshape(b, nc, Q, G, E)
    Bc = padt(Bm).reshape(b, nc, Q, G, N)
    Cc = padt(Cm).reshape(b, nc, Q, G, N)
    acum = jnp.cumsum(dtc * A.astype(f32).reshape(G, E), axis=2)
    seg = acum[:, :, :, None] - acum[:, :, None, :]
    causal = jnp.tril(jnp.ones((Q, Q), dtype=bool))[:, :, None, None]
    Lm = jnp.exp(jnp.where(causal, seg, -jnp.inf))
    xdt = xc * dtc[..., None]
    CB = jnp.einsum('bclgn,bcsgn->bclsg', Cc, Bc)
    y_diag = jnp.einsum('bclsge,bcsgep->bclgep', CB[..., None] * Lm, xdt)
    xdt_dec = xdt * jnp.exp(acum[:, :, -1:] - acum)[..., None]
    states = jnp.einsum('bclgn,bclgep->bcgepn', Bc, xdt_dec)
    chunk_decay = jnp.exp(acum[:, :, -1])

    def step(h, inp):
        dec, s = inp
        return h * dec[..., None, None] + s, h

    h_last, h_prev = lax.scan(step, h0.astype(f32).reshape(b, G, E, P, N),
                              (jnp.moveaxis(chunk_decay, 1, 0), jnp.moveaxis(states, 1, 0)))
    h_prev = jnp.moveaxis(h_prev, 0, 1)
    y_off = jnp.einsum('bclgn,bcgepn->bclgep', Cc, h_prev) * jnp.exp(acum)[..., None]
    y = (y_diag + y_off).reshape(b, nc * Q, H, P)[:, :L]
    return y, h_last.reshape(b, H, P, N)


def mamba2_mixer(xn, ssm0, conv0, w_in, conv_w, conv_b, dt_bias, A_log, Dskip, norm_w, w_out):
    f32 = jnp.float32
    B_, T, _ = xn.shape
    zxbcdt = xn @ w_in
    z = zxbcdt[..., :M_D_INNER]
    xBC = zxbcdt[..., M_D_INNER:M_D_INNER + M_CONV_DIM]
    dt = zxbcdt[..., M_D_INNER + M_CONV_DIM:]
    xBC, conv_new = causal_dwconv(xBC, conv0, conv_w, conv_b)
    xBC = jax.nn.silu(xBC)
    xs = xBC[..., :M_D_INNER].reshape(B_, T, M_HEADS, M_HEADDIM)
    Bm = xBC[..., M_D_INNER:M_D_INNER + M_GN].reshape(B_, T, M_NGROUPS, M_D_STATE)
    Cm = xBC[..., M_D_INNER + M_GN:].reshape(B_, T, M_NGROUPS, M_D_STATE)
    dt = jax.nn.softplus(dt.astype(f32) + dt_bias.astype(f32))
    A = -jnp.exp(A_log.astype(f32))
    y, ssm_new = ssd_scan(xs, dt, A, Bm, Cm, ssm0)
    y = y + Dskip.astype(f32)[:, None] * xs.astype(f32)
    yg = (y.reshape(B_, T, M_D_INNER) * jax.nn.silu(z.astype(f32))).reshape(
        B_, T, M_NGROUPS, M_D_INNER // M_NGROUPS)
    yg = yg * lax.rsqrt(jnp.mean(yg * yg, axis=-1, keepdims=True) + EPS)
    y = (yg.reshape(B_, T, M_D_INNER) * norm_w.astype(f32)).astype(xn.dtype)
    return y @ w_out, ssm_new.astype(ssm0.dtype), conv_new


def conv_ffn(xn, prev, w_up, conv_w, conv_b, w_down):
    u, new_prev = causal_dwconv(xn @ w_up, prev, conv_w, conv_b)
    return (jax.nn.silu(u[..., :D_FF]) * u[..., D_FF:]) @ w_down, new_prev


def dilated_band(q, k, v, window, dil):
    f32 = jnp.float32
    B_, S, H, Dh = q.shape
    band = window // dil
    span = dil * ATT_BLOCK
    L = -(-S // span) * span
    n = L // dil
    nb = n // ATT_BLOCK

    def split(t):
        t = jnp.pad(t, [(0, 0), (0, L - S), (0, 0), (0, 0)])
        return t.reshape(B_, n, dil, H, Dh).transpose(0, 2, 1, 3, 4).reshape(B_, dil, nb, ATT_BLOCK, H, Dh)

    def band_keys(t):
        tp = jnp.pad(t, [(0, 0), (0, 0), (1, 0), (0, 0), (0, 0), (0, 0)])
        return jnp.concatenate([tp[:, :, :-1], tp[:, :, 1:]], axis=3)

    qb = split(q)
    kk = band_keys(split(k))
    vv = band_keys(split(v))
    s = jnp.einsum('brcqhe,brckhe->brchqk', qb, kk).astype(f32) * (Dh ** -0.5)
    i = jnp.arange(ATT_BLOCK)[:, None]
    j = jnp.arange(2 * ATT_BLOCK)[None, :]
    dist = ATT_BLOCK + i - j
    c = jnp.arange(nb)[:, None, None]
    valid = (dist >= 0) & (dist <= band) & ((c > 0) | (j >= ATT_BLOCK))
    s = jnp.where(valid[None, None, :, None], s, -jnp.inf)
    m = s.max(axis=-1)
    p = jnp.exp(s - m[..., None])
    l = p.sum(axis=-1)
    o = jnp.einsum('brchqk,brckhe->brcqhe', p, vv.astype(f32))

    def merge(t):
        rest = t.shape[4:]
        t = t.reshape((B_, dil, n) + rest).swapaxes(1, 2).reshape((B_, L) + rest)
        return t[:, :S]

    return merge(m.swapaxes(-1, -2)), merge(l.swapaxes(-1, -2)), merge(o)


def dilated_gather(q, k_all, v_all, window, dil, q_off):
    f32 = jnp.float32
    B_, T, H, Dh = q.shape
    band = window // dil
    idx = q_off + jnp.arange(T)[:, None] - dil * jnp.arange(band + 1)[None, :]
    valid = idx >= 0
    idx = jnp.maximum(idx, 0)
    kg = jnp.take(k_all, idx, axis=1)
    vg = jnp.take(v_all, idx, axis=1)
    s = jnp.einsum('bthe,btkhe->bthk', q, kg).astype(f32) * (Dh ** -0.5)
    s = jnp.where(valid[None, :, None, :], s, -jnp.inf)
    m = s.max(axis=-1)
    p = jnp.exp(s - m[..., None])
    l = p.sum(axis=-1)
    o = jnp.einsum('bthk,btkhe->bthe', p, vg.astype(f32))
    return m, l, o


def combine_groups(parts):
    m = jnp.stack([pt[0] for pt in parts])
    l = jnp.stack([pt[1] for pt in parts])
    o = jnp.stack([pt[2] for pt in parts])
    w = jnp.exp(m - m.max(axis=0))
    return jnp.einsum('gbth,gbthe->bthe', w, o) / jnp.sum(w * l, axis=0)[..., None]


def dilated_attention(xn, pos, kv, w_q, w_o):
    B_, T, _ = xn.shape
    k_src, v_src, off = kv
    q = rope((xn @ w_q).reshape(B_, T, ATT_HEADS, HEAD_DIM), pos)
    parts = []
    for g, (win, dil) in enumerate(DIL_PAIRS):
        hs = slice(g * HEADS_PER_GROUP, (g + 1) * HEADS_PER_GROUP)
        if off is None:
            parts.append(dilated_band(q[:, :, hs], k_src[:, :, hs], v_src[:, :, hs], win, dil))
        else:
            parts.append(dilated_gather(q[:, :, hs], k_src[:, :, hs], v_src[:, :, hs], win, dil, off))
    o = combine_groups(parts)
    return o.reshape(B_, T, ATT_OUT).astype(xn.dtype) @ w_o


def trunk(x, c, pos, ssm0, conv0, ffn0, kv_past, p):
    B_, T, _ = x.shape
    h = x
    cs = jax.nn.silu(c)
    ssm_out, conv_out, ffn_out = [], [], []
    kv = None
    k_new = None
    v_new = None
    for i in range(DEPTH):
        mod = cs @ p['ada_w'][i] + p['ada_b'][i]
        sh1, sc1, g1, sh2, sc2, g2 = jnp.split(mod, 6, axis=-1)
        xn = modulate(rmsnorm(h, p['norm_mix'][i]), sh1, sc1)
        if i < N_A_LAYERS:
            y, s_new, cv_new = mamba2_mixer(xn, ssm0[i], conv0[i], p['m_w_in'][i], p['m_conv_w'][i],
                                           p['m_conv_b'][i], p['m_dt_bias'][i], p['m_A_log'][i],
                                           p['m_D'][i], p['m_norm'][i], p['m_w_out'][i])
            ssm_out.append(s_new)
            conv_out.append(cv_new)
        else:
            jb = i - N_A_LAYERS
            y = dilated_attention(xn, pos, kv, p['w_q'][jb], p['w_o'][jb])
        h = h + g1[:, None] * y
        xn = modulate(rmsnorm(h, p['norm_ffn'][i]), sh2, sc2)
        y, f_new = conv_ffn(xn, ffn0[i], p['ffn_w_up'][i], p['ffn_conv_w'][i], p['ffn_conv_b'][i],
                            p['ffn_w_down'][i])
        ffn_out.append(f_new)
        h = h + g2[:, None] * y
        if i == N_A_LAYERS - 1:
            ksh, ksc = jnp.split(cs @ p['kv_ada_w'] + p['kv_ada_b'], 2, axis=-1)
            xkv = modulate(rmsnorm(h, p['kv_norm']), ksh, ksc)
            kvp = (xkv @ p['w_kv']).reshape(B_, T, 2, ATT_HEADS, HEAD_DIM)
            k_new = rope(kvp[:, :, 0], pos)
            v_new = kvp[:, :, 1]
            if kv_past is None:
                kv = (k_new, v_new, None)
            else:
                ck, cv = kv_past
                kv = (jnp.concatenate([ck.astype(k_new.dtype), k_new], axis=1),
                      jnp.concatenate([cv.astype(v_new.dtype), v_new], axis=1), ck.shape[1])
    y = rmsnorm(h, p['final_norm'])
    return y, jnp.stack(ssm_out), jnp.stack(conv_out), jnp.stack(ffn_out), k_new, v_new


def setup_inputs(seed: int = 0) -> dict:
    key = jax.random.key(seed)
    keys = iter(jax.random.split(key, 48))
    f32 = jnp.float32

    def nrm(shape, scale):
        return jax.random.normal(next(keys), shape, f32) * scale

    def gain(shape):
        return 1.0 + nrm(shape, 0.01)

    win_buf = min(WINDOW_MAX, PAST_LEN)
    D = D_MODEL
    x_prompt = nrm((BATCH, SEQ, D), 1.0)
    x_sample = nrm((DEC_BATCH, DEC_SEQ, D), 1.0)
    state_ssm = nrm((N_A_LAYERS, DEC_BATCH, M_HEADS, M_HEADDIM, M_D_STATE), 0.1)
    state_conv = nrm((N_A_LAYERS, DEC_BATCH, M_CONV - 1, M_CONV_DIM), 1.0)
    state_ffn_conv = nrm((DEPTH, DEC_BATCH, FFN_CONV - 1, 2 * D_FF), 1.0)
    cache_k = nrm((DEC_BATCH, win_buf, ATT_HEADS, HEAD_DIM), 1.0)
    cache_v = nrm((DEC_BATCH, win_buf, ATT_HEADS, HEAD_DIM), 1.0)
    c_prompt = nrm((BATCH, D), 1.0)
    c_sample = nrm((DEC_BATCH, D), 1.0)
    ada_w = nrm((DEPTH, D, 6 * D), 0.5 * D ** -0.5)
    ada_b = nrm((DEPTH, 6 * D), 0.02)
    norm_mix = gain((DEPTH, D))
    norm_ffn = gain((DEPTH, D))
    m_w_in = nrm((N_A_LAYERS, D, M_IN_DIM), D ** -0.5)
    m_conv_w = nrm((N_A_LAYERS, M_CONV, M_CONV_DIM), 0.5)
    m_conv_b = nrm((N_A_LAYERS, M_CONV_DIM), 0.02)
    dt0 = jnp.exp(jax.random.uniform(next(keys), (N_A_LAYERS, M_HEADS), f32,
                                     math.log(1e-3), math.log(1e-1)))
    m_dt_bias = dt0 + jnp.log(-jnp.expm1(-dt0))
    m_A_log = jnp.log(jax.random.uniform(next(keys), (N_A_LAYERS, M_HEADS), f32, 1.0, 16.0))
    m_D = 1.0 + nrm((N_A_LAYERS, M_HEADS), 0.1)
    m_norm = gain((N_A_LAYERS, M_D_INNER))
    m_w_out = nrm((N_A_LAYERS, M_D_INNER, D), M_D_INNER ** -0.5)
    kv_norm = gain((D,))
    kv_ada_w = nrm((D, 2 * D), 0.5 * D ** -0.5)
    kv_ada_b = nrm((2 * D,), 0.02)
    w_kv = nrm((D, 2 * ATT_WIDTH), D ** -0.5)
    w_q = nrm((N_B_LAYERS, D, ATT_WIDTH), D ** -0.5)
    w_o = nrm((N_B_LAYERS, ATT_OUT, D), ATT_OUT ** -0.5)
    ffn_w_up = nrm((DEPTH, D, 2 * D_FF), D ** -0.5)
    ffn_conv_w = nrm((DEPTH, FFN_CONV, 2 * D_FF), 0.6)
    ffn_conv_b = nrm((DEPTH, 2 * D_FF), 0.02)
    ffn_w_down = nrm((DEPTH, D_FF, D), D_FF ** -0.5)
    final_norm = gain((D,))
    return {'x_prompt': x_prompt, 'x_sample': x_sample, 'state_ssm': state_ssm, 'state_conv': state_conv,
            'state_ffn_conv': state_ffn_conv, 'cache_k': cache_k, 'cache_v': cache_v,
            'c_prompt': c_prompt, 'c_sample': c_sample, 'ada_w': ada_w, 'ada_b': ada_b,
            'norm_mix': norm_mix, 'norm_ffn': norm_ffn, 'm_w_in': m_w_in, 'm_conv_w': m_conv_w,
            'm_conv_b': m_conv_b, 'm_dt_bias': m_dt_bias, 'm_A_log': m_A_log, 'm_D': m_D,
            'm_norm': m_norm, 'm_w_out': m_w_out, 'kv_norm': kv_norm, 'kv_ada_w': kv_ada_w,
            'kv_ada_b': kv_ada_b, 'w_kv': w_kv, 'w_q': w_q, 'w_o': w_o, 'ffn_w_up': ffn_w_up,
            'ffn_conv_w': ffn_conv_w, 'ffn_conv_b': ffn_conv_b, 'ffn_w_down': ffn_w_down,
            'final_norm': final_norm}


def reference(x_prompt, x_sample, state_ssm, state_conv, state_ffn_conv, cache_k, cache_v,
              c_prompt, c_sample, ada_w, ada_b, norm_mix, norm_ffn, m_w_in, m_conv_w, m_conv_b,
              m_dt_bias, m_A_log, m_D, m_norm, m_w_out, kv_norm, kv_ada_w, kv_ada_b, w_kv, w_q, w_o,
              ffn_w_up, ffn_conv_w, ffn_conv_b, ffn_w_down, final_norm):
    p = dict(ada_w=ada_w, ada_b=ada_b, norm_mix=norm_mix, norm_ffn=norm_ffn, m_w_in=m_w_in,
             m_conv_w=m_conv_w, m_conv_b=m_conv_b, m_dt_bias=m_dt_bias, m_A_log=m_A_log, m_D=m_D,
             m_norm=m_norm, m_w_out=m_w_out, kv_norm=kv_norm, kv_ada_w=kv_ada_w, kv_ada_b=kv_ada_b,
             w_kv=w_kv, w_q=w_q, w_o=w_o, ffn_w_up=ffn_w_up, ffn_conv_w=ffn_conv_w,
             ffn_conv_b=ffn_conv_b, ffn_w_down=ffn_w_down, final_norm=final_norm)
    Bp, S, _ = x_prompt.shape
    Bs, T, _ = x_sample.shape
    ssm0 = jnp.zeros((N_A_LAYERS, Bp, M_HEADS, M_HEADDIM, M_D_STATE), state_ssm.dtype)
    conv0 = jnp.zeros((N_A_LAYERS, Bp, M_CONV - 1, M_CONV_DIM), x_prompt.dtype)
    ffn0 = jnp.zeros((DEPTH, Bp, FFN_CONV - 1, 2 * D_FF), x_prompt.dtype)
    y_p, ssm_p, conv_p, ffn_p, k_p, v_p = trunk(x_prompt, c_prompt, jnp.arange(S, dtype=jnp.int32),
                                                ssm0, conv0, ffn0, None, p)
    keep = min(WINDOW_MAX, S)
    k_p = k_p[:, S - keep:]
    v_p = v_p[:, S - keep:]
    pos_s = PAST_LEN + jnp.arange(T, dtype=jnp.int32)
    y_s, ssm_s, conv_s, ffn_s, k_s, v_s = trunk(x_sample, c_sample, pos_s, state_ssm, state_conv,
                                                state_ffn_conv, (cache_k, cache_v), p)
    return (y_p, y_s, ssm_p, ssm_s, conv_p, conv_s, ffn_p, ffn_s, k_p, k_s, v_p, v_s)
```

```python
import functools

import jax
import jax.numpy as jnp
from jax import lax
from jax.experimental import pallas as pl
from jax.experimental.pallas import tpu as pltpu

F32 = jnp.float32
BF16 = jnp.bfloat16

D_MODEL = 1024
DEPTH = 4
N_A = 2
M_D_INNER = 2048
M_HEADDIM = 64
M_HEADS = 32
M_NGROUPS = 4
M_D_STATE = 128
M_CONV = 4
M_GN = 512
M_CONV_DIM = 3072
HEAD_DIM = 64
HEADS_PER_GROUP = 16
DIL_PAIRS = ((128, 1), (512, 4), (2048, 16))
ATT_HEADS = 48
ATT_WIDTH = 3072
ATT_OUT = 1024
WINDOW_MAX = 2048
ATT_BLOCK = 128
ROPE_THETA = 10000.0
D_FF = 2816
FFN_CONV = 3
EPS = 1e-6
SSD_CHUNK = 128
PAST_LEN = 8192

LANES = 128
SUBLANES = 8
FFN_TF = 256
NEG_INF = float("-inf")


def _cparams(sem, vmem_mb=48):
    return pltpu.CompilerParams(dimension_semantics=sem,
                                vmem_limit_bytes=vmem_mb * 1024 * 1024)


def _silu(x):
    return x * jax.nn.sigmoid(x)


def _dot(a, b):
    return jnp.dot(a, b, preferred_element_type=F32)


def _dot_nt(a, b):
    return lax.dot_general(a, b, (((1,), (1,)), ((), ())), preferred_element_type=F32)


def _ada_kernel(c_ref, w_ref, b_ref, o_ref):
    cs = _silu(c_ref[...]).astype(BF16)
    o_ref[0] = _dot(cs, w_ref[0].astype(BF16)) + b_ref[0]


def _ada_linear(c, w, b, tn=1024):
    L, K, N = w.shape
    M = c.shape[0]
    return pl.pallas_call(
        _ada_kernel,
        grid=(L, N // tn),
        in_specs=[pl.BlockSpec((M, K), lambda l, j: (0, 0)),
                  pl.BlockSpec((1, K, tn), lambda l, j: (l, 0, j)),
                  pl.BlockSpec((1, 1, tn), lambda l, j: (l, 0, j))],
        out_specs=pl.BlockSpec((1, M, tn), lambda l, j: (l, 0, j)),
        out_shape=jax.ShapeDtypeStruct((L, M, N), F32),
        compiler_params=_cparams(("parallel", "parallel")),
        name="ada_linear",
    )(c, w, b)


def _rope_tile(acc, cos, sin):
    pieces = []
    first_half = (lax.broadcasted_iota(jnp.int32, (1, LANES), 1) & (HEAD_DIM - 1)) < (HEAD_DIM // 2)
    for c in range(acc.shape[1] // LANES):
        xc = acc[:, c * LANES:(c + 1) * LANES]
        partner = jnp.where(first_half,
                            pltpu.roll(xc, LANES - HEAD_DIM // 2, 1),
                            pltpu.roll(xc, HEAD_DIM // 2, 1))
        pieces.append(xc * cos + partner * sin)
    return jnp.concatenate(pieces, axis=1)


def _nmm_kernel(*refs, n_rope, has_extra):
    h_ref, g_ref, sh_ref, sc_ref, w_ref = refs[:5]
    k = 5
    if n_rope:
        cos_ref, sin_ref = refs[k:k + 2]
        k += 2
    if has_extra:
        we_ref = refs[k]
        k += 1
    o_ref = refs[k]
    k += 1
    if has_extra:
        oe_ref = refs[k]
        k += 1
    xn_ref = refs[k]
    j = pl.program_id(1)

    @pl.when(j == 0)
    def _():
        x = h_ref[...]
        ms = jnp.mean(x * x, axis=-1, keepdims=True)
        y = x * lax.rsqrt(ms + EPS) * g_ref[...]
        xb = (y * (1.0 + sc_ref[0]) + sh_ref[0]).astype(BF16)
        xn_ref[...] = xb
        if has_extra:
            oe_ref[...] = _dot(xb, we_ref[...])

    acc = _dot(xn_ref[...], w_ref[...])
    if n_rope:
        @pl.when(j < n_rope)
        def _():
            o_ref[...] = _rope_tile(acc, cos_ref[...], sin_ref[...]).astype(o_ref.dtype)

        @pl.when(j >= n_rope)
        def _():
            o_ref[...] = acc.astype(o_ref.dtype)
    else:
        o_ref[...] = acc.astype(o_ref.dtype)


def _norm_mod_matmul(h, g, shift, scale, w, *, tm, tn, out_dtype, rope=None, n_rope=0,
                     extra_w=None):
    R, Dm = h.shape
    N = w.shape[1]
    nmod, rm, _ = shift.shape
    tiles_per_mod = R // nmod // tm
    mod_spec = pl.BlockSpec((1, rm, Dm), lambda i, j: (i // tiles_per_mod, 0, 0))
    in_specs = [pl.BlockSpec((tm, Dm), lambda i, j: (i, 0)),
                pl.BlockSpec((1, Dm), lambda i, j: (0, 0)),
                mod_spec, mod_spec,
                pl.BlockSpec((Dm, tn), lambda i, j: (0, j))]
    args = [h, g, shift, scale, w]
    if n_rope:
        in_specs += [pl.BlockSpec((tm, LANES), lambda i, j: (i, 0))] * 2
        args += list(rope)
    out_specs = [pl.BlockSpec((tm, tn), lambda i, j: (i, j))]
    out_shape = [jax.ShapeDtypeStruct((R, N), out_dtype)]
    if extra_w is not None:
        in_specs.append(pl.BlockSpec(extra_w.shape, lambda i, j: (0, 0)))
        args.append(extra_w)
        out_specs.append(pl.BlockSpec((tm, extra_w.shape[1]), lambda i, j: (i, 0)))
        out_shape.append(jax.ShapeDtypeStruct((R, extra_w.shape[1]), F32))
    res = pl.pallas_call(
        functools.partial(_nmm_kernel, n_rope=n_rope, has_extra=extra_w is not None),
        grid=(R // tm, N // tn),
        in_specs=in_specs,
        out_specs=out_specs,
        out_shape=out_shape,
        scratch_shapes=[pltpu.VMEM((tm, Dm), BF16)],
        compiler_params=_cparams(("parallel", "arbitrary")),
        name="norm_mod_matmul",
    )(*args)
    return res if extra_w is not None else res[0]


def _ssd_kernel(zx_ref, dt_ref, conv0_ref, h0_ref, cw_ref, cb_ref, dtb_ref, alog_ref,
                dskip_ref, nw_ref, y_ref, hout_ref, convout_ref,
                xc_ref, tail_ref, hT_ref, *, Q, Tv):
    c = pl.program_id(1)
    KC = M_CONV - 1

    @pl.when(c == 0)
    def _():
        tail_ref[...] = conv0_ref[0]
        hT_ref[...] = h0_ref[0]

    xc_ref[0:SUBLANES, :] = tail_ref[...]
    xc_ref[SUBLANES:SUBLANES + Q, :] = zx_ref[0, :, M_D_INNER:M_D_INNER + M_CONV_DIM]
    acc = xc_ref[pl.ds(SUBLANES - KC, Q), :] * cw_ref[0:1, :]
    for k in range(1, M_CONV):
        acc = acc + xc_ref[pl.ds(SUBLANES - KC + k, Q), :] * cw_ref[k:k + 1, :]
    xact = _silu(acc + cb_ref[...])
    new_tail = xc_ref[pl.ds(Tv, SUBLANES), :]
    tail_ref[...] = new_tail
    convout_ref[0] = new_tail

    dt = dt_ref[0] + dtb_ref[...]
    dt = jnp.maximum(dt, 0.0) + jnp.log(1.0 + jnp.exp(-jnp.abs(dt)))
    if Tv < Q:
        row = lax.broadcasted_iota(jnp.int32, (Q, 1), 0)
        dt = jnp.where(row < Tv, dt, 0.0)
    a = dt * (-jnp.exp(alog_ref[...]))
    ri = lax.broadcasted_iota(jnp.int32, (Q, Q), 0)
    ci = lax.broadcasted_iota(jnp.int32, (Q, Q), 1)
    causal = ri >= ci
    tril = jnp.where(causal, 1.0, 0.0).astype(F32)
    acum = jnp.dot(tril, a, preferred_element_type=F32, precision=lax.Precision.HIGHEST)
    acum_t = acum.T
    lane_lo = lax.broadcasted_iota(jnp.int32, (1, LANES), 1) < M_HEADDIM

    for g in range(M_NGROUPS):
        gs = slice(g * M_GN, (g + 1) * M_GN)
        bsl = slice(M_D_INNER + g * M_D_STATE, M_D_INNER + (g + 1) * M_D_STATE)
        csl = slice(M_D_INNER + M_GN + g * M_D_STATE, M_D_INNER + M_GN + (g + 1) * M_D_STATE)
        b_f = xact[:, bsl]
        b_g = b_f.astype(BF16)
        c_g = xact[:, csl].astype(BF16)
        b_gt = b_f.T.astype(BF16)
        cb = _dot_nt(c_g, b_g)
        ydiag, e_parts, dec_parts = [], [], []
        for p in range(4):
            h0 = g * 8 + 2 * p
            xp = xact[:, h0 * M_HEADDIM:(h0 + 2) * M_HEADDIM]
            col0 = acum[:, h0:h0 + 1]
            col1 = acum[:, h0 + 1:h0 + 2]
            dtp = jnp.where(lane_lo, dt[:, h0:h0 + 1], dt[:, h0 + 1:h0 + 2])
            ap = jnp.where(lane_lo, col0, col1)
            xdt = xp * dtp
            e_parts.append(jnp.exp(ap))
            dec = jnp.exp(ap[Q - 1:Q, :] - ap)
            dec_parts.append((xdt * dec).astype(BF16))
            yp = None
            for hh, col, keep in ((h0, col0, lane_lo), (h0 + 1, col1, jnp.logical_not(lane_lo))):
                seg = col - acum_t[hh:hh + 1, :]
                lm = jnp.exp(jnp.where(causal, seg, NEG_INF))
                m = (cb * lm).astype(BF16)
                t = _dot(m, jnp.where(keep, xdt, 0.0).astype(BF16))
                yp = t if yp is None else yp + t
            ydiag.append(yp)
        e_g = jnp.concatenate(e_parts, axis=1)
        xdec_g = jnp.concatenate(dec_parts, axis=1)
        h_prev = hT_ref[g]
        y_off = _dot(c_g, h_prev.astype(BF16)) * e_g
        hT_ref[g] = h_prev * e_g[Q - 1:Q, :] + _dot(b_gt, xdec_g)
        y = jnp.concatenate(ydiag, axis=1) + y_off + xact[:, gs] * dskip_ref[:, gs]
        yg = y * _silu(zx_ref[0, :, gs])
        ms = jnp.mean(yg * yg, axis=-1, keepdims=True)
        y_ref[0, :, gs] = (yg * lax.rsqrt(ms + EPS) * nw_ref[:, gs]).astype(y_ref.dtype)

    @pl.when(c == pl.num_programs(1) - 1)
    def _():
        hout_ref[0] = hT_ref[...]


def _ssd_mixer(zx, dtr, conv0, h0t, cw, cb, dtb, alog, dskip, nw, *, Tv):
    B, L, _ = zx.shape
    Q = SSD_CHUNK
    nc = L // Q
    const2 = lambda b, c: (0, 0)
    return pl.pallas_call(
        functools.partial(_ssd_kernel, Q=Q, Tv=Tv),
        grid=(B, nc),
        in_specs=[pl.BlockSpec((1, Q, M_D_INNER + M_CONV_DIM), lambda b, c: (b, c, 0)),
                  pl.BlockSpec((1, Q, LANES), lambda b, c: (b, c, 0)),
                  pl.BlockSpec((1, SUBLANES, M_CONV_DIM), lambda b, c: (b, 0, 0)),
                  pl.BlockSpec((1, M_NGROUPS, M_D_STATE, M_GN), lambda b, c: (b, 0, 0, 0)),
                  pl.BlockSpec((M_CONV, M_CONV_DIM), const2),
                  pl.BlockSpec((1, M_CONV_DIM), const2),
                  pl.BlockSpec((1, LANES), const2),
                  pl.BlockSpec((1, LANES), const2),
                  pl.BlockSpec((1, M_D_INNER), const2),
                  pl.BlockSpec((1, M_D_INNER), const2)],
        out_specs=[pl.BlockSpec((1, Q, M_D_INNER), lambda b, c: (b, c, 0)),
                   pl.BlockSpec((1, M_NGROUPS, M_D_STATE, M_GN), lambda b, c: (b, 0, 0, 0)),
                   pl.BlockSpec((1, SUBLANES, M_CONV_DIM), lambda b, c: (b, 0, 0))],
        out_shape=[jax.ShapeDtypeStruct((B, L, M_D_INNER), BF16),
                   jax.ShapeDtypeStruct((B, M_NGROUPS, M_D_STATE, M_GN), F32),
                   jax.ShapeDtypeStruct((B, SUBLANES, M_CONV_DIM), F32)],
        scratch_shapes=[pltpu.VMEM((Q + SUBLANES, M_CONV_DIM), F32),
                        pltpu.VMEM((SUBLANES, M_CONV_DIM), F32),
                        pltpu.VMEM((M_NGROUPS, M_D_STATE, M_GN), F32)],
        compiler_params=_cparams(("parallel", "arbitrary")),
        name="ssd_mixer",
    )(zx, dtr, conv0, h0t, cw, cb, dtb, alog, dskip, nw)


def _proj_res_kernel(a_ref, w_ref, h_ref, gate_ref, o_ref):
    o_ref[...] = h_ref[...] + gate_ref[0] * _dot(a_ref[...], w_ref[...])


def _proj_residual(a, w, h, gate, *, tm):
    R, K = a.shape
    Dm = w.shape[1]
    nmod, rm, _ = gate.shape
    tiles_per_mod = R // nmod // tm
    return pl.pallas_call(
        _proj_res_kernel,
        grid=(R // tm,),
        in_specs=[pl.BlockSpec((tm, K), lambda i: (i, 0)),
                  pl.BlockSpec((K, Dm), lambda i: (0, 0)),
                  pl.BlockSpec((tm, Dm), lambda i: (i, 0)),
                  pl.BlockSpec((1, rm, Dm), lambda i: (i // tiles_per_mod, 0, 0))],
        out_specs=pl.BlockSpec((tm, Dm), lambda i: (i, 0)),
        out_shape=jax.ShapeDtypeStruct((R, Dm), F32),
        compiler_params=_cparams(("parallel",)),
        name="proj_residual",
    )(a, w, h, gate)


def _attn_out_kernel(o0_ref, o1_ref, o2_ref, l0_ref, l1_ref, l2_ref, e_ref, w_ref, h_ref,
                     gate_ref, out_ref):
    l0, l1, l2 = l0_ref[...], l1_ref[...], l2_ref[...]
    mx = jnp.maximum(jnp.maximum(l0, l1), l2)
    w0, w1, w2 = jnp.exp(l0 - mx), jnp.exp(l1 - mx), jnp.exp(l2 - mx)
    inv = 1.0 / (w0 + w1 + w2)
    comb = None
    for wg, og in ((w0, o0_ref), (w1, o1_ref), (w2, o2_ref)):
        alpha = jnp.dot(wg * inv, e_ref[...], preferred_element_type=F32,
                        precision=lax.Precision.HIGHEST)
        t = alpha * og[...]
        comb = t if comb is None else comb + t
    out_ref[...] = h_ref[...] + gate_ref[0] * _dot(comb.astype(BF16), w_ref[...])


def _attn_out(o_parts, lse_parts, expand, w, h, gate, *, tm):
    R, Dm = h.shape
    nmod, rm, _ = gate.shape
    tiles_per_mod = R // nmod // tm
    row = lambda i: (i, 0)
    return pl.pallas_call(
        _attn_out_kernel,
        grid=(R // tm,),
        in_specs=[pl.BlockSpec((tm, ATT_OUT), row)] * 3
        + [pl.BlockSpec((tm, LANES), row)] * 3
        + [pl.BlockSpec((LANES, ATT_OUT), lambda i: (0, 0)),
           pl.BlockSpec((ATT_OUT, Dm), lambda i: (0, 0)),
           pl.BlockSpec((tm, Dm), row),
           pl.BlockSpec((1, rm, Dm), lambda i: (i // tiles_per_mod, 0, 0))],
        out_specs=pl.BlockSpec((tm, Dm), row),
        out_shape=jax.ShapeDtypeStruct((R, Dm), F32),
        compiler_params=_cparams(("parallel",)),
        name="attn_out",
    )(*o_parts, *lse_parts, expand, w, h, gate)


def _ffn_kernel(*refs, tm, tf, nf, seq_tiles, step_mode, has_final):
    (h_ref, g_ref, sh_ref, sc_ref, gate_ref, wg_ref, wv_ref, cwg_ref, cwv_ref,
     cbg_ref, cbv_ref, wd_ref) = refs[:12]
    k = 12
    if step_mode:
        ppg_ref, ppv_ref = refs[k:k + 2]
        k += 2
    if has_final:
        fg_ref = refs[k]
        k += 1
    out_ref, ug_ref, uv_ref = refs[k:k + 3]
    k += 3
    xn_ref, acc_ref = refs[k:k + 2]
    k += 2
    if not step_mode:
        ubuf_ref, carry_ref = refs[k:k + 2]
    i = pl.program_id(0)
    f = pl.program_id(1)
    KC = FFN_CONV - 1

    @pl.when(f == 0)
    def _():
        x = h_ref[...]
        ms = jnp.mean(x * x, axis=-1, keepdims=True)
        y = x * lax.rsqrt(ms + EPS) * g_ref[...]
        xn_ref[...] = (y * (1.0 + sc_ref[0]) + sh_ref[0]).astype(BF16)
        acc_ref[...] = jnp.zeros_like(acc_ref)

    xn = xn_ref[...]

    def conv(u, part, cw_ref, cb_ref, u_out_ref, pp_ref):
        if step_mode:
            u_out_ref[...] = u
            t = lax.broadcasted_iota(jnp.int32, (tm, 1), 0) & (SUBLANES - 1)
            pp = pp_ref[...]
            acc = None
            for kk in range(FFN_CONV):
                s = KC - kk
                if s == 0:
                    term = u
                else:
                    term = jnp.where(t >= s, pltpu.roll(u, s, 0),
                                     pltpu.roll(pp, tm + s - SUBLANES, 0))
                term = term * cw_ref[kk:kk + 1, :]
                acc = term if acc is None else acc + term
            return acc + cb_ref[...]
        ubuf_ref[part, SUBLANES:SUBLANES + tm, :] = u

        @pl.when(i % seq_tiles == 0)
        def _():
            ubuf_ref[part, 0:SUBLANES, :] = jnp.zeros((SUBLANES, tf), F32)

        @pl.when(i % seq_tiles != 0)
        def _():
            ubuf_ref[part, 0:SUBLANES, :] = carry_ref[f, part]

        acc = ubuf_ref[part, pl.ds(SUBLANES - KC, tm), :] * cw_ref[0:1, :]
        for kk in range(1, FFN_CONV):
            acc = acc + ubuf_ref[part, pl.ds(SUBLANES - KC + kk, tm), :] * cw_ref[kk:kk + 1, :]
        last = ubuf_ref[part, tm:tm + SUBLANES, :]
        carry_ref[f, part] = last
        u_out_ref[0] = last
        return acc + cb_ref[...]

    cg = conv(_dot(xn, wg_ref[...]), 0, cwg_ref, cbg_ref, ug_ref, ppg_ref if step_mode else None)
    cv = conv(_dot(xn, wv_ref[...]), 1, cwv_ref, cbv_ref, uv_ref, ppv_ref if step_mode else None)
    act = (_silu(cg) * cv).astype(BF16)
    acc_ref[...] += _dot(act, wd_ref[...])

    @pl.when(f == nf - 1)
    def _():
        hn = h_ref[...] + gate_ref[0] * acc_ref[...]
        if has_final:
            ms = jnp.mean(hn * hn, axis=-1, keepdims=True)
            hn = hn * lax.rsqrt(ms + EPS) * fg_ref[...]
        out_ref[...] = hn


def _conv_ffn(h, g, shift, scale, gate, w_up, cw, cb, w_down, *, tm, seq_rows,
              prev=None, final_g=None):
    R, Dm = h.shape
    tf = FFN_TF
    nf = D_FF // tf
    nmod, rm, _ = shift.shape
    tiles_per_mod = R // nmod // tm
    step_mode = prev is not None
    seq_tiles = 1 if step_mode else seq_rows // tm
    ntiles = R // tm
    mod_spec = pl.BlockSpec((1, rm, Dm), lambda i, f: (i // tiles_per_mod, 0, 0))
    row_spec = pl.BlockSpec((tm, Dm), lambda i, f: (i, 0))
    in_specs = [row_spec, pl.BlockSpec((1, Dm), lambda i, f: (0, 0)), mod_spec, mod_spec, mod_spec,
                pl.BlockSpec((Dm, tf), lambda i, f: (0, f)),
                pl.BlockSpec((Dm, tf), lambda i, f: (0, nf + f)),
                pl.BlockSpec((FFN_CONV, tf), lambda i, f: (0, f)),
                pl.BlockSpec((FFN_CONV, tf), lambda i, f: (0, nf + f)),
                pl.BlockSpec((1, tf), lambda i, f: (0, f)),
                pl.BlockSpec((1, tf), lambda i, f: (0, nf + f)),
                pl.BlockSpec((tf, Dm), lambda i, f: (f, 0))]
    args = [h, g, shift, scale, gate, w_up, w_up, cw, cw, cb, cb, w_down]
    scratch = [pltpu.VMEM((tm, Dm), BF16), pltpu.VMEM((tm, Dm), F32)]
    if step_mode:
        in_specs += [pl.BlockSpec((tm, tf), lambda i, f: (i, f)),
                     pl.BlockSpec((tm, tf), lambda i, f: (i, nf + f))]
        args += [prev, prev]
        u_specs = [pl.BlockSpec((tm, tf), lambda i, f: (i, f))] * 2
        u_shape = jax.ShapeDtypeStruct((R, D_FF), F32)
    else:
        u_specs = [pl.BlockSpec((1, SUBLANES, tf), lambda i, f: (i, 0, f))] * 2
        u_shape = jax.ShapeDtypeStruct((ntiles, SUBLANES, D_FF), F32)
        scratch += [pltpu.VMEM((2, tm + SUBLANES, tf), F32),
                    pltpu.VMEM((nf, 2, SUBLANES, tf), F32)]
    if final_g is not None:
        in_specs.append(pl.BlockSpec((1, Dm), lambda i, f: (0, 0)))
        args.append(final_g)
    out, ug, uv = pl.pallas_call(
        functools.partial(_ffn_kernel, tm=tm, tf=tf, nf=nf, seq_tiles=seq_tiles,
                          step_mode=step_mode, has_final=final_g is not None),
        grid=(ntiles, nf),
        in_specs=in_specs,
        out_specs=[row_spec] + u_specs,
        out_shape=[jax.ShapeDtypeStruct((R, Dm), F32), u_shape, u_shape],
        scratch_shapes=scratch,
        compiler_params=_cparams(("arbitrary", "arbitrary")),
        name="conv_ffn",
    )(*args)
    return out, jnp.concatenate([ug, uv], axis=-1)


def _band_attn_kernel(q_ref, kc_ref, kp_ref, vc_ref, vp_ref, o_ref, lse_ref):
    c = pl.program_id(2)
    BL = ATT_BLOCK
    i = lax.broadcasted_iota(jnp.int32, (BL, 2 * BL), 0)
    j = lax.broadcasted_iota(jnp.int32, (BL, 2 * BL), 1)
    dist = BL + i - j
    valid = (dist >= 0) & (dist <= BL) & ((c > 0) | (j >= BL))
    lane_lo = lax.broadcasted_iota(jnp.int32, (1, LANES), 1) < HEAD_DIM
    lse_ref[...] = jnp.zeros_like(lse_ref)
    for p in range(HEADS_PER_GROUP // 2):
        ps = slice(p * LANES, (p + 1) * LANES)
        qp = q_ref[0, 0, :, ps]
        kc, kp = kc_ref[0, 0, :, ps], kp_ref[0, 0, :, ps]
        vc, vp = vc_ref[0, 0, :, ps], vp_ref[0, 0, :, ps]
        zero = jnp.zeros_like(qp)
        o_pair = None
        inv_pair = None
        for hh, keep in ((0, lane_lo), (1, jnp.logical_not(lane_lo))):
            qh = jnp.where(keep, qp, zero)
            s = jnp.concatenate([_dot_nt(qh, kp), _dot_nt(qh, kc)], axis=1)
            s = jnp.where(valid, s, NEG_INF)
            m = jnp.max(s, axis=-1, keepdims=True)
            pr = jnp.exp(s - m)
            l = jnp.sum(pr, axis=-1, keepdims=True)
            prb = pr.astype(BF16)
            o = (_dot(prb[:, :BL], jnp.where(keep, vp, zero))
                 + _dot(prb[:, BL:], jnp.where(keep, vc, zero)))
            inv = jnp.where(keep, 1.0 / l, 0.0)
            o_pair = o if o_pair is None else o_pair + o
            inv_pair = inv if inv_pair is None else inv_pair + inv
            h = 2 * p + hh
            lse_ref[0, 0, :, h:h + 1] = m + jnp.log(l)
        o_ref[0, 0, :, ps] = o_pair * inv_pair


def _band_attention(q, k, v):
    B, d, n, W = q.shape
    nb = n // ATT_BLOCK
    cur = pl.BlockSpec((1, 1, ATT_BLOCK, W), lambda b, r, c: (b, r, c, 0))
    prv = pl.BlockSpec((1, 1, ATT_BLOCK, W), lambda b, r, c: (b, r, jnp.maximum(c - 1, 0), 0))
    return pl.pallas_call(
        _band_attn_kernel,
        grid=(B, d, nb),
        in_specs=[cur, cur, prv, cur, prv],
        out_specs=[cur, pl.BlockSpec((1, 1, ATT_BLOCK, LANES), lambda b, r, c: (b, r, c, 0))],
        out_shape=[jax.ShapeDtypeStruct((B, d, n, W), F32),
                   jax.ShapeDtypeStruct((B, d, n, LANES), F32)],
        compiler_params=_cparams(("parallel", "parallel", "arbitrary")),
        name="band_attention",
    )(q, k, k, v, v)


def _dec_attn_kernel(q_ref, kn_ref, vn_ref, k0_ref, v0_ref, k1_ref, v1_ref, k2_ref, v2_ref,
                     o_ref, *, T):
    half = pl.program_id(1)
    HG = HEADS_PER_GROUP
    mi = lax.broadcasted_iota(jnp.int32, (ATT_BLOCK, 1, 1), 0)
    ui = lax.broadcasted_iota(jnp.int32, (T, 1, 1), 0)
    for tt in range(4):
        t = half * 4 + tt
        parts = []
        for g, (win, dil) in enumerate(DIL_PAIRS):
            hs = slice(g * HG, (g + 1) * HG)
            q = q_ref[0, pl.ds(t, 1), hs, :]
            if g == 0:
                kc, vc = k0_ref[0, :, 0], v0_ref[0, :, 0]
                cmask = mi >= t
            elif g == 1:
                kc, vc = k1_ref[0, :, tt], v1_ref[0, :, tt]
                cmask = (mi >= 1) | (half == 0)
            else:
                kc, vc = k2_ref[0, :, tt], v2_ref[0, :, tt]
                cmask = None
            kn, vn = kn_ref[0, :, hs, :], vn_ref[0, :, hs, :]
            nmask = (ui <= t) & (((t - ui) & (dil - 1)) == 0)
            sc = jnp.sum(kc * q, axis=-1, keepdims=True)
            if cmask is not None:
                sc = jnp.where(cmask, sc, NEG_INF)
            sn = jnp.where(nmask, jnp.sum(kn * q, axis=-1, keepdims=True), NEG_INF)
            m = jnp.maximum(jnp.max(sc, axis=0, keepdims=True), jnp.max(sn, axis=0, keepdims=True))
            pc = jnp.exp(sc - m)
            pn = jnp.exp(sn - m)
            l = jnp.sum(pc, axis=0, keepdims=True) + jnp.sum(pn, axis=0, keepdims=True)
            o = jnp.sum(pc * vc, axis=0, keepdims=True) + jnp.sum(pn * vn, axis=0, keepdims=True)
            parts.append((m, l, o))
        mx = jnp.maximum(jnp.maximum(parts[0][0], parts[1][0]), parts[2][0])
        num = None
        den = None
        for m, l, o in parts:
            w = jnp.exp(m - mx)
            num = w * o if num is None else num + w * o
            den = w * l if den is None else den + w * l
        o_ref[0, pl.ds(tt, 1)] = num / den


def _decode_attention(q4, kn4, vn4, cache_k, cache_v):
    B, T, H, E = q4.shape
    P = cache_k.shape[1]
    HG = HEADS_PER_GROUP
    views = []
    specs = []
    for g, (win, dil) in enumerate(DIL_PAIRS):
        shp = (B, P // dil, dil, H, E)
        last = P // dil // ATT_BLOCK - 1
        if g == 0:
            blk, imap = (1, ATT_BLOCK, 1, HG, E), functools.partial(
                lambda b, s, last: (b, last, 0, 0, 0), last=last)
        elif g == 1:
            blk, imap = (1, ATT_BLOCK, 4, HG, E), functools.partial(
                lambda b, s, last: (b, last, 0, 1, 0), last=last)
        else:
            blk, imap = (1, ATT_BLOCK, 4, HG, E), (lambda b, s: (b, 0, s, 2, 0))
        for cache in (cache_k, cache_v):
            views.append(cache.reshape(shp))
            specs.append(pl.BlockSpec(blk, imap))
    full = pl.BlockSpec((1, T, H, E), lambda b, s: (b, 0, 0, 0))
    return pl.pallas_call(
        functools.partial(_dec_attn_kernel, T=T),
        grid=(B, T // 4),
        in_specs=[full, full, full] + specs,
        out_specs=pl.BlockSpec((1, 4, HG, E), lambda b, s: (b, s, 0, 0)),
        out_shape=jax.ShapeDtypeStruct((B, T, HG, E), F32),
        compiler_params=_cparams(("parallel", "arbitrary"), vmem_mb=56),
        name="decode_attention",
    )(q4, kn4, vn4, *views)


def _rope_tables(pos):
    half = HEAD_DIM // 2
    inv = ROPE_THETA ** (-jnp.arange(half, dtype=F32) * (2.0 / HEAD_DIM))
    ang = pos.astype(F32)[:, None] * inv[None, :]
    cos, sin = jnp.cos(ang), jnp.sin(ang)
    cos_t = jnp.concatenate([cos, cos, cos, cos], axis=1)
    sin_t = jnp.concatenate([-sin, sin, -sin, sin], axis=1)
    return cos_t, sin_t


def _prep_params(p):
    w = {}
    w_in = p['m_w_in']
    w['w_in_zx'] = w_in[:, :, :M_D_INNER + M_CONV_DIM].astype(BF16)
    w['w_in_dt'] = jnp.pad(w_in[:, :, M_D_INNER + M_CONV_DIM:],
                           ((0, 0), (0, 0), (0, LANES - M_HEADS))).astype(BF16)
    w['dt_bias'] = jnp.pad(p['m_dt_bias'], ((0, 0), (0, LANES - M_HEADS)))[:, None, :]
    w['a_log'] = jnp.pad(p['m_A_log'], ((0, 0), (0, LANES - M_HEADS)))[:, None, :]
    w['d_skip'] = jnp.repeat(p['m_D'], M_HEADDIM, axis=1)[:, None, :]
    w['m_norm'] = p['m_norm'][:, None, :]
    w['conv_w'] = p['m_conv_w']
    w['conv_b'] = p['m_conv_b'][:, None, :]
    w['w_out'] = p['m_w_out'].astype(BF16)
    w['w_q'] = (p['w_q'] * (HEAD_DIM ** -0.5)).astype(BF16)
    w['w_kv'] = p['w_kv'].astype(BF16)
    w['w_o'] = p['w_o'].astype(BF16)
    w['ffn_up'] = p['ffn_w_up'].astype(BF16)
    w['ffn_down'] = p['ffn_w_down'].astype(BF16)
    w['ffn_cw'] = p['ffn_conv_w']
    w['ffn_cb'] = p['ffn_conv_b'][:, None, :]
    head = jnp.arange(ATT_OUT) // HEAD_DIM
    w['expand'] = (jnp.arange(LANES)[:, None] == head[None, :]).astype(F32)
    return w


def _state_to_kernel(s):
    B = s.shape[0]
    return s.reshape(B, M_NGROUPS, 8, M_HEADDIM, M_D_STATE).transpose(0, 1, 4, 2, 3).reshape(
        B, M_NGROUPS, M_D_STATE, M_GN)


def _state_from_kernel(s):
    B = s.shape[0]
    return s.reshape(B, M_NGROUPS, M_D_STATE, 8, M_HEADDIM).transpose(0, 1, 3, 4, 2).reshape(
        B, M_HEADS, M_HEADDIM, M_D_STATE)


def _split_residues(x, dil):
    B, S, W = x.shape
    return x.reshape(B, S // dil, dil, W).transpose(0, 2, 1, 3)


def _merge_residues(x):
    B, dil, n, W = x.shape
    return x.transpose(0, 2, 1, 3).reshape(B * n * dil, W)


def _trunk(x, mods, kvmod, pos, ssm0, conv0, ffn0, kv_past, p, w, *, tm, step):
    B, T, Dm = x.shape
    R = B * T
    h = x.reshape(R, Dm)
    cos_t, sin_t = _rope_tables(pos)
    cos_r = jnp.tile(cos_t, (B, 1))
    sin_r = jnp.tile(sin_t, (B, 1))
    ssm_out, conv_out, ffn_out = [], [], []
    k_new = v_new = None
    kv_split = None
    for i in range(DEPTH):
        sh1, sc1, g1, sh2, sc2, g2 = mods[i]
        if i < N_A:
            zx, dtr = _norm_mod_matmul(h, p['norm_mix'][i][None], sh1, sc1, w['w_in_zx'][i],
                                       tm=tm, tn=1024, out_dtype=F32, extra_w=w['w_in_dt'][i])
            zx = zx.reshape(B, T, -1)
            dtr = dtr.reshape(B, T, LANES)
            if step:
                padr = ((0, 0), (0, SSD_CHUNK - T), (0, 0))
                zx, dtr = jnp.pad(zx, padr), jnp.pad(dtr, padr)
            conv_prev = jnp.pad(conv0[i], ((0, 0), (SUBLANES - (M_CONV - 1), 0), (0, 0)))
            y, h_t, conv_tail = _ssd_mixer(
                zx, dtr, conv_prev, _state_to_kernel(ssm0[i]), w['conv_w'][i], w['conv_b'][i],
                w['dt_bias'][i], w['a_log'][i], w['d_skip'][i], w['m_norm'][i], Tv=min(T, SSD_CHUNK))
            ssm_out.append(_state_from_kernel(h_t))
            conv_out.append(conv_tail[:, SUBLANES - (M_CONV - 1):])
            y = y[:, :T].reshape(R, M_D_INNER)
            h = _proj_residual(y, w['w_out'][i], h, g1, tm=min(tm, 512))
        else:
            jb = i - N_A
            q = _norm_mod_matmul(h, p['norm_mix'][i][None], sh1, sc1, w['w_q'][jb], tm=tm, tn=1024,
                                 out_dtype=F32 if step else BF16, rope=(cos_r, sin_r),
                                 n_rope=ATT_WIDTH // 1024)
            if step:
                o = _decode_attention(q.reshape(B, T, ATT_HEADS, HEAD_DIM), k_new, v_new, *kv_past)
                h = _proj_residual(o.reshape(R, ATT_OUT).astype(BF16), w['w_o'][jb], h, g1, tm=tm)
            else:
                q = q.reshape(B, T, ATT_WIDTH)
                o_parts, lse_parts = [], []
                for g, (win, dil) in enumerate(DIL_PAIRS):
                    gs = slice(g * ATT_OUT, (g + 1) * ATT_OUT)
                    og, lg = _band_attention(_split_residues(q[:, :, gs], dil), *kv_split[g])
                    o_parts.append(_merge_residues(og))
                    lse_parts.append(_merge_residues(lg))
                h = _attn_out(o_parts, lse_parts, w['expand'], w['w_o'][jb], h, g1, tm=min(tm, 512))
        last = i == DEPTH - 1
        fin = p['final_norm'][None] if last else None
        if step:
            prev = jnp.pad(ffn0[i], ((0, 0), (SUBLANES - (FFN_CONV - 1), 0), (0, 0))).reshape(R, 2 * D_FF)
            h, u = _conv_ffn(h, p['norm_ffn'][i][None], sh2, sc2, g2, w['ffn_up'][i], w['ffn_cw'][i],
                             w['ffn_cb'][i], w['ffn_down'][i], tm=tm, seq_rows=T, prev=prev, final_g=fin)
            ffn_out.append(u.reshape(B, T, 2 * D_FF)[:, T - (FFN_CONV - 1):])
        else:
            h, u = _conv_ffn(h, p['norm_ffn'][i][None], sh2, sc2, g2, w['ffn_up'][i], w['ffn_cw'][i],
                             w['ffn_cb'][i], w['ffn_down'][i], tm=tm, seq_rows=T, final_g=fin)
            seq_tiles = T // tm
            ffn_out.append(u[seq_tiles - 1::seq_tiles, SUBLANES - (FFN_CONV - 1):])
        if i == N_A - 1:
            ksh, ksc = kvmod
            kv = _norm_mod_matmul(h, p['kv_norm'][None], ksh, ksc, w['w_kv'], tm=tm, tn=1024,
                                  out_dtype=F32, rope=(cos_r, sin_r), n_rope=ATT_WIDTH // 1024)
            kv = kv.reshape(B, T, 2, ATT_HEADS, HEAD_DIM)
            k_new, v_new = kv[:, :, 0], kv[:, :, 1]
            if not step:
                kb = k_new.reshape(B, T, ATT_WIDTH).astype(BF16)
                vb = v_new.reshape(B, T, ATT_WIDTH).astype(BF16)
                kv_split = []
                for g, (win, dil) in enumerate(DIL_PAIRS):
                    gs = slice(g * ATT_OUT, (g + 1) * ATT_OUT)
                    kv_split.append((_split_residues(kb[:, :, gs], dil),
                                     _split_residues(vb[:, :, gs], dil)))
    return (h.reshape(B, T, Dm), jnp.stack(ssm_out), jnp.stack(conv_out), jnp.stack(ffn_out),
            k_new, v_new)


def kernel(x_prompt, x_sample, state_ssm, state_conv, state_ffn_conv, cache_k, cache_v, c_prompt, c_sample, ada_w, ada_b, norm_mix, norm_ffn, m_w_in, m_conv_w, m_conv_b, m_dt_bias, m_A_log, m_D, m_norm, m_w_out, kv_norm, kv_ada_w, kv_ada_b, w_kv, w_q, w_o, ffn_w_up, ffn_conv_w, ffn_conv_b, ffn_w_down, final_norm):
    p = dict(norm_mix=norm_mix, norm_ffn=norm_ffn, m_w_in=m_w_in, m_conv_w=m_conv_w,
             m_conv_b=m_conv_b, m_dt_bias=m_dt_bias, m_A_log=m_A_log, m_D=m_D, m_norm=m_norm,
             m_w_out=m_w_out, kv_norm=kv_norm, w_kv=w_kv, w_q=w_q, w_o=w_o, ffn_w_up=ffn_w_up,
             ffn_conv_w=ffn_conv_w, ffn_conv_b=ffn_conv_b, ffn_w_down=ffn_w_down,
             final_norm=final_norm)
    w = _prep_params(p)
    Bp, S, Dm = x_prompt.shape
    Bs, T, _ = x_sample.shape

    nrow = Bp + Bs
    npad = -(-nrow // SUBLANES) * SUBLANES
    c_all = jnp.pad(jnp.concatenate([c_prompt, c_sample], axis=0), ((0, npad - nrow), (0, 0)))
    mod = _ada_linear(c_all, ada_w, ada_b[:, None, :])
    kvm = _ada_linear(c_all, kv_ada_w[None], kv_ada_b[None, None, :])[0]

    def mod_prompt(m):
        return [a[:, None, :] for a in jnp.split(m, m.shape[-1] // Dm, axis=-1)]

    def mod_sample(m):
        return [jnp.repeat(a, T, axis=0)[None] for a in jnp.split(m, m.shape[-1] // Dm, axis=-1)]

    mods_p = [mod_prompt(mod[i, :Bp]) for i in range(DEPTH)]
    mods_s = [mod_sample(mod[i, Bp:nrow]) for i in range(DEPTH)]
    kvm_p = mod_prompt(kvm[:Bp])
    kvm_s = mod_sample(kvm[Bp:nrow])

    ssm0 = jnp.zeros((N_A, Bp, M_HEADS, M_HEADDIM, M_D_STATE), state_ssm.dtype)
    conv0 = jnp.zeros((N_A, Bp, M_CONV - 1, M_CONV_DIM), x_prompt.dtype)
    y_p, ssm_p, conv_p, ffn_p, k_p, v_p = _trunk(
        x_prompt, mods_p, kvm_p, jnp.arange(S, dtype=jnp.int32), ssm0, conv0, None, None, p, w,
        tm=1024, step=False)
    keep = min(WINDOW_MAX, S)
    k_p = k_p[:, S - keep:]
    v_p = v_p[:, S - keep:]

    y_s, ssm_s, conv_s, ffn_s, k_s, v_s = _trunk(
        x_sample, mods_s, kvm_s, PAST_LEN + jnp.arange(T, dtype=jnp.int32), state_ssm,
        state_conv, state_ffn_conv, (cache_k, cache_v), p, w, tm=Bs * T, step=True)
    return (y_p, y_s, ssm_p, ssm_s, conv_p, conv_s, ffn_p, ffn_s, k_p, k_s, v_p, v_s)
```

```python
import functools

import jax
import jax.numpy as jnp
from jax import lax
from jax.experimental import pallas as pl
from jax.experimental.pallas import tpu as pltpu

F32 = jnp.float32
BF16 = jnp.bfloat16

D_MODEL = 1024
DEPTH = 4
N_A = 2
M_D_INNER = 2048
M_HEADDIM = 64
M_HEADS = 32
M_NGROUPS = 4
M_D_STATE = 128
M_CONV = 4
M_GN = 512
M_CONV_DIM = 3072
HEAD_DIM = 64
HEADS_PER_GROUP = 16
DIL_PAIRS = ((128, 1), (512, 4), (2048, 16))
ATT_HEADS = 48
ATT_WIDTH = 3072
ATT_OUT = 1024
WINDOW_MAX = 2048
ATT_BLOCK = 128
ROPE_THETA = 10000.0
D_FF = 2816
FFN_CONV = 3
EPS = 1e-6
SSD_CHUNK = 128
PAST_LEN = 8192

LANES = 128
SUBLANES = 8
FFN_TC = 256
SPLIT_TM = 512
NEG_INF = float("-inf")


def _cparams(sem, vmem_mb=48):
    return pltpu.CompilerParams(dimension_semantics=sem,
                                vmem_limit_bytes=vmem_mb * 1024 * 1024)


def _silu(x):
    return x * jax.nn.sigmoid(x)


def _dot(a, b):
    return jnp.dot(a, b, preferred_element_type=F32)


def _dot_nt(a, b):
    return lax.dot_general(a, b, (((1,), (1,)), ((), ())), preferred_element_type=F32)


def _ada_kernel(c_ref, w_ref, b_ref, o_ref):
    cs = _silu(c_ref[...]).astype(BF16)
    o_ref[0] = _dot(cs, w_ref[0].astype(BF16)) + b_ref[0]


def _ada_linear(c, w, b, tn=1024):
    L, K, N = w.shape
    M = c.shape[0]
    return pl.pallas_call(
        _ada_kernel,
        grid=(L, N // tn),
        in_specs=[pl.BlockSpec((M, K), lambda l, j: (0, 0)),
                  pl.BlockSpec((1, K, tn), lambda l, j: (l, 0, j)),
                  pl.BlockSpec((1, 1, tn), lambda l, j: (l, 0, j))],
        out_specs=pl.BlockSpec((1, M, tn), lambda l, j: (l, 0, j)),
        out_shape=jax.ShapeDtypeStruct((L, M, N), F32),
        compiler_params=_cparams(("parallel", "parallel")),
        name="ada_linear",
    )(c, w, b)


def _rope_tile(acc, cos, sin):
    pieces = []
    first_half = (lax.broadcasted_iota(jnp.int32, (1, LANES), 1) & (HEAD_DIM - 1)) < (HEAD_DIM // 2)
    for c in range(acc.shape[1] // LANES):
        xc = acc[:, c * LANES:(c + 1) * LANES]
        partner = jnp.where(first_half,
                            pltpu.roll(xc, LANES - HEAD_DIM // 2, 1),
                            pltpu.roll(xc, HEAD_DIM // 2, 1))
        pieces.append(xc * cos + partner * sin)
    return jnp.concatenate(pieces, axis=1)


def _nmm_kernel(*refs, n_rope, has_extra):
    h_ref, g_ref, sh_ref, sc_ref, w_ref = refs[:5]
    k = 5
    if n_rope:
        cos_ref, sin_ref = refs[k:k + 2]
        k += 2
    if has_extra:
        we_ref = refs[k]
        k += 1
    o_ref = refs[k]
    k += 1
    if has_extra:
        oe_ref = refs[k]
        k += 1
    xn_ref = refs[k]
    j = pl.program_id(1)

    @pl.when(j == 0)
    def _():
        x = h_ref[...]
        ms = jnp.mean(x * x, axis=-1, keepdims=True)
        y = x * lax.rsqrt(ms + EPS) * g_ref[...]
        xb = (y * (1.0 + sc_ref[0]) + sh_ref[0]).astype(BF16)
        xn_ref[...] = xb
        if has_extra:
            oe_ref[...] = _dot(xb, we_ref[...])

    acc = _dot(xn_ref[...], w_ref[...])
    if n_rope:
        @pl.when(j < n_rope)
        def _():
            o_ref[...] = _rope_tile(acc, cos_ref[...], sin_ref[...]).astype(o_ref.dtype)

        @pl.when(j >= n_rope)
        def _():
            o_ref[...] = acc.astype(o_ref.dtype)
    else:
        o_ref[...] = acc.astype(o_ref.dtype)


def _norm_mod_matmul(h, g, shift, scale, w, *, tm, tn, out_dtype, rope=None, n_rope=0,
                     extra_w=None):
    R, Dm = h.shape
    N = w.shape[1]
    nmod, rm, _ = shift.shape
    tiles_per_mod = R // nmod // tm
    mod_spec = pl.BlockSpec((1, rm, Dm), lambda i, j: (i // tiles_per_mod, 0, 0))
    in_specs = [pl.BlockSpec((tm, Dm), lambda i, j: (i, 0)),
                pl.BlockSpec((1, Dm), lambda i, j: (0, 0)),
                mod_spec, mod_spec,
                pl.BlockSpec((Dm, tn), lambda i, j: (0, j))]
    args = [h, g, shift, scale, w]
    if n_rope:
        in_specs += [pl.BlockSpec((tm, LANES), lambda i, j: (i, 0))] * 2
        args += list(rope)
    out_specs = [pl.BlockSpec((tm, tn), lambda i, j: (i, j))]
    out_shape = [jax.ShapeDtypeStruct((R, N), out_dtype)]
    if extra_w is not None:
        in_specs.append(pl.BlockSpec(extra_w.shape, lambda i, j: (0, 0)))
        args.append(extra_w)
        out_specs.append(pl.BlockSpec((tm, extra_w.shape[1]), lambda i, j: (i, 0)))
        out_shape.append(jax.ShapeDtypeStruct((R, extra_w.shape[1]), F32))
    res = pl.pallas_call(
        functools.partial(_nmm_kernel, n_rope=n_rope, has_extra=extra_w is not None),
        grid=(R // tm, N // tn),
        in_specs=in_specs,
        out_specs=out_specs,
        out_shape=out_shape,
        scratch_shapes=[pltpu.VMEM((tm, Dm), BF16)],
        compiler_params=_cparams(("parallel", "arbitrary")),
        name="norm_mod_matmul",
    )(*args)
    return res if extra_w is not None else res[0]


def _nmm_split_kernel(*refs, n_rope, n_out, tm):
    h_ref, g_ref, sh_ref, sc_ref, w_ref, cos_ref, sin_ref = refs[:7]
    outs = refs[7:7 + n_out]
    xf_ref, xn_ref = refs[7 + n_out:]
    j = pl.program_id(1)
    ng = len(DIL_PAIRS)

    @pl.when(j == 0)
    def _():
        x = h_ref[...]
        ms = jnp.mean(x * x, axis=-1, keepdims=True)
        y = x * lax.rsqrt(ms + EPS) * g_ref[...]
        xn = y * (1.0 + sc_ref[0]) + sh_ref[0]
        nlc = xn.shape[1] // LANES
        for lc in range(nlc):
            xf_ref[lc] = xn[:, lc * LANES:(lc + 1) * LANES]
        for g, (win, d) in enumerate(DIL_PAIRS):
            rows = tm // d
            if d == 1:
                xn_ref[g] = xn.astype(BF16)
            else:
                for r in range(d):
                    for lc in range(nlc):
                        xn_ref[g, r * rows:(r + 1) * rows, lc * LANES:(lc + 1) * LANES] = (
                            xf_ref.at[lc][pl.ds(r, rows, stride=d), :].astype(BF16))

    acc = _dot(xn_ref[j % ng], w_ref[...])
    for k in range(n_out):
        @pl.when(j == k)
        def _(k=k):
            val = _rope_tile(acc, cos_ref[0], sin_ref[0]) if k < n_rope else acc
            d = DIL_PAIRS[k % ng][1]
            rows = tm // d
            for r in range(d):
                outs[k][0, r] = val[r * rows:(r + 1) * rows].astype(outs[k].dtype)


def _split_rows(x, d, tm):
    T, C = x.shape
    return x.reshape(T // tm, tm // d, d, C).transpose(0, 2, 1, 3).reshape(T, C)


def _norm_mod_matmul_split(h, g, shift, scale, w, cos_t, sin_t, *, B, tm, n_rope):
    R, Dm = h.shape
    S = R // B
    W = ATT_OUT
    n_out = w.shape[1] // W
    seq_tiles = S // tm
    ng = len(DIL_PAIRS)
    cos_g = jnp.stack([_split_rows(cos_t, d, tm) for _, d in DIL_PAIRS])
    sin_g = jnp.stack([_split_rows(sin_t, d, tm) for _, d in DIL_PAIRS])
    mod_spec = pl.BlockSpec((1, 1, Dm), lambda i, j: (i // seq_tiles, 0, 0))
    tab_spec = pl.BlockSpec((1, tm, LANES), lambda i, j: (j % ng, i % seq_tiles, 0))
    out_specs, out_shape = [], []
    for k in range(n_out):
        d = DIL_PAIRS[k % ng][1]
        out_specs.append(pl.BlockSpec((1, d, tm // d, W),
                                      lambda i, j: (i // seq_tiles, 0, i % seq_tiles, 0)))
        out_shape.append(jax.ShapeDtypeStruct((B, d, S // d, W), BF16))
    return pl.pallas_call(
        functools.partial(_nmm_split_kernel, n_rope=n_rope, n_out=n_out, tm=tm),
        grid=(R // tm, n_out),
        in_specs=[pl.BlockSpec((tm, Dm), lambda i, j: (i, 0)),
                  pl.BlockSpec((1, Dm), lambda i, j: (0, 0)),
                  mod_spec, mod_spec,
                  pl.BlockSpec((Dm, W), lambda i, j: (0, j)),
                  tab_spec, tab_spec],
        out_specs=out_specs,
        out_shape=out_shape,
        scratch_shapes=[pltpu.VMEM((Dm // LANES, tm, LANES), F32), pltpu.VMEM((ng, tm, Dm), BF16)],
        compiler_params=_cparams(("parallel", "arbitrary")),
        name="norm_mod_matmul_split",
    )(h, g, shift, scale, w, cos_g, sin_g)


def _ssd_kernel(zx_ref, dt_ref, conv0_ref, h0_ref, cw_ref, cb_ref, dtb_ref, alog_ref,
                dskip_ref, nw_ref, y_ref, hout_ref, convout_ref,
                xc_ref, tail_ref, hT_ref, *, Q, Tv):
    c = pl.program_id(1)
    KC = M_CONV - 1

    @pl.when(c == 0)
    def _():
        tail_ref[...] = conv0_ref[0]
        hT_ref[...] = h0_ref[0]

    xc_ref[0:SUBLANES, :] = tail_ref[...]
    xc_ref[SUBLANES:SUBLANES + Q, :] = zx_ref[0, :, M_D_INNER:M_D_INNER + M_CONV_DIM]
    acc = xc_ref[pl.ds(SUBLANES - KC, Q), :] * cw_ref[0:1, :]
    for k in range(1, M_CONV):
        acc = acc + xc_ref[pl.ds(SUBLANES - KC + k, Q), :] * cw_ref[k:k + 1, :]
    xact = _silu(acc + cb_ref[...])
    new_tail = xc_ref[pl.ds(Tv, SUBLANES), :]
    tail_ref[...] = new_tail
    convout_ref[0] = new_tail

    dt = dt_ref[0] + dtb_ref[...]
    dt = jnp.maximum(dt, 0.0) + jnp.log(1.0 + jnp.exp(-jnp.abs(dt)))
    if Tv < Q:
        row = lax.broadcasted_iota(jnp.int32, (Q, 1), 0)
        dt = jnp.where(row < Tv, dt, 0.0)
    a = dt * (-jnp.exp(alog_ref[...]))
    ri = lax.broadcasted_iota(jnp.int32, (Q, Q), 0)
    ci = lax.broadcasted_iota(jnp.int32, (Q, Q), 1)
    causal = ri >= ci
    tril = jnp.where(causal, 1.0, 0.0).astype(F32)
    acum = jnp.dot(tril, a, preferred_element_type=F32, precision=lax.Precision.HIGHEST)
    acum_t = acum.T
    lane_lo = lax.broadcasted_iota(jnp.int32, (1, LANES), 1) < M_HEADDIM

    for g in range(M_NGROUPS):
        gs = slice(g * M_GN, (g + 1) * M_GN)
        bsl = slice(M_D_INNER + g * M_D_STATE, M_D_INNER + (g + 1) * M_D_STATE)
        csl = slice(M_D_INNER + M_GN + g * M_D_STATE, M_D_INNER + M_GN + (g + 1) * M_D_STATE)
        b_f = xact[:, bsl]
        b_g = b_f.astype(BF16)
        c_g = xact[:, csl].astype(BF16)
        b_gt = b_f.T.astype(BF16)
        cb = _dot_nt(c_g, b_g)
        ydiag, e_parts, dec_parts = [], [], []
        for p in range(4):
            h0 = g * 8 + 2 * p
            xp = xact[:, h0 * M_HEADDIM:(h0 + 2) * M_HEADDIM]
            col0 = acum[:, h0:h0 + 1]
            col1 = acum[:, h0 + 1:h0 + 2]
            dtp = jnp.where(lane_lo, dt[:, h0:h0 + 1], dt[:, h0 + 1:h0 + 2])
            ap = jnp.where(lane_lo, col0, col1)
            xdt = xp * dtp
            e_parts.append(jnp.exp(ap))
            dec = jnp.exp(ap[Q - 1:Q, :] - ap)
            dec_parts.append((xdt * dec).astype(BF16))
            yp = None
            for hh, col, keep in ((h0, col0, lane_lo), (h0 + 1, col1, jnp.logical_not(lane_lo))):
                seg = col - acum_t[hh:hh + 1, :]
                lm = jnp.exp(jnp.where(causal, seg, NEG_INF))
                m = (cb * lm).astype(BF16)
                t = _dot(m, jnp.where(keep, xdt, 0.0).astype(BF16))
                yp = t if yp is None else yp + t
            ydiag.append(yp)
        e_g = jnp.concatenate(e_parts, axis=1)
        xdec_g = jnp.concatenate(dec_parts, axis=1)
        h_prev = hT_ref[g]
        y_off = _dot(c_g, h_prev.astype(BF16)) * e_g
        hT_ref[g] = h_prev * e_g[Q - 1:Q, :] + _dot(b_gt, xdec_g)
        y = jnp.concatenate(ydiag, axis=1) + y_off + xact[:, gs] * dskip_ref[:, gs]
        yg = y * _silu(zx_ref[0, :, gs])
        ms = jnp.mean(yg * yg, axis=-1, keepdims=True)
        y_ref[0, :, gs] = (yg * lax.rsqrt(ms + EPS) * nw_ref[:, gs]).astype(y_ref.dtype)

    @pl.when(c == pl.num_programs(1) - 1)
    def _():
        hout_ref[0] = hT_ref[...]


def _ssd_mixer(zx, dtr, conv0, h0t, cw, cb, dtb, alog, dskip, nw, *, Tv):
    B, L, _ = zx.shape
    Q = SSD_CHUNK
    nc = L // Q
    const2 = lambda b, c: (0, 0)
    return pl.pallas_call(
        functools.partial(_ssd_kernel, Q=Q, Tv=Tv),
        grid=(B, nc),
        in_specs=[pl.BlockSpec((1, Q, M_D_INNER + M_CONV_DIM), lambda b, c: (b, c, 0)),
                  pl.BlockSpec((1, Q, LANES), lambda b, c: (b, c, 0)),
                  pl.BlockSpec((1, SUBLANES, M_CONV_DIM), lambda b, c: (b, 0, 0)),
                  pl.BlockSpec((1, M_NGROUPS, M_D_STATE, M_GN), lambda b, c: (b, 0, 0, 0)),
                  pl.BlockSpec((M_CONV, M_CONV_DIM), const2),
                  pl.BlockSpec((1, M_CONV_DIM), const2),
                  pl.BlockSpec((1, LANES), const2),
                  pl.BlockSpec((1, LANES), const2),
                  pl.BlockSpec((1, M_D_INNER), const2),
                  pl.BlockSpec((1, M_D_INNER), const2)],
        out_specs=[pl.BlockSpec((1, Q, M_D_INNER), lambda b, c: (b, c, 0)),
                   pl.BlockSpec((1, M_NGROUPS, M_D_STATE, M_GN), lambda b, c: (b, 0, 0, 0)),
                   pl.BlockSpec((1, SUBLANES, M_CONV_DIM), lambda b, c: (b, 0, 0))],
        out_shape=[jax.ShapeDtypeStruct((B, L, M_D_INNER), BF16),
                   jax.ShapeDtypeStruct((B, M_NGROUPS, M_D_STATE, M_GN), F32),
                   jax.ShapeDtypeStruct((B, SUBLANES, M_CONV_DIM), F32)],
        scratch_shapes=[pltpu.VMEM((Q + SUBLANES, M_CONV_DIM), F32),
                        pltpu.VMEM((SUBLANES, M_CONV_DIM), F32),
                        pltpu.VMEM((M_NGROUPS, M_D_STATE, M_GN), F32)],
        compiler_params=_cparams(("parallel", "arbitrary")),
        name="ssd_mixer",
    )(zx, dtr, conv0, h0t, cw, cb, dtb, alog, dskip, nw)


def _proj_res_kernel(a_ref, w_ref, h_ref, gate_ref, o_ref):
    o_ref[...] = h_ref[...] + gate_ref[0] * _dot(a_ref[...], w_ref[...])


def _proj_residual(a, w, h, gate, *, tm):
    R, K = a.shape
    Dm = w.shape[1]
    nmod, rm, _ = gate.shape
    tiles_per_mod = R // nmod // tm
    return pl.pallas_call(
        _proj_res_kernel,
        grid=(R // tm,),
        in_specs=[pl.BlockSpec((tm, K), lambda i: (i, 0)),
                  pl.BlockSpec((K, Dm), lambda i: (0, 0)),
                  pl.BlockSpec((tm, Dm), lambda i: (i, 0)),
                  pl.BlockSpec((1, rm, Dm), lambda i: (i // tiles_per_mod, 0, 0))],
        out_specs=pl.BlockSpec((tm, Dm), lambda i: (i, 0)),
        out_shape=jax.ShapeDtypeStruct((R, Dm), F32),
        compiler_params=_cparams(("parallel",)),
        name="proj_residual",
    )(a, w, h, gate)


def _attn_out_kernel(o0_ref, o1_ref, o2_ref, l0_ref, l1_ref, l2_ref, e_ref, w_ref, h_ref,
                     gate_ref, out_ref, nat_o, nat_l, *, tm):
    nlc = ATT_OUT // LANES
    for g, (o_ref, l_ref) in enumerate(((o0_ref, l0_ref), (o1_ref, l1_ref), (o2_ref, l2_ref))):
        d = DIL_PAIRS[g][1]
        rows = tm // d
        for r in range(d):
            sl = pl.ds(0, tm) if d == 1 else pl.ds(r, rows, stride=d)
            nat_l.at[g][sl, :] = l_ref[0, r]
            for lc in range(nlc):
                nat_o.at[g * nlc + lc][sl, :] = o_ref[0, r, lc]
    l0, l1, l2 = nat_l[0], nat_l[1], nat_l[2]
    mx = jnp.maximum(jnp.maximum(l0, l1), l2)
    w0, w1, w2 = jnp.exp(l0 - mx), jnp.exp(l1 - mx), jnp.exp(l2 - mx)
    inv = 1.0 / (w0 + w1 + w2)
    alphas = []
    for wg in (w0, w1, w2):
        a = wg * inv
        hi = a.astype(BF16)
        lo = (a - hi.astype(F32)).astype(BF16)
        alphas.append(_dot(hi, e_ref[...]) + _dot(lo, e_ref[...]))
    pieces = []
    for lc in range(nlc):
        ls = slice(lc * LANES, (lc + 1) * LANES)
        t = alphas[0][:, ls] * nat_o[lc]
        for g in range(1, len(DIL_PAIRS)):
            t = t + alphas[g][:, ls] * nat_o[g * nlc + lc]
        pieces.append(t.astype(BF16))
    comb = jnp.concatenate(pieces, axis=1)
    out_ref[...] = h_ref[...] + gate_ref[0] * _dot(comb, w_ref[...])


def _attn_out(o_parts, lse_parts, expand, w, h, gate, *, B, tm):
    R, Dm = h.shape
    seq_tiles = R // B // tm
    nlc = ATT_OUT // LANES
    ng = len(DIL_PAIRS)
    row = lambda i: (i, 0)
    o_specs = [pl.BlockSpec((1, d, nlc, tm // d, LANES),
                            lambda i: (i // seq_tiles, 0, 0, i % seq_tiles, 0)) for _, d in DIL_PAIRS]
    l_specs = [pl.BlockSpec((1, d, tm // d, LANES),
                            lambda i: (i // seq_tiles, 0, i % seq_tiles, 0)) for _, d in DIL_PAIRS]
    return pl.pallas_call(
        functools.partial(_attn_out_kernel, tm=tm),
        grid=(R // tm,),
        in_specs=o_specs + l_specs
        + [pl.BlockSpec((LANES, ATT_OUT), lambda i: (0, 0)),
           pl.BlockSpec((ATT_OUT, Dm), lambda i: (0, 0)),
           pl.BlockSpec((tm, Dm), row),
           pl.BlockSpec((1, 1, Dm), lambda i: (i // seq_tiles, 0, 0))],
        out_specs=pl.BlockSpec((tm, Dm), row),
        out_shape=jax.ShapeDtypeStruct((R, Dm), F32),
        scratch_shapes=[pltpu.VMEM((ng * nlc, tm, LANES), F32), pltpu.VMEM((ng, tm, LANES), F32)],
        compiler_params=_cparams(("parallel",)),
        name="attn_out",
    )(*o_parts, *lse_parts, expand, w, h, gate)


def _ffn_kernel(*refs, tm, seq_tiles, step_mode, has_final):
    h_ref, g_ref, sh_ref, sc_ref, gate_ref, wup_ref, cw_ref, cb_ref, wd_ref = refs[:9]
    k = 9
    if step_mode:
        pp_ref = refs[k]
        k += 1
    if has_final:
        fg_ref = refs[k]
        k += 1
    out_ref, u_ref, act_ref = refs[k:k + 3]
    k += 3
    if not step_mode:
        carry_ref, fix_ref = refs[k:k + 2]
    i = pl.program_id(0)
    KC = FFN_CONV - 1
    tc = FFN_TC

    x = h_ref[...]
    ms = jnp.mean(x * x, axis=-1, keepdims=True)
    xn = (x * lax.rsqrt(ms + EPS) * g_ref[...] * (1.0 + sc_ref[0]) + sh_ref[0]).astype(BF16)

    if step_mode:
        t = lax.broadcasted_iota(jnp.int32, (tm, 1), 0) & (SUBLANES - 1)
    else:
        @pl.when(i % seq_tiles == 0)
        def _():
            carry_ref[...] = jnp.zeros_like(carry_ref)

    def conv(cols, part):
        u = _dot(xn, wup_ref[:, cols])
        w = [cw_ref[kk:kk + 1, cols] for kk in range(FFN_CONV)]
        b = cb_ref[:, cols]
        if step_mode:
            u_ref[:, cols] = u
            pp = pp_ref[:, cols]
            acc = u * w[KC]
            for s in range(1, FFN_CONV):
                term = jnp.where(t >= s, pltpu.roll(u, s, 0), pltpu.roll(pp, tm + s - SUBLANES, 0))
                acc = acc + term * w[KC - s]
            return acc + b
        fix_ref[part, 0:SUBLANES, :] = carry_ref[:, cols]
        fix_ref[part, SUBLANES:2 * SUBLANES, :] = u[0:SUBLANES]
        last = u[tm - SUBLANES:tm]
        carry_ref[:, cols] = last
        u_ref[0, :, cols] = last
        acc = u * w[KC] + b
        fix = fix_ref[part, SUBLANES:2 * SUBLANES, :] * w[KC] + b
        for s in range(1, FFN_CONV):
            acc = acc + pltpu.roll(u, s, 0) * w[KC - s]
            fix = fix + fix_ref[part, pl.ds(SUBLANES - s, SUBLANES), :] * w[KC - s]
        return jnp.concatenate([fix, acc[SUBLANES:]], axis=0)

    for c in range(D_FF // tc):
        cg = conv(slice(c * tc, (c + 1) * tc), 0)
        cv = conv(slice(D_FF + c * tc, D_FF + (c + 1) * tc), 1)
        act_ref[:, c * tc:(c + 1) * tc] = (_silu(cg) * cv).astype(BF16)

    hn = x + gate_ref[0] * _dot(act_ref[...], wd_ref[...])
    if has_final:
        ms = jnp.mean(hn * hn, axis=-1, keepdims=True)
        hn = hn * lax.rsqrt(ms + EPS) * fg_ref[...]
    out_ref[...] = hn


def _conv_ffn(h, g, shift, scale, gate, w_up, cw, cb, w_down, *, tm, seq_rows,
              prev=None, final_g=None):
    R, Dm = h.shape
    nmod, rm, _ = shift.shape
    tiles_per_mod = R // nmod // tm
    step_mode = prev is not None
    seq_tiles = 1 if step_mode else seq_rows // tm
    ntiles = R // tm
    once = pl.Buffered(1)
    const = lambda i: (0, 0)
    mod_spec = pl.BlockSpec((1, rm, Dm), lambda i: (i // tiles_per_mod, 0, 0))
    row_spec = pl.BlockSpec((tm, Dm), lambda i: (i, 0))
    in_specs = [row_spec, pl.BlockSpec((1, Dm), const), mod_spec, mod_spec, mod_spec,
                pl.BlockSpec((Dm, 2 * D_FF), const, pipeline_mode=once),
                pl.BlockSpec((FFN_CONV, 2 * D_FF), const, pipeline_mode=once),
                pl.BlockSpec((1, 2 * D_FF), const, pipeline_mode=once),
                pl.BlockSpec((D_FF, Dm), const, pipeline_mode=once)]
    args = [h, g, shift, scale, gate, w_up, cw, cb, w_down]
    scratch = [pltpu.VMEM((tm, D_FF), BF16)]
    if step_mode:
        in_specs.append(pl.BlockSpec((tm, 2 * D_FF), lambda i: (i, 0)))
        args.append(prev)
        u_spec = pl.BlockSpec((tm, 2 * D_FF), lambda i: (i, 0))
        u_shape = jax.ShapeDtypeStruct((R, 2 * D_FF), F32)
    else:
        u_spec = pl.BlockSpec((1, SUBLANES, 2 * D_FF), lambda i: (i, 0, 0))
        u_shape = jax.ShapeDtypeStruct((ntiles, SUBLANES, 2 * D_FF), F32)
        scratch += [pltpu.VMEM((SUBLANES, 2 * D_FF), F32),
                    pltpu.VMEM((2, 2 * SUBLANES, FFN_TC), F32)]
    if final_g is not None:
        in_specs.append(pl.BlockSpec((1, Dm), const))
        args.append(final_g)
    return pl.pallas_call(
        functools.partial(_ffn_kernel, tm=tm, seq_tiles=seq_tiles,
                          step_mode=step_mode, has_final=final_g is not None),
        grid=(ntiles,),
        in_specs=in_specs,
        out_specs=[row_spec, u_spec],
        out_shape=[jax.ShapeDtypeStruct((R, Dm), F32), u_shape],
        scratch_shapes=scratch,
        compiler_params=_cparams(("arbitrary",), vmem_mb=56),
        name="conv_ffn",
    )(*args)


def _band_attn_kernel(q_ref, kc_ref, kp_ref, vc_ref, vp_ref, o_ref, lse_ref):
    c = pl.program_id(2)
    BL = ATT_BLOCK
    lane = lax.broadcasted_iota(jnp.int32, (1, LANES), 1)
    i = lax.broadcasted_iota(jnp.int32, (BL, 2 * BL), 0)
    j = lax.broadcasted_iota(jnp.int32, (BL, 2 * BL), 1)
    dist = BL + i - j
    valid = (dist >= 0) & (dist <= BL) & ((c > 0) | (j >= BL))
    lane_lo = lane < HEAD_DIM
    keeps = (lane_lo, jnp.logical_not(lane_lo))
    heads = range(HEADS_PER_GROUP)
    zero = jnp.zeros((BL, LANES), BF16)
    scores = []
    for h in heads:
        ps = slice((h // 2) * LANES, (h // 2 + 1) * LANES)
        qh = jnp.where(keeps[h % 2], q_ref[0, 0, :, ps], zero)
        s = jnp.concatenate([_dot_nt(qh, kp_ref[0, 0, :, ps]), _dot_nt(qh, kc_ref[0, 0, :, ps])], axis=1)
        scores.append(jnp.where(valid, s, NEG_INF))
    probs = []
    lse = jnp.zeros((BL, LANES), F32)
    for h in heads:
        m = jnp.max(scores[h], axis=-1, keepdims=True)
        pr = jnp.exp(scores[h] - m)
        l = jnp.sum(pr, axis=-1, keepdims=True)
        lse = jnp.where(lane == h, m + jnp.log(l), lse)
        probs.append((pr.astype(BF16), jnp.where(keeps[h % 2], 1.0 / l, 0.0)))
    lse_ref[0, 0] = lse
    for p in range(HEADS_PER_GROUP // 2):
        ps = slice(p * LANES, (p + 1) * LANES)
        vc, vp = vc_ref[0, 0, :, ps], vp_ref[0, 0, :, ps]
        o_pair = None
        for hh in range(2):
            prb = probs[2 * p + hh][0]
            o = (_dot(prb[:, :BL], jnp.where(keeps[hh], vp, zero))
                 + _dot(prb[:, BL:], jnp.where(keeps[hh], vc, zero)))
            o_pair = o if o_pair is None else o_pair + o
        o_ref[0, 0, p] = o_pair * (probs[2 * p][1] + probs[2 * p + 1][1])


def _band_attention(q, k, v):
    B, d, n, W = q.shape
    nb = n // ATT_BLOCK
    cur = pl.BlockSpec((1, 1, ATT_BLOCK, W), lambda b, r, c: (b, r, c, 0))
    prv = pl.BlockSpec((1, 1, ATT_BLOCK, W), lambda b, r, c: (b, r, jnp.maximum(c - 1, 0), 0))
    return pl.pallas_call(
        _band_attn_kernel,
        grid=(B, d, nb),
        in_specs=[cur, cur, prv, cur, prv],
        out_specs=[pl.BlockSpec((1, 1, W // LANES, ATT_BLOCK, LANES), lambda b, r, c: (b, r, 0, c, 0)),
                   pl.BlockSpec((1, 1, ATT_BLOCK, LANES), lambda b, r, c: (b, r, c, 0))],
        out_shape=[jax.ShapeDtypeStruct((B, d, W // LANES, n, LANES), F32),
                   jax.ShapeDtypeStruct((B, d, n, LANES), F32)],
        compiler_params=_cparams(("parallel", "parallel", "arbitrary")),
        name="band_attention",
    )(q, k, k, v, v)


def _dec_attn_kernel(q_ref, kn_ref, vn_ref, k0_ref, v0_ref, k1_ref, v1_ref, k2_ref, v2_ref,
                     o_ref, *, T, HB):
    s = pl.program_id(1)
    caches = ((k0_ref, v0_ref), (k1_ref, v1_ref), (k2_ref, v2_ref))
    t = lax.broadcasted_iota(jnp.int32, (T, 1), 0)
    u = lax.broadcasted_iota(jnp.int32, (1, LANES), 1)
    masks = []
    for g, (win, dil) in enumerate(DIL_PAIRS):
        p = lax.broadcasted_iota(jnp.int32, (1, win), 1)
        dist = win + t - p
        cmask = (dist <= win) & ((dist & (dil - 1)) == 0)
        nmask = (u <= t) & (((t - u) & (dil - 1)) == 0)
        masks.append((cmask, nmask))
    units = [(hh, g) for hh in range(HB) for g in range(len(DIL_PAIRS))]
    scores = {}
    for hh, g in units:
        head = g * HEADS_PER_GROUP + s * HB + hh
        cmask, nmask = masks[g]
        q = q_ref[0, head].astype(BF16)
        sc = jnp.where(cmask, _dot(q, caches[g][0][0, hh].astype(BF16)), NEG_INF)
        sn = jnp.where(nmask, _dot(q, kn_ref[0, head].astype(BF16)), NEG_INF)
        scores[hh, g] = (sc, sn)
    probs = {}
    for hh, g in units:
        sc, sn = scores[hh, g]
        m = jnp.maximum(jnp.max(sc, axis=-1, keepdims=True), jnp.max(sn, axis=-1, keepdims=True))
        pc = jnp.exp(sc - m)
        pn = jnp.exp(sn - m)
        l = jnp.sum(pc, axis=-1, keepdims=True) + jnp.sum(pn, axis=-1, keepdims=True)
        probs[hh, g] = (m, l, pc.astype(BF16), pn.astype(BF16))
    outs = {}
    for hh, g in units:
        head = g * HEADS_PER_GROUP + s * HB + hh
        m, l, pc, pn = probs[hh, g]
        outs[hh, g] = (_dot_nt(pc, caches[g][1][0, hh].astype(BF16))
                       + _dot_nt(pn, vn_ref[0, head].astype(BF16)))
    for hh in range(HB):
        ms = [probs[hh, g][0] for g in range(len(DIL_PAIRS))]
        mx = jnp.maximum(jnp.maximum(ms[0], ms[1]), ms[2])
        num = None
        den = None
        for g in range(len(DIL_PAIRS)):
            w = jnp.exp(ms[g] - mx)
            num = w * outs[hh, g] if num is None else num + w * outs[hh, g]
            den = w * probs[hh, g][1] if den is None else den + w * probs[hh, g][1]
        o_ref[0, hh] = num / den


def _decode_attention(qh, knt, vnt, cache_kt, cache_vt):
    B, H, T, E = qh.shape
    P = cache_kt.shape[-1]
    HB = 8
    nhb = HEADS_PER_GROUP // HB
    specs = []
    for g, (win, dil) in enumerate(DIL_PAIRS):
        imap = functools.partial(lambda b, s, g, last: (b, g * nhb + s, 0, last),
                                 g=g, last=P // win - 1)
        specs += [pl.BlockSpec((1, HB, E, win), imap)] * 2
    full_q = pl.BlockSpec((1, H, T, E), lambda b, s: (b, 0, 0, 0))
    full_n = pl.BlockSpec((1, H, E, LANES), lambda b, s: (b, 0, 0, 0))
    return pl.pallas_call(
        functools.partial(_dec_attn_kernel, T=T, HB=HB),
        grid=(B, nhb),
        in_specs=[full_q, full_n, full_n] + specs,
        out_specs=pl.BlockSpec((1, HB, T, E), lambda b, s: (b, s, 0, 0)),
        out_shape=jax.ShapeDtypeStruct((B, HEADS_PER_GROUP, T, E), F32),
        compiler_params=_cparams(("parallel", "arbitrary"), vmem_mb=56),
        name="decode_attention",
    )(qh, knt, vnt, cache_kt, cache_vt, cache_kt, cache_vt, cache_kt, cache_vt)


def _rope_tables(pos):
    half = HEAD_DIM // 2
    inv = ROPE_THETA ** (-jnp.arange(half, dtype=F32) * (2.0 / HEAD_DIM))
    ang = pos.astype(F32)[:, None] * inv[None, :]
    cos, sin = jnp.cos(ang), jnp.sin(ang)
    cos_t = jnp.concatenate([cos, cos, cos, cos], axis=1)
    sin_t = jnp.concatenate([-sin, sin, -sin, sin], axis=1)
    return cos_t, sin_t


def _prep_params(p):
    w = {}
    w_in = p['m_w_in']
    w['w_in_zx'] = w_in[:, :, :M_D_INNER + M_CONV_DIM].astype(BF16)
    w['w_in_dt'] = jnp.pad(w_in[:, :, M_D_INNER + M_CONV_DIM:],
                           ((0, 0), (0, 0), (0, LANES - M_HEADS))).astype(BF16)
    w['dt_bias'] = jnp.pad(p['m_dt_bias'], ((0, 0), (0, LANES - M_HEADS)))[:, None, :]
    w['a_log'] = jnp.pad(p['m_A_log'], ((0, 0), (0, LANES - M_HEADS)))[:, None, :]
    w['d_skip'] = jnp.repeat(p['m_D'], M_HEADDIM, axis=1)[:, None, :]
    w['m_norm'] = p['m_norm'][:, None, :]
    w['conv_w'] = p['m_conv_w']
    w['conv_b'] = p['m_conv_b'][:, None, :]
    w['w_out'] = p['m_w_out'].astype(BF16)
    w['w_q'] = (p['w_q'] * (HEAD_DIM ** -0.5)).astype(BF16)
    w['w_kv'] = p['w_kv'].astype(BF16)
    w['w_o'] = p['w_o'].astype(BF16)
    w['ffn_up'] = p['ffn_w_up'].astype(BF16)
    w['ffn_down'] = p['ffn_w_down'].astype(BF16)
    w['ffn_cw'] = p['ffn_conv_w']
    w['ffn_cb'] = p['ffn_conv_b'][:, None, :]
    head = jnp.arange(ATT_OUT) // HEAD_DIM
    w['expand'] = (jnp.arange(LANES)[:, None] == head[None, :]).astype(BF16)
    return w


def _state_to_kernel(s):
    B = s.shape[0]
    return s.reshape(B, M_NGROUPS, 8, M_HEADDIM, M_D_STATE).transpose(0, 1, 4, 2, 3).reshape(
        B, M_NGROUPS, M_D_STATE, M_GN)


def _state_from_kernel(s):
    B = s.shape[0]
    return s.reshape(B, M_NGROUPS, M_D_STATE, 8, M_HEADDIM).transpose(0, 1, 3, 4, 2).reshape(
        B, M_HEADS, M_HEADDIM, M_D_STATE)


def _trunk(x, mods, kvmod, pos, ssm0, conv0, ffn0, kv_past, p, w, *, tm, step):
    B, T, Dm = x.shape
    R = B * T
    h = x.reshape(R, Dm)
    cos_t, sin_t = _rope_tables(pos)
    cos_r = jnp.tile(cos_t, (B, 1))
    sin_r = jnp.tile(sin_t, (B, 1))
    ssm_out, conv_out, ffn_out = [], [], []
    k_new = v_new = new_t = kv_split = None
    for i in range(DEPTH):
        sh1, sc1, g1, sh2, sc2, g2 = mods[i]
        if i < N_A:
            zx, dtr = _norm_mod_matmul(h, p['norm_mix'][i][None], sh1, sc1, w['w_in_zx'][i],
                                       tm=tm, tn=1024, out_dtype=F32, extra_w=w['w_in_dt'][i])
            zx = zx.reshape(B, T, -1)
            dtr = dtr.reshape(B, T, LANES)
            if step:
                padr = ((0, 0), (0, SSD_CHUNK - T), (0, 0))
                zx, dtr = jnp.pad(zx, padr), jnp.pad(dtr, padr)
            conv_prev = jnp.pad(conv0[i], ((0, 0), (SUBLANES - (M_CONV - 1), 0), (0, 0)))
            y, h_t, conv_tail = _ssd_mixer(
                zx, dtr, conv_prev, _state_to_kernel(ssm0[i]), w['conv_w'][i], w['conv_b'][i],
                w['dt_bias'][i], w['a_log'][i], w['d_skip'][i], w['m_norm'][i], Tv=min(T, SSD_CHUNK))
            ssm_out.append(_state_from_kernel(h_t))
            conv_out.append(conv_tail[:, SUBLANES - (M_CONV - 1):])
            y = y[:, :T].reshape(R, M_D_INNER)
            h = _proj_residual(y, w['w_out'][i], h, g1, tm=min(tm, 512))
        else:
            jb = i - N_A
            if step:
                q = _norm_mod_matmul(h, p['norm_mix'][i][None], sh1, sc1, w['w_q'][jb], tm=tm,
                                     tn=1024, out_dtype=F32, rope=(cos_r, sin_r),
                                     n_rope=ATT_WIDTH // 1024)
                qh = q.reshape(B, T, ATT_HEADS, HEAD_DIM).transpose(0, 2, 1, 3)
                o = _decode_attention(qh, *new_t, *kv_past)
                o = o.transpose(0, 2, 1, 3).reshape(R, ATT_OUT).astype(BF16)
                h = _proj_residual(o, w['w_o'][jb], h, g1, tm=tm)
            else:
                q_split = _norm_mod_matmul_split(h, p['norm_mix'][i][None], sh1, sc1, w['w_q'][jb],
                                                 cos_t, sin_t, B=B, tm=SPLIT_TM, n_rope=len(DIL_PAIRS))
                o_parts, lse_parts = [], []
                for g in range(len(DIL_PAIRS)):
                    og, lg = _band_attention(q_split[g], *kv_split[g])
                    o_parts.append(og)
                    lse_parts.append(lg)
                h = _attn_out(o_parts, lse_parts, w['expand'], w['w_o'][jb], h, g1, B=B, tm=SPLIT_TM)
        last = i == DEPTH - 1
        fin = p['final_norm'][None] if last else None
        if step:
            prev = jnp.pad(ffn0[i], ((0, 0), (SUBLANES - (FFN_CONV - 1), 0), (0, 0))).reshape(R, 2 * D_FF)
            h, u = _conv_ffn(h, p['norm_ffn'][i][None], sh2, sc2, g2, w['ffn_up'][i], w['ffn_cw'][i],
                             w['ffn_cb'][i], w['ffn_down'][i], tm=tm, seq_rows=T, prev=prev, final_g=fin)
            ffn_out.append(u.reshape(B, T, 2 * D_FF)[:, T - (FFN_CONV - 1):])
        else:
            tmf = min(tm, 512)
            h, u = _conv_ffn(h, p['norm_ffn'][i][None], sh2, sc2, g2, w['ffn_up'][i], w['ffn_cw'][i],
                             w['ffn_cb'][i], w['ffn_down'][i], tm=tmf, seq_rows=T, final_g=fin)
            seq_tiles = T // tmf
            ffn_out.append(u[seq_tiles - 1::seq_tiles, SUBLANES - (FFN_CONV - 1):])
        if i == N_A - 1:
            ksh, ksc = kvmod
            keep = min(WINDOW_MAX, T)
            if step:
                h_tail, ksh_t, ksc_t, cos_k, sin_k, tm_k = h, ksh, ksc, cos_r, sin_r, tm
            else:
                ng = len(DIL_PAIRS)
                kvs = _norm_mod_matmul_split(h, p['kv_norm'][None], ksh, ksc, w['w_kv'], cos_t, sin_t,
                                             B=B, tm=SPLIT_TM, n_rope=ng)
                kv_split = [(kvs[g], kvs[ng + g]) for g in range(ng)]
                h_tail = h.reshape(B, T, Dm)[:, T - keep:].reshape(B * keep, Dm)
                ksh_t, ksc_t, tm_k = ksh, ksc, min(tm, keep)
                cos_k = jnp.tile(cos_t[T - keep:], (B, 1))
                sin_k = jnp.tile(sin_t[T - keep:], (B, 1))
            kv = _norm_mod_matmul(h_tail, p['kv_norm'][None], ksh_t, ksc_t, w['w_kv'], tm=tm_k,
                                  tn=1024, out_dtype=F32, rope=(cos_k, sin_k),
                                  n_rope=ATT_WIDTH // 1024)
            kv = kv.reshape(B, keep, 2, ATT_HEADS, HEAD_DIM)
            k_new, v_new = kv[:, :, 0], kv[:, :, 1]
            if step:
                new_t = [jnp.pad(a.transpose(0, 2, 3, 1), ((0, 0), (0, 0), (0, 0), (0, LANES - T)))
                         for a in (k_new, v_new)]
    return (h.reshape(B, T, Dm), jnp.stack(ssm_out), jnp.stack(conv_out), jnp.stack(ffn_out),
            k_new, v_new)


def kernel(x_prompt, x_sample, state_ssm, state_conv, state_ffn_conv, cache_k, cache_v, c_prompt, c_sample, ada_w, ada_b, norm_mix, norm_ffn, m_w_in, m_conv_w, m_conv_b, m_dt_bias, m_A_log, m_D, m_norm, m_w_out, kv_norm, kv_ada_w, kv_ada_b, w_kv, w_q, w_o, ffn_w_up, ffn_conv_w, ffn_conv_b, ffn_w_down, final_norm):
    p = dict(norm_mix=norm_mix, norm_ffn=norm_ffn, m_w_in=m_w_in, m_conv_w=m_conv_w,
             m_conv_b=m_conv_b, m_dt_bias=m_dt_bias, m_A_log=m_A_log, m_D=m_D, m_norm=m_norm,
             m_w_out=m_w_out, kv_norm=kv_norm, w_kv=w_kv, w_q=w_q, w_o=w_o, ffn_w_up=ffn_w_up,
             ffn_conv_w=ffn_conv_w, ffn_conv_b=ffn_conv_b, ffn_w_down=ffn_w_down,
             final_norm=final_norm)
    w = _prep_params(p)
    Bp, S, Dm = x_prompt.shape
    Bs, T, _ = x_sample.shape

    nrow = Bp + Bs
    npad = -(-nrow // SUBLANES) * SUBLANES
    c_all = jnp.pad(jnp.concatenate([c_prompt, c_sample], axis=0), ((0, npad - nrow), (0, 0)))
    mod = _ada_linear(c_all, ada_w, ada_b[:, None, :])
    kvm = _ada_linear(c_all, kv_ada_w[None], kv_ada_b[None, None, :])[0]

    def mod_prompt(m):
        return [a[:, None, :] for a in jnp.split(m, m.shape[-1] // Dm, axis=-1)]

    def mod_sample(m):
        return [jnp.repeat(a, T, axis=0)[None] for a in jnp.split(m, m.shape[-1] // Dm, axis=-1)]

    mods_p = [mod_prompt(mod[i, :Bp]) for i in range(DEPTH)]
    mods_s = [mod_sample(mod[i, Bp:nrow]) for i in range(DEPTH)]
    kvm_p = mod_prompt(kvm[:Bp])
    kvm_s = mod_sample(kvm[Bp:nrow])

    ssm0 = jnp.zeros((N_A, Bp, M_HEADS, M_HEADDIM, M_D_STATE), state_ssm.dtype)
    conv0 = jnp.zeros((N_A, Bp, M_CONV - 1, M_CONV_DIM), x_prompt.dtype)
    y_p, ssm_p, conv_p, ffn_p, k_p, v_p = _trunk(
        x_prompt, mods_p, kvm_p, jnp.arange(S, dtype=jnp.int32), ssm0, conv0, None, None, p, w,
        tm=1024, step=False)
    cache_t = (cache_k.transpose(0, 2, 3, 1), cache_v.transpose(0, 2, 3, 1))
    y_s, ssm_s, conv_s, ffn_s, k_s, v_s = _trunk(
        x_sample, mods_s, kvm_s, PAST_LEN + jnp.arange(T, dtype=jnp.int32), state_ssm,
        state_conv, state_ffn_conv, cache_t, p, w, tm=Bs * T, step=True)
    return (y_p, y_s, ssm_p, ssm_s, conv_p, conv_s, ffn_p, ffn_s, k_p, k_s, v_p, v_s)
```

```python
import functools

import jax
import jax.numpy as jnp
from jax import lax
from jax.experimental import pallas as pl
from jax.experimental.pallas import tpu as pltpu

F32 = jnp.float32
BF16 = jnp.bfloat16

D_MODEL = 1024
DEPTH = 4
N_A = 2
M_D_INNER = 2048
M_HEADDIM = 64
M_HEADS = 32
M_NGROUPS = 4
M_D_STATE = 128
M_CONV = 4
M_GN = 512
M_CONV_DIM = 3072
HEAD_DIM = 64
HEADS_PER_GROUP = 16
DIL_PAIRS = ((128, 1), (512, 4), (2048, 16))
ATT_HEADS = 48
ATT_WIDTH = 3072
ATT_OUT = 1024
WINDOW_MAX = 2048
ATT_BLOCK = 128
ROPE_THETA = 10000.0
D_FF = 2816
FFN_CONV = 3
EPS = 1e-6
SSD_CHUNK = 128
PAST_LEN = 8192

LANES = 128
SUBLANES = 8
FFN_TC = 256
INPROJ_TC = 512
SPLIT_TM = 512
NEG_INF = float("-inf")


def _cparams(sem, vmem_mb=48):
    return pltpu.CompilerParams(dimension_semantics=sem,
                                vmem_limit_bytes=vmem_mb * 1024 * 1024)


def _silu(x):
    return x * jax.nn.sigmoid(x)


def _dot(a, b):
    return jnp.dot(a, b, preferred_element_type=F32)


def _dot_nt(a, b):
    return lax.dot_general(a, b, (((1,), (1,)), ((), ())), preferred_element_type=F32)


def _ada_kernel(c_ref, w_ref, b_ref, o_ref):
    cs = _silu(c_ref[...]).astype(BF16)
    o_ref[0] = _dot(cs, w_ref[0].astype(BF16)) + b_ref[0]


def _ada_linear(c, w, b, tn=1024):
    L, K, N = w.shape
    M = c.shape[0]
    return pl.pallas_call(
        _ada_kernel,
        grid=(L, N // tn),
        in_specs=[pl.BlockSpec((M, K), lambda l, j: (0, 0)),
                  pl.BlockSpec((1, K, tn), lambda l, j: (l, 0, j)),
                  pl.BlockSpec((1, 1, tn), lambda l, j: (l, 0, j))],
        out_specs=pl.BlockSpec((1, M, tn), lambda l, j: (l, 0, j)),
        out_shape=jax.ShapeDtypeStruct((L, M, N), F32),
        compiler_params=_cparams(("parallel", "parallel")),
        name="ada_linear",
    )(c, w, b)


def _rope_tile(acc, cos, sin):
    pieces = []
    first_half = (lax.broadcasted_iota(jnp.int32, (1, LANES), 1) & (HEAD_DIM - 1)) < (HEAD_DIM // 2)
    for c in range(acc.shape[1] // LANES):
        xc = acc[:, c * LANES:(c + 1) * LANES]
        partner = jnp.where(first_half,
                            pltpu.roll(xc, LANES - HEAD_DIM // 2, 1),
                            pltpu.roll(xc, HEAD_DIM // 2, 1))
        pieces.append(xc * cos + partner * sin)
    return jnp.concatenate(pieces, axis=1)


def _nmm_kernel(h_ref, g_ref, sh_ref, sc_ref, w_ref, cos_ref, sin_ref, o_ref, xn_ref, *, n_rope):
    j = pl.program_id(1)

    @pl.when(j == 0)
    def _():
        x = h_ref[...]
        ms = jnp.mean(x * x, axis=-1, keepdims=True)
        y = x * lax.rsqrt(ms + EPS) * g_ref[...]
        xn_ref[...] = (y * (1.0 + sc_ref[0]) + sh_ref[0]).astype(BF16)

    acc = _dot(xn_ref[...], w_ref[...])

    @pl.when(j < n_rope)
    def _():
        o_ref[...] = _rope_tile(acc, cos_ref[...], sin_ref[...]).astype(o_ref.dtype)

    @pl.when(j >= n_rope)
    def _():
        o_ref[...] = acc.astype(o_ref.dtype)


def _norm_mod_matmul(h, g, shift, scale, w, rope, *, tm, tn, out_dtype, n_rope):
    R, Dm = h.shape
    N = w.shape[1]
    nmod, rm, _ = shift.shape
    tiles_per_mod = R // nmod // tm
    mod_spec = pl.BlockSpec((1, rm, Dm), lambda i, j: (i // tiles_per_mod, 0, 0))
    tab_spec = pl.BlockSpec((tm, LANES), lambda i, j: (i, 0))
    return pl.pallas_call(
        functools.partial(_nmm_kernel, n_rope=n_rope),
        grid=(R // tm, N // tn),
        in_specs=[pl.BlockSpec((tm, Dm), lambda i, j: (i, 0)),
                  pl.BlockSpec((1, Dm), lambda i, j: (0, 0)),
                  mod_spec, mod_spec,
                  pl.BlockSpec((Dm, tn), lambda i, j: (0, j)),
                  tab_spec, tab_spec],
        out_specs=pl.BlockSpec((tm, tn), lambda i, j: (i, j)),
        out_shape=jax.ShapeDtypeStruct((R, N), out_dtype),
        scratch_shapes=[pltpu.VMEM((tm, Dm), BF16)],
        compiler_params=_cparams(("parallel", "arbitrary")),
        name="norm_mod_matmul",
    )(h, g, shift, scale, w, *rope)


def _nmm_split_kernel(*refs, n_rope, n_out, tm):
    h_ref, g_ref, sh_ref, sc_ref, w_ref, cos_ref, sin_ref = refs[:7]
    outs = refs[7:7 + n_out]
    xf_ref, xn_ref = refs[7 + n_out:]
    ng = len(DIL_PAIRS)
    W = ATT_OUT

    x = h_ref[...]
    ms = jnp.mean(x * x, axis=-1, keepdims=True)
    y = x * lax.rsqrt(ms + EPS) * g_ref[...]
    xn = y * (1.0 + sc_ref[0]) + sh_ref[0]
    nlc = xn.shape[1] // LANES
    for lc in range(nlc):
        xf_ref[lc] = xn[:, lc * LANES:(lc + 1) * LANES]
    for g, (win, d) in enumerate(DIL_PAIRS):
        rows = tm // d
        if d == 1:
            xn_ref[g] = xn.astype(BF16)
        else:
            for r in range(d):
                for lc in range(nlc):
                    xn_ref[g, r * rows:(r + 1) * rows, lc * LANES:(lc + 1) * LANES] = (
                        xf_ref.at[lc][pl.ds(r, rows, stride=d), :].astype(BF16))

    for k in range(n_out):
        g = k % ng
        acc = _dot(xn_ref[g], w_ref[:, k * W:(k + 1) * W])
        val = _rope_tile(acc, cos_ref[g], sin_ref[g]) if k < n_rope else acc
        d = DIL_PAIRS[g][1]
        rows = tm // d
        for r in range(d):
            outs[k][0, r] = val[r * rows:(r + 1) * rows].astype(outs[k].dtype)


def _split_rows(x, d, tm):
    T, C = x.shape
    return x.reshape(T // tm, tm // d, d, C).transpose(0, 2, 1, 3).reshape(T, C)


def _norm_mod_matmul_split(h, g, shift, scale, w, cos_t, sin_t, *, B, tm, n_rope):
    R, Dm = h.shape
    S = R // B
    W = ATT_OUT
    n_out = w.shape[1] // W
    seq_tiles = S // tm
    ng = len(DIL_PAIRS)
    cos_g = jnp.stack([_split_rows(cos_t, d, tm) for _, d in DIL_PAIRS])
    sin_g = jnp.stack([_split_rows(sin_t, d, tm) for _, d in DIL_PAIRS])
    mod_spec = pl.BlockSpec((1, 1, Dm), lambda i: (i // seq_tiles, 0, 0))
    tab_spec = pl.BlockSpec((ng, tm, LANES), lambda i: (0, i % seq_tiles, 0))
    out_specs, out_shape = [], []
    for k in range(n_out):
        d = DIL_PAIRS[k % ng][1]
        out_specs.append(pl.BlockSpec((1, d, tm // d, W),
                                      lambda i: (i // seq_tiles, 0, i % seq_tiles, 0)))
        out_shape.append(jax.ShapeDtypeStruct((B, d, S // d, W), BF16))
    return pl.pallas_call(
        functools.partial(_nmm_split_kernel, n_rope=n_rope, n_out=n_out, tm=tm),
        grid=(R // tm,),
        in_specs=[pl.BlockSpec((tm, Dm), lambda i: (i, 0)),
                  pl.BlockSpec((1, Dm), lambda i: (0, 0)),
                  mod_spec, mod_spec,
                  pl.BlockSpec((Dm, n_out * W), lambda i: (0, 0), pipeline_mode=pl.Buffered(1)),
                  tab_spec, tab_spec],
        out_specs=out_specs,
        out_shape=out_shape,
        scratch_shapes=[pltpu.VMEM((Dm // LANES, tm, LANES), F32), pltpu.VMEM((ng, tm, Dm), BF16)],
        compiler_params=_cparams(("parallel",), vmem_mb=56),
        name="norm_mod_matmul_split",
    )(h, g, shift, scale, w, cos_g, sin_g)


def _inproj_kernel(*refs, tm, seq_tiles, step_mode):
    h_ref, g_ref, sh_ref, sc_ref, w_ref, wdt_ref, cw_ref, cb_ref = refs[:8]
    k = 8
    if step_mode:
        pp_ref = refs[k]
        k += 1
    zx_ref, dt_ref, u_ref = refs[k:k + 3]
    k += 3
    if not step_mode:
        carry_ref, fix_ref = refs[k:k + 2]
    i = pl.program_id(0)
    KC = M_CONV - 1
    tc = INPROJ_TC

    x = h_ref[...]
    ms = jnp.mean(x * x, axis=-1, keepdims=True)
    xn = (x * lax.rsqrt(ms + EPS) * g_ref[...] * (1.0 + sc_ref[0]) + sh_ref[0]).astype(BF16)
    dt_ref[...] = _dot(xn, wdt_ref[...])
    for c in range(M_D_INNER // tc):
        cols = slice(c * tc, (c + 1) * tc)
        zx_ref[:, cols] = _dot(xn, w_ref[:, cols])

    if step_mode:
        t = lax.broadcasted_iota(jnp.int32, (tm, 1), 0) & (SUBLANES - 1)
    else:
        @pl.when(i % seq_tiles == 0)
        def _():
            carry_ref[...] = jnp.zeros_like(carry_ref)

    for c in range(M_CONV_DIM // tc):
        cols = slice(c * tc, (c + 1) * tc)
        u = _dot(xn, w_ref[:, M_D_INNER + c * tc:M_D_INNER + (c + 1) * tc])
        w = [cw_ref[kk:kk + 1, cols] for kk in range(M_CONV)]
        b = cb_ref[:, cols]
        if step_mode:
            u_ref[:, cols] = u
            pp = pp_ref[:, cols]
            acc = u * w[KC] + b
            for s in range(1, M_CONV):
                term = jnp.where(t >= s, pltpu.roll(u, s, 0), pltpu.roll(pp, tm + s - SUBLANES, 0))
                acc = acc + term * w[KC - s]
        else:
            fix_ref[0:SUBLANES, :] = carry_ref[:, cols]
            fix_ref[SUBLANES:2 * SUBLANES, :] = u[0:SUBLANES]
            last = u[tm - SUBLANES:tm]
            carry_ref[:, cols] = last
            u_ref[0, :, cols] = last
            acc = u * w[KC] + b
            fix = fix_ref[SUBLANES:2 * SUBLANES, :] * w[KC] + b
            for s in range(1, M_CONV):
                acc = acc + pltpu.roll(u, s, 0) * w[KC - s]
                fix = fix + fix_ref[pl.ds(SUBLANES - s, SUBLANES), :] * w[KC - s]
            acc = jnp.concatenate([fix, acc[SUBLANES:]], axis=0)
        zx_ref[:, M_D_INNER + c * tc:M_D_INNER + (c + 1) * tc] = _silu(acc)


def _in_proj(h, g, shift, scale, w_zx, w_dt, cw, cb, *, tm, seq_rows, prev=None):
    R, Dm = h.shape
    N = w_zx.shape[1]
    nmod, rm, _ = shift.shape
    tiles_per_mod = R // nmod // tm
    step_mode = prev is not None
    seq_tiles = 1 if step_mode else seq_rows // tm
    ntiles = R // tm
    once = pl.Buffered(1)
    const = lambda i: (0, 0)
    mod_spec = pl.BlockSpec((1, rm, Dm), lambda i: (i // tiles_per_mod, 0, 0))
    in_specs = [pl.BlockSpec((tm, Dm), lambda i: (i, 0)), pl.BlockSpec((1, Dm), const),
                mod_spec, mod_spec,
                pl.BlockSpec((Dm, N), const, pipeline_mode=once),
                pl.BlockSpec((Dm, LANES), const, pipeline_mode=once),
                pl.BlockSpec((M_CONV, M_CONV_DIM), const, pipeline_mode=once),
                pl.BlockSpec((1, M_CONV_DIM), const, pipeline_mode=once)]
    args = [h, g, shift, scale, w_zx, w_dt, cw, cb]
    scratch = []
    if step_mode:
        in_specs.append(pl.BlockSpec((tm, M_CONV_DIM), lambda i: (i, 0)))
        args.append(prev)
        u_spec = pl.BlockSpec((tm, M_CONV_DIM), lambda i: (i, 0))
        u_shape = jax.ShapeDtypeStruct((R, M_CONV_DIM), F32)
    else:
        u_spec = pl.BlockSpec((1, SUBLANES, M_CONV_DIM), lambda i: (i, 0, 0))
        u_shape = jax.ShapeDtypeStruct((ntiles, SUBLANES, M_CONV_DIM), F32)
        scratch = [pltpu.VMEM((SUBLANES, M_CONV_DIM), F32),
                   pltpu.VMEM((2 * SUBLANES, INPROJ_TC), F32)]
    return pl.pallas_call(
        functools.partial(_inproj_kernel, tm=tm, seq_tiles=seq_tiles, step_mode=step_mode),
        grid=(ntiles,),
        in_specs=in_specs,
        out_specs=[pl.BlockSpec((tm, N), lambda i: (i, 0)),
                   pl.BlockSpec((tm, LANES), lambda i: (i, 0)), u_spec],
        out_shape=[jax.ShapeDtypeStruct((R, N), F32), jax.ShapeDtypeStruct((R, LANES), F32), u_shape],
        scratch_shapes=scratch,
        compiler_params=_cparams(("arbitrary",), vmem_mb=56),
        name="in_proj",
    )(*args)


def _ssd_kernel(*refs, Q, Tv):
    zx_ref, dt_ref, h0_ref, dtb_ref, alog_ref, dskip_ref, nw_ref, y_ref, hout_ref = refs[:9]
    hT_ref = refs[9]
    c = pl.program_id(1)
    npair = M_HEADS // M_NGROUPS // 2

    @pl.when(c == 0)
    def _():
        for g in range(M_NGROUPS):
            for p in range(npair):
                h0 = g * 2 * npair + 2 * p
                pair = h0_ref[0, h0:h0 + 2].reshape(2 * M_HEADDIM, M_D_STATE)
                hT_ref[g, :, p * LANES:(p + 1) * LANES] = pair.T

    if Tv < Q:
        pad_ref, dtp_ref = refs[10:12]
        pad_ref[...] = jnp.zeros_like(pad_ref)
        pad_ref[0:Tv, :] = zx_ref[0]
        dtp_ref[...] = jnp.zeros_like(dtp_ref)
        dtp_ref[0:Tv, :] = dt_ref[0]
        zfull = pad_ref[:, 0:M_D_INNER]
        xact = pad_ref[:, M_D_INNER:]
        dt = dtp_ref[...] + dtb_ref[...]
    else:
        zfull = None
        xact = zx_ref[0, :, M_D_INNER:]
        dt = dt_ref[0] + dtb_ref[...]
    dt = jnp.maximum(dt, 0.0) + jnp.log(1.0 + jnp.exp(-jnp.abs(dt)))
    if Tv < Q:
        row = lax.broadcasted_iota(jnp.int32, (Q, 1), 0)
        dt = jnp.where(row < Tv, dt, 0.0)
    a = dt * (-jnp.exp(alog_ref[...]))
    ri = lax.broadcasted_iota(jnp.int32, (Q, Q), 0)
    ci = lax.broadcasted_iota(jnp.int32, (Q, Q), 1)
    causal = ri >= ci
    tril = jnp.where(causal, 1.0, 0.0).astype(F32)
    acum = jnp.dot(tril, a, preferred_element_type=F32, precision=lax.Precision.HIGHEST)
    acum_t = acum.T
    lane_lo = lax.broadcasted_iota(jnp.int32, (1, LANES), 1) < M_HEADDIM

    for g in range(M_NGROUPS):
        gs = slice(g * M_GN, (g + 1) * M_GN)
        bsl = slice(M_D_INNER + g * M_D_STATE, M_D_INNER + (g + 1) * M_D_STATE)
        csl = slice(M_D_INNER + M_GN + g * M_D_STATE, M_D_INNER + M_GN + (g + 1) * M_D_STATE)
        b_f = xact[:, bsl]
        b_g = b_f.astype(BF16)
        c_g = xact[:, csl].astype(BF16)
        b_gt = b_f.T.astype(BF16)
        cb = _dot_nt(c_g, b_g)
        ydiag, e_parts, dec_parts = [], [], []
        for p in range(4):
            h0 = g * 8 + 2 * p
            xp = xact[:, h0 * M_HEADDIM:(h0 + 2) * M_HEADDIM]
            col0 = acum[:, h0:h0 + 1]
            col1 = acum[:, h0 + 1:h0 + 2]
            dtp = jnp.where(lane_lo, dt[:, h0:h0 + 1], dt[:, h0 + 1:h0 + 2])
            ap = jnp.where(lane_lo, col0, col1)
            xdt = xp * dtp
            e_parts.append(jnp.exp(ap))
            dec = jnp.exp(ap[Q - 1:Q, :] - ap)
            dec_parts.append((xdt * dec).astype(BF16))
            yp = None
            for hh, col, keep in ((h0, col0, lane_lo), (h0 + 1, col1, jnp.logical_not(lane_lo))):
                seg = col - acum_t[hh:hh + 1, :]
                lm = jnp.exp(jnp.where(causal, seg, NEG_INF))
                m = (cb * lm).astype(BF16)
                t = _dot(m, jnp.where(keep, xdt, 0.0).astype(BF16))
                yp = t if yp is None else yp + t
            ydiag.append(yp)
        e_g = jnp.concatenate(e_parts, axis=1)
        xdec_g = jnp.concatenate(dec_parts, axis=1)
        h_prev = hT_ref[g]
        y_off = _dot(c_g, h_prev.astype(BF16)) * e_g
        hT_ref[g] = h_prev * e_g[Q - 1:Q, :] + _dot(b_gt, xdec_g)
        y = jnp.concatenate(ydiag, axis=1) + y_off + xact[:, gs] * dskip_ref[:, gs]
        yg = y * _silu(zx_ref[0, :, gs] if zfull is None else zfull[:, gs])
        ms = jnp.mean(yg * yg, axis=-1, keepdims=True)
        y_ref[0, :, gs] = (yg * lax.rsqrt(ms + EPS) * nw_ref[:, gs])[0:Tv].astype(y_ref.dtype)

    @pl.when(c == pl.num_programs(1) - 1)
    def _():
        for g in range(M_NGROUPS):
            for p in range(npair):
                h0 = g * 2 * npair + 2 * p
                pair = hT_ref[g, :, p * LANES:(p + 1) * LANES].T
                hout_ref[0, h0:h0 + 2] = pair.reshape(2, M_HEADDIM, M_D_STATE)


def _ssd_mixer(zx, dtr, h0, dtb, alog, dskip, nw):
    B, L, W = zx.shape
    Q = SSD_CHUNK
    Tv = min(L, Q)
    nc = L // Tv
    const2 = lambda b, c: (0, 0)
    state_spec = pl.BlockSpec((1, M_HEADS, M_HEADDIM, M_D_STATE), lambda b, c: (b, 0, 0, 0))
    scratch = [pltpu.VMEM((M_NGROUPS, M_D_STATE, M_GN), F32)]
    if Tv < Q:
        scratch += [pltpu.VMEM((Q, W), F32), pltpu.VMEM((Q, LANES), F32)]
    return pl.pallas_call(
        functools.partial(_ssd_kernel, Q=Q, Tv=Tv),
        grid=(B, nc),
        in_specs=[pl.BlockSpec((1, Tv, W), lambda b, c: (b, c, 0)),
                  pl.BlockSpec((1, Tv, LANES), lambda b, c: (b, c, 0)),
                  state_spec,
                  pl.BlockSpec((1, LANES), const2),
                  pl.BlockSpec((1, LANES), const2),
                  pl.BlockSpec((1, M_D_INNER), const2),
                  pl.BlockSpec((1, M_D_INNER), const2)],
        out_specs=[pl.BlockSpec((1, Tv, M_D_INNER), lambda b, c: (b, c, 0)), state_spec],
        out_shape=[jax.ShapeDtypeStruct((B, L, M_D_INNER), BF16),
                   jax.ShapeDtypeStruct((B, M_HEADS, M_HEADDIM, M_D_STATE), F32)],
        scratch_shapes=scratch,
        compiler_params=_cparams(("parallel", "arbitrary")),
        name="ssd_mixer",
    )(zx, dtr, h0, dtb, alog, dskip, nw)


def _proj_res_kernel(a_ref, w_ref, h_ref, gate_ref, o_ref):
    o_ref[...] = h_ref[...] + gate_ref[0] * _dot(a_ref[...], w_ref[...])


def _proj_residual(a, w, h, gate, *, tm):
    R, K = a.shape
    Dm = w.shape[1]
    nmod, rm, _ = gate.shape
    tiles_per_mod = R // nmod // tm
    return pl.pallas_call(
        _proj_res_kernel,
        grid=(R // tm,),
        in_specs=[pl.BlockSpec((tm, K), lambda i: (i, 0)),
                  pl.BlockSpec((K, Dm), lambda i: (0, 0)),
                  pl.BlockSpec((tm, Dm), lambda i: (i, 0)),
                  pl.BlockSpec((1, rm, Dm), lambda i: (i // tiles_per_mod, 0, 0))],
        out_specs=pl.BlockSpec((tm, Dm), lambda i: (i, 0)),
        out_shape=jax.ShapeDtypeStruct((R, Dm), F32),
        compiler_params=_cparams(("parallel",)),
        name="proj_residual",
    )(a, w, h, gate)


def _attn_out_kernel(o0_ref, o1_ref, o2_ref, l0_ref, l1_ref, l2_ref, e_ref, w_ref, h_ref,
                     gate_ref, out_ref, nat_o, nat_l, *, tm):
    nlc = ATT_OUT // LANES
    for g, (o_ref, l_ref) in enumerate(((o0_ref, l0_ref), (o1_ref, l1_ref), (o2_ref, l2_ref))):
        d = DIL_PAIRS[g][1]
        rows = tm // d
        for r in range(d):
            sl = pl.ds(0, tm) if d == 1 else pl.ds(r, rows, stride=d)
            nat_l.at[g][sl, :] = l_ref[0, r]
            for lc in range(nlc):
                nat_o.at[g * nlc + lc][sl, :] = o_ref[0, r, lc]
    l0, l1, l2 = nat_l[0], nat_l[1], nat_l[2]
    mx = jnp.maximum(jnp.maximum(l0, l1), l2)
    w0, w1, w2 = jnp.exp(l0 - mx), jnp.exp(l1 - mx), jnp.exp(l2 - mx)
    inv = 1.0 / (w0 + w1 + w2)
    alphas = []
    for wg in (w0, w1, w2):
        a = wg * inv
        hi = a.astype(BF16)
        lo = (a - hi.astype(F32)).astype(BF16)
        alphas.append(_dot(hi, e_ref[...]) + _dot(lo, e_ref[...]))
    pieces = []
    for lc in range(nlc):
        ls = slice(lc * LANES, (lc + 1) * LANES)
        t = alphas[0][:, ls] * nat_o[lc]
        for g in range(1, len(DIL_PAIRS)):
            t = t + alphas[g][:, ls] * nat_o[g * nlc + lc]
        pieces.append(t.astype(BF16))
    comb = jnp.concatenate(pieces, axis=1)
    out_ref[...] = h_ref[...] + gate_ref[0] * _dot(comb, w_ref[...])


def _attn_out(o_parts, lse_parts, expand, w, h, gate, *, B, tm):
    R, Dm = h.shape
    seq_tiles = R // B // tm
    nlc = ATT_OUT // LANES
    ng = len(DIL_PAIRS)
    row = lambda i: (i, 0)
    o_specs = [pl.BlockSpec((1, d, nlc, tm // d, LANES),
                            lambda i: (i // seq_tiles, 0, 0, i % seq_tiles, 0)) for _, d in DIL_PAIRS]
    l_specs = [pl.BlockSpec((1, d, tm // d, LANES),
                            lambda i: (i // seq_tiles, 0, i % seq_tiles, 0)) for _, d in DIL_PAIRS]
    return pl.pallas_call(
        functools.partial(_attn_out_kernel, tm=tm),
        grid=(R // tm,),
        in_specs=o_specs + l_specs
        + [pl.BlockSpec((LANES, ATT_OUT), lambda i: (0, 0)),
           pl.BlockSpec((ATT_OUT, Dm), lambda i: (0, 0)),
           pl.BlockSpec((tm, Dm), row),
           pl.BlockSpec((1, 1, Dm), lambda i: (i // seq_tiles, 0, 0))],
        out_specs=pl.BlockSpec((tm, Dm), row),
        out_shape=jax.ShapeDtypeStruct((R, Dm), F32),
        scratch_shapes=[pltpu.VMEM((ng * nlc, tm, LANES), F32), pltpu.VMEM((ng, tm, LANES), F32)],
        compiler_params=_cparams(("parallel",)),
        name="attn_out",
    )(*o_parts, *lse_parts, expand, w, h, gate)


def _ffn_kernel(*refs, tm, seq_tiles, step_mode, has_final):
    h_ref, g_ref, sh_ref, sc_ref, gate_ref, wup_ref, cw_ref, cb_ref, wd_ref = refs[:9]
    k = 9
    if step_mode:
        pp_ref = refs[k]
        k += 1
    if has_final:
        fg_ref = refs[k]
        k += 1
    out_ref, u_ref, act_ref = refs[k:k + 3]
    k += 3
    if not step_mode:
        carry_ref, fix_ref = refs[k:k + 2]
    i = pl.program_id(0)
    KC = FFN_CONV - 1
    tc = FFN_TC

    x = h_ref[...]
    ms = jnp.mean(x * x, axis=-1, keepdims=True)
    xn = (x * lax.rsqrt(ms + EPS) * g_ref[...] * (1.0 + sc_ref[0]) + sh_ref[0]).astype(BF16)

    if step_mode:
        t = lax.broadcasted_iota(jnp.int32, (tm, 1), 0) & (SUBLANES - 1)
    else:
        @pl.when(i % seq_tiles == 0)
        def _():
            carry_ref[...] = jnp.zeros_like(carry_ref)

    def conv(cols, part):
        u = _dot(xn, wup_ref[:, cols])
        w = [cw_ref[kk:kk + 1, cols] for kk in range(FFN_CONV)]
        b = cb_ref[:, cols]
        if step_mode:
            u_ref[:, cols] = u
            pp = pp_ref[:, cols]
            acc = u * w[KC]
            for s in range(1, FFN_CONV):
                term = jnp.where(t >= s, pltpu.roll(u, s, 0), pltpu.roll(pp, tm + s - SUBLANES, 0))
                acc = acc + term * w[KC - s]
            return acc + b
        fix_ref[part, 0:SUBLANES, :] = carry_ref[:, cols]
        fix_ref[part, SUBLANES:2 * SUBLANES, :] = u[0:SUBLANES]
        last = u[tm - SUBLANES:tm]
        carry_ref[:, cols] = last
        u_ref[0, :, cols] = last
        acc = u * w[KC] + b
        fix = fix_ref[part, SUBLANES:2 * SUBLANES, :] * w[KC] + b
        for s in range(1, FFN_CONV):
            acc = acc + pltpu.roll(u, s, 0) * w[KC - s]
            fix = fix + fix_ref[part, pl.ds(SUBLANES - s, SUBLANES), :] * w[KC - s]
        return jnp.concatenate([fix, acc[SUBLANES:]], axis=0)

    for c in range(D_FF // tc):
        cg = conv(slice(c * tc, (c + 1) * tc), 0)
        cv = conv(slice(D_FF + c * tc, D_FF + (c + 1) * tc), 1)
        act_ref[:, c * tc:(c + 1) * tc] = (_silu(cg) * cv).astype(BF16)

    hn = x + gate_ref[0] * _dot(act_ref[...], wd_ref[...])
    if has_final:
        ms = jnp.mean(hn * hn, axis=-1, keepdims=True)
        hn = hn * lax.rsqrt(ms + EPS) * fg_ref[...]
    out_ref[...] = hn


def _conv_ffn(h, g, shift, scale, gate, w_up, cw, cb, w_down, *, tm, seq_rows,
              prev=None, final_g=None):
    R, Dm = h.shape
    nmod, rm, _ = shift.shape
    tiles_per_mod = R // nmod // tm
    step_mode = prev is not None
    seq_tiles = 1 if step_mode else seq_rows // tm
    ntiles = R // tm
    once = pl.Buffered(1)
    const = lambda i: (0, 0)
    mod_spec = pl.BlockSpec((1, rm, Dm), lambda i: (i // tiles_per_mod, 0, 0))
    row_spec = pl.BlockSpec((tm, Dm), lambda i: (i, 0))
    in_specs = [row_spec, pl.BlockSpec((1, Dm), const), mod_spec, mod_spec, mod_spec,
                pl.BlockSpec((Dm, 2 * D_FF), const, pipeline_mode=once),
                pl.BlockSpec((FFN_CONV, 2 * D_FF), const, pipeline_mode=once),
                pl.BlockSpec((1, 2 * D_FF), const, pipeline_mode=once),
                pl.BlockSpec((D_FF, Dm), const, pipeline_mode=once)]
    args = [h, g, shift, scale, gate, w_up, cw, cb, w_down]
    scratch = [pltpu.VMEM((tm, D_FF), BF16)]
    if step_mode:
        in_specs.append(pl.BlockSpec((tm, 2 * D_FF), lambda i: (i, 0)))
        args.append(prev)
        u_spec = pl.BlockSpec((tm, 2 * D_FF), lambda i: (i, 0))
        u_shape = jax.ShapeDtypeStruct((R, 2 * D_FF), F32)
    else:
        u_spec = pl.BlockSpec((1, SUBLANES, 2 * D_FF), lambda i: (i, 0, 0))
        u_shape = jax.ShapeDtypeStruct((ntiles, SUBLANES, 2 * D_FF), F32)
        scratch += [pltpu.VMEM((SUBLANES, 2 * D_FF), F32),
                    pltpu.VMEM((2, 2 * SUBLANES, FFN_TC), F32)]
    if final_g is not None:
        in_specs.append(pl.BlockSpec((1, Dm), const))
        args.append(final_g)
    return pl.pallas_call(
        functools.partial(_ffn_kernel, tm=tm, seq_tiles=seq_tiles,
                          step_mode=step_mode, has_final=final_g is not None),
        grid=(ntiles,),
        in_specs=in_specs,
        out_specs=[row_spec, u_spec],
        out_shape=[jax.ShapeDtypeStruct((R, Dm), F32), u_shape],
        scratch_shapes=scratch,
        compiler_params=_cparams(("arbitrary",), vmem_mb=56),
        name="conv_ffn",
    )(*args)


def _band_attn_kernel(q_ref, kc_ref, kp_ref, vc_ref, vp_ref, o_ref, lse_ref):
    c = pl.program_id(2)
    BL = ATT_BLOCK
    lane = lax.broadcasted_iota(jnp.int32, (1, LANES), 1)
    i = lax.broadcasted_iota(jnp.int32, (BL, 2 * BL), 0)
    j = lax.broadcasted_iota(jnp.int32, (BL, 2 * BL), 1)
    dist = BL + i - j
    valid = (dist >= 0) & (dist <= BL) & ((c > 0) | (j >= BL))
    lane_lo = lane < HEAD_DIM
    keeps = (lane_lo, jnp.logical_not(lane_lo))
    heads = range(HEADS_PER_GROUP)
    zero = jnp.zeros((BL, LANES), BF16)
    scores = []
    for h in heads:
        ps = slice((h // 2) * LANES, (h // 2 + 1) * LANES)
        qh = jnp.where(keeps[h % 2], q_ref[0, 0, :, ps], zero)
        s = jnp.concatenate([_dot_nt(qh, kp_ref[0, 0, :, ps]), _dot_nt(qh, kc_ref[0, 0, :, ps])], axis=1)
        scores.append(jnp.where(valid, s, NEG_INF))
    probs = []
    lse = jnp.zeros((BL, LANES), F32)
    for h in heads:
        m = jnp.max(scores[h], axis=-1, keepdims=True)
        pr = jnp.exp(scores[h] - m)
        l = jnp.sum(pr, axis=-1, keepdims=True)
        lse = jnp.where(lane == h, m + jnp.log(l), lse)
        probs.append((pr.astype(BF16), jnp.where(keeps[h % 2], 1.0 / l, 0.0)))
    lse_ref[0, 0] = lse
    for p in range(HEADS_PER_GROUP // 2):
        ps = slice(p * LANES, (p + 1) * LANES)
        vc, vp = vc_ref[0, 0, :, ps], vp_ref[0, 0, :, ps]
        o_pair = None
        for hh in range(2):
            prb = probs[2 * p + hh][0]
            o = (_dot(prb[:, :BL], jnp.where(keeps[hh], vp, zero))
                 + _dot(prb[:, BL:], jnp.where(keeps[hh], vc, zero)))
            o_pair = o if o_pair is None else o_pair + o
        o_ref[0, 0, p] = o_pair * (probs[2 * p][1] + probs[2 * p + 1][1])


def _band_attention(q, k, v):
    B, d, n, W = q.shape
    nb = n // ATT_BLOCK
    cur = pl.BlockSpec((1, 1, ATT_BLOCK, W), lambda b, r, c: (b, r, c, 0))
    prv = pl.BlockSpec((1, 1, ATT_BLOCK, W), lambda b, r, c: (b, r, jnp.maximum(c - 1, 0), 0))
    return pl.pallas_call(
        _band_attn_kernel,
        grid=(B, d, nb),
        in_specs=[cur, cur, prv, cur, prv],
        out_specs=[pl.BlockSpec((1, 1, W // LANES, ATT_BLOCK, LANES), lambda b, r, c: (b, r, 0, c, 0)),
                   pl.BlockSpec((1, 1, ATT_BLOCK, LANES), lambda b, r, c: (b, r, c, 0))],
        out_shape=[jax.ShapeDtypeStruct((B, d, W // LANES, n, LANES), F32),
                   jax.ShapeDtypeStruct((B, d, n, LANES), F32)],
        compiler_params=_cparams(("parallel", "parallel", "arbitrary")),
        name="band_attention",
    )(q, k, k, v, v)


def _dec_attn_kernel(q_ref, kn_ref, vn_ref, k0_ref, v0_ref, k1_ref, v1_ref, k2_ref, v2_ref,
                     o_ref, *, T, HB):
    s = pl.program_id(1)
    caches = ((k0_ref, v0_ref), (k1_ref, v1_ref), (k2_ref, v2_ref))
    t = lax.broadcasted_iota(jnp.int32, (T, 1), 0)
    u = lax.broadcasted_iota(jnp.int32, (1, LANES), 1)
    masks = []
    for g, (win, dil) in enumerate(DIL_PAIRS):
        p = lax.broadcasted_iota(jnp.int32, (1, win), 1)
        dist = win + t - p
        cmask = (dist <= win) & ((dist & (dil - 1)) == 0)
        nmask = (u <= t) & (((t - u) & (dil - 1)) == 0)
        masks.append((cmask, nmask))
    units = [(hh, g) for hh in range(HB) for g in range(len(DIL_PAIRS))]
    scores = {}
    for hh, g in units:
        head = g * HEADS_PER_GROUP + s * HB + hh
        cmask, nmask = masks[g]
        q = q_ref[0, head].astype(BF16)
        sc = jnp.where(cmask, _dot(q, caches[g][0][0, hh].astype(BF16)), NEG_INF)
        sn = jnp.where(nmask, _dot(q, kn_ref[0, head].astype(BF16)), NEG_INF)
        scores[hh, g] = (sc, sn)
    probs = {}
    for hh, g in units:
        sc, sn = scores[hh, g]
        m = jnp.maximum(jnp.max(sc, axis=-1, keepdims=True), jnp.max(sn, axis=-1, keepdims=True))
        pc = jnp.exp(sc - m)
        pn = jnp.exp(sn - m)
        l = jnp.sum(pc, axis=-1, keepdims=True) + jnp.sum(pn, axis=-1, keepdims=True)
        probs[hh, g] = (m, l, pc.astype(BF16), pn.astype(BF16))
    outs = {}
    for hh, g in units:
        head = g * HEADS_PER_GROUP + s * HB + hh
        m, l, pc, pn = probs[hh, g]
        outs[hh, g] = (_dot_nt(pc, caches[g][1][0, hh].astype(BF16))
                       + _dot_nt(pn, vn_ref[0, head].astype(BF16)))
    for hh in range(HB):
        ms = [probs[hh, g][0] for g in range(len(DIL_PAIRS))]
        mx = jnp.maximum(jnp.maximum(ms[0], ms[1]), ms[2])
        num = None
        den = None
        for g in range(len(DIL_PAIRS)):
            w = jnp.exp(ms[g] - mx)
            num = w * outs[hh, g] if num is None else num + w * outs[hh, g]
            den = w * probs[hh, g][1] if den is None else den + w * probs[hh, g][1]
        o_ref[0, hh] = num / den


def _decode_attention(qh, knt, vnt, cache_kt, cache_vt):
    B, H, T, E = qh.shape
    P = cache_kt.shape[-1]
    HB = 8
    nhb = HEADS_PER_GROUP // HB
    specs = []
    for g, (win, dil) in enumerate(DIL_PAIRS):
        imap = functools.partial(lambda b, s, g, last: (b, g * nhb + s, 0, last),
                                 g=g, last=P // win - 1)
        specs += [pl.BlockSpec((1, HB, E, win), imap)] * 2
    full_q = pl.BlockSpec((1, H, T, E), lambda b, s: (b, 0, 0, 0))
    full_n = pl.BlockSpec((1, H, E, LANES), lambda b, s: (b, 0, 0, 0))
    return pl.pallas_call(
        functools.partial(_dec_attn_kernel, T=T, HB=HB),
        grid=(B, nhb),
        in_specs=[full_q, full_n, full_n] + specs,
        out_specs=pl.BlockSpec((1, HB, T, E), lambda b, s: (b, s, 0, 0)),
        out_shape=jax.ShapeDtypeStruct((B, HEADS_PER_GROUP, T, E), F32),
        compiler_params=_cparams(("parallel", "arbitrary"), vmem_mb=56),
        name="decode_attention",
    )(qh, knt, vnt, cache_kt, cache_vt, cache_kt, cache_vt, cache_kt, cache_vt)


def _rope_tables(pos):
    half = HEAD_DIM // 2
    inv = ROPE_THETA ** (-jnp.arange(half, dtype=F32) * (2.0 / HEAD_DIM))
    ang = pos.astype(F32)[:, None] * inv[None, :]
    cos, sin = jnp.cos(ang), jnp.sin(ang)
    cos_t = jnp.concatenate([cos, cos, cos, cos], axis=1)
    sin_t = jnp.concatenate([-sin, sin, -sin, sin], axis=1)
    return cos_t, sin_t


def _prep_params(p):
    w = {}
    w_in = p['m_w_in']
    w['w_in_zx'] = w_in[:, :, :M_D_INNER + M_CONV_DIM].astype(BF16)
    w['w_in_dt'] = jnp.pad(w_in[:, :, M_D_INNER + M_CONV_DIM:],
                           ((0, 0), (0, 0), (0, LANES - M_HEADS))).astype(BF16)
    w['dt_bias'] = jnp.pad(p['m_dt_bias'], ((0, 0), (0, LANES - M_HEADS)))[:, None, :]
    w['a_log'] = jnp.pad(p['m_A_log'], ((0, 0), (0, LANES - M_HEADS)))[:, None, :]
    w['d_skip'] = jnp.repeat(p['m_D'], M_HEADDIM, axis=1)[:, None, :]
    w['m_norm'] = p['m_norm'][:, None, :]
    w['conv_w'] = p['m_conv_w']
    w['conv_b'] = p['m_conv_b'][:, None, :]
    w['w_out'] = p['m_w_out'].astype(BF16)
    w['w_q'] = (p['w_q'] * (HEAD_DIM ** -0.5)).astype(BF16)
    w['w_kv'] = p['w_kv'].astype(BF16)
    w['w_o'] = p['w_o'].astype(BF16)
    w['ffn_up'] = p['ffn_w_up'].astype(BF16)
    w['ffn_down'] = p['ffn_w_down'].astype(BF16)
    w['ffn_cw'] = p['ffn_conv_w']
    w['ffn_cb'] = p['ffn_conv_b'][:, None, :]
    head = jnp.arange(ATT_OUT) // HEAD_DIM
    w['expand'] = (jnp.arange(LANES)[:, None] == head[None, :]).astype(BF16)
    return w


def _trunk(x, mods, kvmod, pos, ssm0, conv0, ffn0, kv_past, p, w, *, tm, step):
    B, T, Dm = x.shape
    R = B * T
    h = x.reshape(R, Dm)
    cos_t, sin_t = _rope_tables(pos)
    cos_r = jnp.tile(cos_t, (B, 1))
    sin_r = jnp.tile(sin_t, (B, 1))
    ssm_out, conv_out, ffn_out = [], [], []
    k_new = v_new = new_t = kv_split = None
    for i in range(DEPTH):
        sh1, sc1, g1, sh2, sc2, g2 = mods[i]
        if i < N_A:
            tmi = min(tm, 512)
            if step:
                prev = jnp.pad(conv0[i], ((0, 0), (SUBLANES - (M_CONV - 1), 0), (0, 0)))
                zx, dtr, u = _in_proj(h, p['norm_mix'][i][None], sh1, sc1, w['w_in_zx'][i],
                                      w['w_in_dt'][i], w['conv_w'][i], w['conv_b'][i], tm=tmi,
                                      seq_rows=T, prev=prev.reshape(R, M_CONV_DIM))
                conv_out.append(u.reshape(B, T, M_CONV_DIM)[:, T - (M_CONV - 1):])
            else:
                zx, dtr, u = _in_proj(h, p['norm_mix'][i][None], sh1, sc1, w['w_in_zx'][i],
                                      w['w_in_dt'][i], w['conv_w'][i], w['conv_b'][i], tm=tmi,
                                      seq_rows=T)
                seq_tiles = T // tmi
                conv_out.append(u[seq_tiles - 1::seq_tiles, SUBLANES - (M_CONV - 1):])
            y, h_t = _ssd_mixer(zx.reshape(B, T, -1), dtr.reshape(B, T, LANES), ssm0[i],
                                w['dt_bias'][i], w['a_log'][i], w['d_skip'][i], w['m_norm'][i])
            ssm_out.append(h_t)
            h = _proj_residual(y.reshape(R, M_D_INNER), w['w_out'][i], h, g1, tm=tmi)
        else:
            jb = i - N_A
            if step:
                q = _norm_mod_matmul(h, p['norm_mix'][i][None], sh1, sc1, w['w_q'][jb],
                                     (cos_r, sin_r), tm=tm, tn=1024, out_dtype=F32,
                                     n_rope=ATT_WIDTH // 1024)
                qh = q.reshape(B, T, ATT_HEADS, HEAD_DIM).transpose(0, 2, 1, 3)
                o = _decode_attention(qh, *new_t, *kv_past)
                o = o.transpose(0, 2, 1, 3).reshape(R, ATT_OUT).astype(BF16)
                h = _proj_residual(o, w['w_o'][jb], h, g1, tm=tm)
            else:
                q_split = _norm_mod_matmul_split(h, p['norm_mix'][i][None], sh1, sc1, w['w_q'][jb],
                                                 cos_t, sin_t, B=B, tm=SPLIT_TM, n_rope=len(DIL_PAIRS))
                o_parts, lse_parts = [], []
                for g in range(len(DIL_PAIRS)):
                    og, lg = _band_attention(q_split[g], *kv_split[g])
                    o_parts.append(og)
                    lse_parts.append(lg)
                h = _attn_out(o_parts, lse_parts, w['expand'], w['w_o'][jb], h, g1, B=B, tm=SPLIT_TM)
        last = i == DEPTH - 1
        fin = p['final_norm'][None] if last else None
        if step:
            prev = jnp.pad(ffn0[i], ((0, 0), (SUBLANES - (FFN_CONV - 1), 0), (0, 0))).reshape(R, 2 * D_FF)
            h, u = _conv_ffn(h, p['norm_ffn'][i][None], sh2, sc2, g2, w['ffn_up'][i], w['ffn_cw'][i],
                             w['ffn_cb'][i], w['ffn_down'][i], tm=tm, seq_rows=T, prev=prev, final_g=fin)
            ffn_out.append(u.reshape(B, T, 2 * D_FF)[:, T - (FFN_CONV - 1):])
        else:
            tmf = min(tm, 512)
            h, u = _conv_ffn(h, p['norm_ffn'][i][None], sh2, sc2, g2, w['ffn_up'][i], w['ffn_cw'][i],
                             w['ffn_cb'][i], w['ffn_down'][i], tm=tmf, seq_rows=T, final_g=fin)
            seq_tiles = T // tmf
            ffn_out.append(u[seq_tiles - 1::seq_tiles, SUBLANES - (FFN_CONV - 1):])
        if i == N_A - 1:
            ksh, ksc = kvmod
            keep = min(WINDOW_MAX, T)
            if step:
                h_tail, ksh_t, ksc_t, cos_k, sin_k, tm_k = h, ksh, ksc, cos_r, sin_r, tm
            else:
                ng = len(DIL_PAIRS)
                kvs = _norm_mod_matmul_split(h, p['kv_norm'][None], ksh, ksc, w['w_kv'], cos_t, sin_t,
                                             B=B, tm=SPLIT_TM, n_rope=ng)
                kv_split = [(kvs[g], kvs[ng + g]) for g in range(ng)]
                h_tail = h.reshape(B, T, Dm)[:, T - keep:].reshape(B * keep, Dm)
                ksh_t, ksc_t, tm_k = ksh, ksc, min(tm, keep)
                cos_k = jnp.tile(cos_t[T - keep:], (B, 1))
                sin_k = jnp.tile(sin_t[T - keep:], (B, 1))
            kv = _norm_mod_matmul(h_tail, p['kv_norm'][None], ksh_t, ksc_t, w['w_kv'],
                                  (cos_k, sin_k), tm=tm_k, tn=1024, out_dtype=F32,
                                  n_rope=ATT_WIDTH // 1024)
            kv = kv.reshape(B, keep, 2, ATT_HEADS, HEAD_DIM)
            k_new, v_new = kv[:, :, 0], kv[:, :, 1]
            if step:
                new_t = [jnp.pad(a.transpose(0, 2, 3, 1), ((0, 0), (0, 0), (0, 0), (0, LANES - T)))
                         for a in (k_new, v_new)]
    return (h.reshape(B, T, Dm), jnp.stack(ssm_out), jnp.stack(conv_out), jnp.stack(ffn_out),
            k_new, v_new)


def kernel(x_prompt, x_sample, state_ssm, state_conv, state_ffn_conv, cache_k, cache_v, c_prompt, c_sample, ada_w, ada_b, norm_mix, norm_ffn, m_w_in, m_conv_w, m_conv_b, m_dt_bias, m_A_log, m_D, m_norm, m_w_out, kv_norm, kv_ada_w, kv_ada_b, w_kv, w_q, w_o, ffn_w_up, ffn_conv_w, ffn_conv_b, ffn_w_down, final_norm):
    p = dict(norm_mix=norm_mix, norm_ffn=norm_ffn, m_w_in=m_w_in, m_conv_w=m_conv_w,
             m_conv_b=m_conv_b, m_dt_bias=m_dt_bias, m_A_log=m_A_log, m_D=m_D, m_norm=m_norm,
             m_w_out=m_w_out, kv_norm=kv_norm, w_kv=w_kv, w_q=w_q, w_o=w_o, ffn_w_up=ffn_w_up,
             ffn_conv_w=ffn_conv_w, ffn_conv_b=ffn_conv_b, ffn_w_down=ffn_w_down,
             final_norm=final_norm)
    w = _prep_params(p)
    Bp, S, Dm = x_prompt.shape
    Bs, T, _ = x_sample.shape

    nrow = Bp + Bs
    npad = -(-nrow // SUBLANES) * SUBLANES
    c_all = jnp.pad(jnp.concatenate([c_prompt, c_sample], axis=0), ((0, npad - nrow), (0, 0)))
    mod = _ada_linear(c_all, ada_w, ada_b[:, None, :])
    kvm = _ada_linear(c_all, kv_ada_w[None], kv_ada_b[None, None, :])[0]

    def mod_prompt(m):
        return [a[:, None, :] for a in jnp.split(m, m.shape[-1] // Dm, axis=-1)]

    def mod_sample(m):
        return [jnp.repeat(a, T, axis=0)[None] for a in jnp.split(m, m.shape[-1] // Dm, axis=-1)]

    mods_p = [mod_prompt(mod[i, :Bp]) for i in range(DEPTH)]
    mods_s = [mod_sample(mod[i, Bp:nrow]) for i in range(DEPTH)]
    kvm_p = mod_prompt(kvm[:Bp])
    kvm_s = mod_sample(kvm[Bp:nrow])

    ssm0 = jnp.zeros((N_A, Bp, M_HEADS, M_HEADDIM, M_D_STATE), state_ssm.dtype)
    y_p, ssm_p, conv_p, ffn_p, k_p, v_p = _trunk(
        x_prompt, mods_p, kvm_p, jnp.arange(S, dtype=jnp.int32), ssm0, None, None, None, p, w,
        tm=1024, step=False)
    cache_t = (cache_k.transpose(0, 2, 3, 1), cache_v.transpose(0, 2, 3, 1))
    y_s, ssm_s, conv_s, ffn_s, k_s, v_s = _trunk(
        x_sample, mods_s, kvm_s, PAST_LEN + jnp.arange(T, dtype=jnp.int32), state_ssm,
        state_conv, state_ffn_conv, cache_t, p, w, tm=Bs * T, step=True)
    return (y_p, y_s, ssm_p, ssm_s, conv_p, conv_s, ffn_p, ffn_s, k_p, k_s, v_p, v_s)
```

```python
import functools

import jax
import jax.numpy as jnp
from jax import lax
from jax.experimental import pallas as pl
from jax.experimental.pallas import tpu as pltpu

F32 = jnp.float32
BF16 = jnp.bfloat16

D_MODEL = 1024
DEPTH = 4
N_A = 2
M_D_INNER = 2048
M_HEADDIM = 64
M_HEADS = 32
M_NGROUPS = 4
M_D_STATE = 128
M_CONV = 4
M_GN = 512
M_CONV_DIM = 3072
HEAD_DIM = 64
HEADS_PER_GROUP = 16
DIL_PAIRS = ((128, 1), (512, 4), (2048, 16))
ATT_HEADS = 48
ATT_WIDTH = 3072
ATT_OUT = 1024
WINDOW_MAX = 2048
ATT_BLOCK = 128
ROPE_THETA = 10000.0
D_FF = 2816
FFN_CONV = 3
EPS = 1e-6
SSD_CHUNK = 128
PAST_LEN = 8192

LANES = 128
SUBLANES = 8
BF16_ROWS = 16
LOG2E = 1.4426950408889634
FFN_TC = 256
INPROJ_TC = 512
BAND_BLOCKS = 2
SPLIT_TM = 512
NEG_INF = float("-inf")


def _cparams(sem, vmem_mb=48):
    return pltpu.CompilerParams(dimension_semantics=sem,
                                vmem_limit_bytes=vmem_mb * 1024 * 1024)


def _silu(x):
    return x * jax.nn.sigmoid(x)


def _dot(a, b):
    return jnp.dot(a, b, preferred_element_type=F32)


def _dot_nt(a, b):
    return lax.dot_general(a, b, (((1,), (1,)), ((), ())), preferred_element_type=F32)


def _ada_kernel(c_ref, w_ref, b_ref, o_ref):
    cs = _silu(c_ref[...]).astype(BF16)
    o_ref[0] = _dot(cs, w_ref[0].astype(BF16)) + b_ref[0]


def _ada_linear(c, w, b, tn=1024):
    L, K, N = w.shape
    M = c.shape[0]
    return pl.pallas_call(
        _ada_kernel,
        grid=(L, N // tn),
        in_specs=[pl.BlockSpec((M, K), lambda l, j: (0, 0)),
                  pl.BlockSpec((1, K, tn), lambda l, j: (l, 0, j)),
                  pl.BlockSpec((1, 1, tn), lambda l, j: (l, 0, j))],
        out_specs=pl.BlockSpec((1, M, tn), lambda l, j: (l, 0, j)),
        out_shape=jax.ShapeDtypeStruct((L, M, N), F32),
        compiler_params=_cparams(("parallel", "parallel")),
        name="ada_linear",
    )(c, w, b)


def _rope_tile(acc, cos, sin):
    pieces = []
    first_half = (lax.broadcasted_iota(jnp.int32, (1, LANES), 1) & (HEAD_DIM - 1)) < (HEAD_DIM // 2)
    for c in range(acc.shape[1] // LANES):
        xc = acc[:, c * LANES:(c + 1) * LANES]
        partner = jnp.where(first_half,
                            pltpu.roll(xc, LANES - HEAD_DIM // 2, 1),
                            pltpu.roll(xc, HEAD_DIM // 2, 1))
        pieces.append(xc * cos + partner * sin)
    return jnp.concatenate(pieces, axis=1)


def _nmm_kernel(h_ref, g_ref, sh_ref, sc_ref, w_ref, cos_ref, sin_ref, o_ref, xn_ref, *, n_rope):
    j = pl.program_id(1)

    @pl.when(j == 0)
    def _():
        x = h_ref[...]
        ms = jnp.mean(x * x, axis=-1, keepdims=True)
        y = x * lax.rsqrt(ms + EPS) * g_ref[...]
        xn_ref[...] = (y * (1.0 + sc_ref[0]) + sh_ref[0]).astype(BF16)

    acc = _dot(xn_ref[...], w_ref[...])

    @pl.when(j < n_rope)
    def _():
        o_ref[...] = _rope_tile(acc, cos_ref[...], sin_ref[...]).astype(o_ref.dtype)

    @pl.when(j >= n_rope)
    def _():
        o_ref[...] = acc.astype(o_ref.dtype)


def _norm_mod_matmul(h, g, shift, scale, w, rope, *, tm, tn, out_dtype, n_rope):
    R, Dm = h.shape
    N = w.shape[1]
    nmod, rm, _ = shift.shape
    tiles_per_mod = R // nmod // tm
    mod_spec = pl.BlockSpec((1, rm, Dm), lambda i, j: (i // tiles_per_mod, 0, 0))
    tab_spec = pl.BlockSpec((tm, LANES), lambda i, j: (i, 0))
    return pl.pallas_call(
        functools.partial(_nmm_kernel, n_rope=n_rope),
        grid=(R // tm, N // tn),
        in_specs=[pl.BlockSpec((tm, Dm), lambda i, j: (i, 0)),
                  pl.BlockSpec((1, Dm), lambda i, j: (0, 0)),
                  mod_spec, mod_spec,
                  pl.BlockSpec((Dm, tn), lambda i, j: (0, j)),
                  tab_spec, tab_spec],
        out_specs=pl.BlockSpec((tm, tn), lambda i, j: (i, j)),
        out_shape=jax.ShapeDtypeStruct((R, N), out_dtype),
        scratch_shapes=[pltpu.VMEM((tm, Dm), BF16)],
        compiler_params=_cparams(("parallel", "arbitrary")),
        name="norm_mod_matmul",
    )(h, g, shift, scale, w, *rope)


def _nmm_t_kernel(h_ref, g_ref, sh_ref, sc_ref, wt_ref, cos_ref, sin_ref, o_ref, xn_ref, *, n_rope):
    j = pl.program_id(1)

    @pl.when(j == 0)
    def _():
        x = h_ref[...]
        ms = jnp.mean(x * x, axis=-1, keepdims=True)
        y = x * lax.rsqrt(ms + EPS) * g_ref[...]
        xn_ref[...] = (y * (1.0 + sc_ref[0]) + sh_ref[0]).astype(BF16)

    acc = _dot_nt(wt_ref[...], xn_ref[...])

    @pl.when(j < n_rope)
    def _():
        cos, sin = cos_ref[...], sin_ref[...]
        half = HEAD_DIM // 2
        pieces = []
        for hb in range(acc.shape[0] // HEAD_DIM):
            x = acc[hb * HEAD_DIM:(hb + 1) * HEAD_DIM]
            partner = jnp.concatenate([x[half:], x[:half]], axis=0)
            pieces.append(x * cos + partner * sin)
        o_ref[0] = jnp.concatenate(pieces, axis=0)

    @pl.when(j >= n_rope)
    def _():
        o_ref[0] = acc


def _norm_mod_matmul_t(h, g, shift, scale, wt, rope_t, *, nb, tm, tn, n_rope):
    R, Dm = h.shape
    N = wt.shape[0]
    cols = R // nb
    tiles_per_b = cols // tm
    nmod, rm, _ = shift.shape
    tiles_per_mod = R // nmod // tm
    mod_spec = pl.BlockSpec((1, rm, Dm), lambda i, j: (i // tiles_per_mod, 0, 0))
    tab_spec = pl.BlockSpec((HEAD_DIM, tm), lambda i, j: (0, i % tiles_per_b))
    return pl.pallas_call(
        functools.partial(_nmm_t_kernel, n_rope=n_rope),
        grid=(R // tm, N // tn),
        in_specs=[pl.BlockSpec((tm, Dm), lambda i, j: (i, 0)),
                  pl.BlockSpec((1, Dm), lambda i, j: (0, 0)),
                  mod_spec, mod_spec,
                  pl.BlockSpec((tn, Dm), lambda i, j: (j, 0)),
                  tab_spec, tab_spec],
        out_specs=pl.BlockSpec((1, tn, tm), lambda i, j: (i // tiles_per_b, j, i % tiles_per_b)),
        out_shape=jax.ShapeDtypeStruct((nb, N, cols), F32),
        scratch_shapes=[pltpu.VMEM((tm, Dm), BF16)],
        compiler_params=_cparams(("parallel", "arbitrary")),
        name="norm_mod_matmul_t",
    )(h, g, shift, scale, wt, *rope_t)


def _nmm_split_kernel(*refs, n_rope, n_out, tm):
    h_ref, g_ref, sh_ref, sc_ref, w_ref, cos_ref, sin_ref = refs[:7]
    outs = refs[7:7 + n_out]
    xf_ref, xn_ref = refs[7 + n_out:]
    ng = len(DIL_PAIRS)
    W = ATT_OUT

    x = h_ref[...]
    ms = jnp.mean(x * x, axis=-1, keepdims=True)
    y = x * lax.rsqrt(ms + EPS) * g_ref[...]
    xn = y * (1.0 + sc_ref[0]) + sh_ref[0]
    nlc = xn.shape[1] // LANES
    for lc in range(nlc):
        xf_ref[lc] = xn[:, lc * LANES:(lc + 1) * LANES]
    for g, (win, d) in enumerate(DIL_PAIRS):
        rows = tm // d
        if d == 1:
            xn_ref[g] = xn.astype(BF16)
        else:
            for r in range(d):
                for lc in range(nlc):
                    xn_ref[g, r * rows:(r + 1) * rows, lc * LANES:(lc + 1) * LANES] = (
                        xf_ref.at[lc][pl.ds(r, rows, stride=d), :].astype(BF16))

    for k in range(n_out):
        g = k % ng
        acc = _dot(xn_ref[g], w_ref[:, k * W:(k + 1) * W])
        val = _rope_tile(acc, cos_ref[g], sin_ref[g]) if k < n_rope else acc
        d = DIL_PAIRS[g][1]
        rows = tm // d
        for r in range(d):
            outs[k][0, r] = val[r * rows:(r + 1) * rows].astype(outs[k].dtype)


def _split_rows(x, d, tm):
    T, C = x.shape
    return x.reshape(T // tm, tm // d, d, C).transpose(0, 2, 1, 3).reshape(T, C)


def _norm_mod_matmul_split(h, g, shift, scale, w, cos_t, sin_t, *, B, tm, n_rope):
    R, Dm = h.shape
    S = R // B
    W = ATT_OUT
    n_out = w.shape[1] // W
    seq_tiles = S // tm
    ng = len(DIL_PAIRS)
    cos_g = jnp.stack([_split_rows(cos_t, d, tm) for _, d in DIL_PAIRS])
    sin_g = jnp.stack([_split_rows(sin_t, d, tm) for _, d in DIL_PAIRS])
    mod_spec = pl.BlockSpec((1, 1, Dm), lambda i: (i // seq_tiles, 0, 0))
    tab_spec = pl.BlockSpec((ng, tm, LANES), lambda i: (0, i % seq_tiles, 0))
    out_specs, out_shape = [], []
    for k in range(n_out):
        d = DIL_PAIRS[k % ng][1]
        out_specs.append(pl.BlockSpec((1, d, tm // d, W),
                                      lambda i: (i // seq_tiles, 0, i % seq_tiles, 0)))
        out_shape.append(jax.ShapeDtypeStruct((B, d, S // d, W), BF16))
    return pl.pallas_call(
        functools.partial(_nmm_split_kernel, n_rope=n_rope, n_out=n_out, tm=tm),
        grid=(R // tm,),
        in_specs=[pl.BlockSpec((tm, Dm), lambda i: (i, 0)),
                  pl.BlockSpec((1, Dm), lambda i: (0, 0)),
                  mod_spec, mod_spec,
                  pl.BlockSpec((Dm, n_out * W), lambda i: (0, 0), pipeline_mode=pl.Buffered(1)),
                  tab_spec, tab_spec],
        out_specs=out_specs,
        out_shape=out_shape,
        scratch_shapes=[pltpu.VMEM((Dm // LANES, tm, LANES), F32), pltpu.VMEM((ng, tm, Dm), BF16)],
        compiler_params=_cparams(("parallel",), vmem_mb=56),
        name="norm_mod_matmul_split",
    )(h, g, shift, scale, w, cos_g, sin_g)


def _decay_terms(dt_raw, dtb, alog, valid_rows=None):
    Q = dt_raw.shape[0]
    dt = dt_raw + dtb
    dt = jnp.maximum(dt, 0.0) + jnp.log(1.0 + jnp.exp(-jnp.abs(dt)))
    if valid_rows is not None:
        dt = jnp.where(lax.broadcasted_iota(jnp.int32, (Q, 1), 0) < valid_rows, dt, 0.0)
    a = dt * (-jnp.exp(alog) * LOG2E)
    ri = lax.broadcasted_iota(jnp.int32, (Q, Q), 0)
    ci = lax.broadcasted_iota(jnp.int32, (Q, Q), 1)
    tril = jnp.where(ri >= ci, 1.0, 0.0).astype(F32)
    acum = jnp.dot(tril, a, preferred_element_type=F32, precision=lax.Precision.HIGHEST)
    return dt, acum


def _inproj_kernel(*refs, tm, seq_tiles, step_mode):
    h_ref, g_ref, sh_ref, sc_ref, w_ref, wdt_ref, cw_ref, cb_ref = refs[:8]
    k = 8
    if step_mode:
        pp_ref = refs[k]
        k += 1
    else:
        dtb_ref, alog_ref = refs[k:k + 2]
        k += 2
    zx_ref, dt_ref, u_ref = refs[k:k + 3]
    k += 3
    if not step_mode:
        ac_ref, act_ref, carry_ref, fix_ref = refs[k:k + 4]
    i = pl.program_id(0)
    KC = M_CONV - 1
    tc = INPROJ_TC

    x = h_ref[...]
    ms = jnp.mean(x * x, axis=-1, keepdims=True)
    xn = (x * lax.rsqrt(ms + EPS) * g_ref[...] * (1.0 + sc_ref[0]) + sh_ref[0]).astype(BF16)
    dt_raw = _dot(xn, wdt_ref[...])
    if step_mode:
        dt_ref[...] = dt_raw
    else:
        for ck in range(tm // SSD_CHUNK):
            rows = slice(ck * SSD_CHUNK, (ck + 1) * SSD_CHUNK)
            dt, acum = _decay_terms(dt_raw[rows], dtb_ref[...], alog_ref[...])
            dt_ref[rows, :] = dt
            ac_ref[rows, :] = acum
            act_ref[ck] = acum.T
    for c in range(M_D_INNER // tc):
        cols = slice(c * tc, (c + 1) * tc)
        zx_ref[:, cols] = _dot(xn, w_ref[:, cols])

    if step_mode:
        t = lax.broadcasted_iota(jnp.int32, (tm, 1), 0) & (SUBLANES - 1)
    else:
        @pl.when(i % seq_tiles == 0)
        def _():
            carry_ref[...] = jnp.zeros_like(carry_ref)

    for c in range(M_CONV_DIM // tc):
        cols = slice(c * tc, (c + 1) * tc)
        u = _dot(xn, w_ref[:, M_D_INNER + c * tc:M_D_INNER + (c + 1) * tc])
        w = [cw_ref[kk:kk + 1, cols] for kk in range(M_CONV)]
        b = cb_ref[:, cols]
        if step_mode:
            u_ref[:, cols] = u
            pp = pp_ref[:, cols]
            acc = u * w[KC] + b
            for s in range(1, M_CONV):
                term = jnp.where(t >= s, pltpu.roll(u, s, 0), pltpu.roll(pp, tm + s - SUBLANES, 0))
                acc = acc + term * w[KC - s]
        else:
            fix_ref[0:SUBLANES, :] = carry_ref[:, cols]
            fix_ref[SUBLANES:2 * SUBLANES, :] = u[0:SUBLANES]
            last = u[tm - SUBLANES:tm]
            carry_ref[:, cols] = last
            u_ref[0, :, cols] = last
            acc = u * w[KC] + b
            fix = fix_ref[SUBLANES:2 * SUBLANES, :] * w[KC] + b
            for s in range(1, M_CONV):
                acc = acc + pltpu.roll(u, s, 0) * w[KC - s]
                fix = fix + fix_ref[pl.ds(SUBLANES - s, SUBLANES), :] * w[KC - s]
            acc = jnp.concatenate([fix, acc[SUBLANES:]], axis=0)
        zx_ref[:, M_D_INNER + c * tc:M_D_INNER + (c + 1) * tc] = _silu(acc)


def _in_proj(h, g, shift, scale, w_zx, w_dt, cw, cb, dtb, alog, *, tm, seq_rows, prev=None):
    R, Dm = h.shape
    N = w_zx.shape[1]
    nmod, rm, _ = shift.shape
    tiles_per_mod = R // nmod // tm
    step_mode = prev is not None
    seq_tiles = 1 if step_mode else seq_rows // tm
    ntiles = R // tm
    once = pl.Buffered(1)
    const = lambda i: (0, 0)
    mod_spec = pl.BlockSpec((1, rm, Dm), lambda i: (i // tiles_per_mod, 0, 0))
    in_specs = [pl.BlockSpec((tm, Dm), lambda i: (i, 0)), pl.BlockSpec((1, Dm), const),
                mod_spec, mod_spec,
                pl.BlockSpec((Dm, N), const, pipeline_mode=once),
                pl.BlockSpec((Dm, LANES), const, pipeline_mode=once),
                pl.BlockSpec((M_CONV, M_CONV_DIM), const, pipeline_mode=once),
                pl.BlockSpec((1, M_CONV_DIM), const, pipeline_mode=once)]
    args = [h, g, shift, scale, w_zx, w_dt, cw, cb]
    scratch = []
    lane_spec = pl.BlockSpec((tm, LANES), lambda i: (i, 0))
    out_specs = [pl.BlockSpec((tm, N), lambda i: (i, 0)), lane_spec]
    out_shape = [jax.ShapeDtypeStruct((R, N), F32), jax.ShapeDtypeStruct((R, LANES), F32)]
    if step_mode:
        in_specs.append(pl.BlockSpec((tm, M_CONV_DIM), lambda i: (i, 0)))
        args.append(prev)
        out_specs.append(pl.BlockSpec((tm, M_CONV_DIM), lambda i: (i, 0)))
        out_shape.append(jax.ShapeDtypeStruct((R, M_CONV_DIM), F32))
    else:
        in_specs += [pl.BlockSpec((1, LANES), const)] * 2
        args += [dtb, alog]
        cpt = tm // SSD_CHUNK
        out_specs += [pl.BlockSpec((1, SUBLANES, M_CONV_DIM), lambda i: (i, 0, 0)), lane_spec,
                      pl.BlockSpec((cpt, SSD_CHUNK, LANES), lambda i: (i, 0, 0))]
        out_shape += [jax.ShapeDtypeStruct((ntiles, SUBLANES, M_CONV_DIM), F32),
                      jax.ShapeDtypeStruct((R, LANES), F32),
                      jax.ShapeDtypeStruct((R // SSD_CHUNK, SSD_CHUNK, LANES), F32)]
        scratch = [pltpu.VMEM((SUBLANES, M_CONV_DIM), F32),
                   pltpu.VMEM((2 * SUBLANES, INPROJ_TC), F32)]
    return pl.pallas_call(
        functools.partial(_inproj_kernel, tm=tm, seq_tiles=seq_tiles, step_mode=step_mode),
        grid=(ntiles,),
        in_specs=in_specs,
        out_specs=out_specs,
        out_shape=out_shape,
        scratch_shapes=scratch,
        compiler_params=_cparams(("arbitrary",), vmem_mb=56),
        name="in_proj",
    )(*args)


def _ssd_kernel(*refs, Q, Tv):
    pre = Tv == Q
    zx_ref, dt_ref = refs[:2]
    k = 2
    if pre:
        ac_ref, act_ref = refs[k:k + 2]
    else:
        dtb_ref, alog_ref = refs[k:k + 2]
    k += 2
    h0_ref, dskip_ref, nw_ref, wout_ref, h_ref, gate_ref, out_ref, hout_ref, hT_ref, ybuf_ref = refs[k:k + 10]
    k += 10
    c = pl.program_id(1)
    npair = M_HEADS // M_NGROUPS // 2

    @pl.when(c == 0)
    def _():
        for g in range(M_NGROUPS):
            for p in range(npair):
                h0 = g * 2 * npair + 2 * p
                pair = h0_ref[0, h0:h0 + 2].reshape(2 * M_HEADDIM, M_D_STATE)
                hT_ref[g, :, p * LANES:(p + 1) * LANES] = pair.T

    if pre:
        zfull = None
        xact = zx_ref[0, :, M_D_INNER:]
        dt, acum, acum_t = dt_ref[0], ac_ref[0], act_ref[0]
    else:
        pad_ref, dtp_ref = refs[k:k + 2]
        pad_ref[...] = jnp.zeros_like(pad_ref)
        pad_ref[0:Tv, :] = zx_ref[0]
        dtp_ref[...] = jnp.zeros_like(dtp_ref)
        dtp_ref[0:Tv, :] = dt_ref[0]
        zfull = pad_ref[:, 0:M_D_INNER]
        xact = pad_ref[:, M_D_INNER:]
        dt, acum = _decay_terms(dtp_ref[...], dtb_ref[...], alog_ref[...], Tv)
        acum_t = acum.T
    causal = (lax.broadcasted_iota(jnp.int32, (Q, Q), 0)
              >= lax.broadcasted_iota(jnp.int32, (Q, Q), 1))
    lane_lo = lax.broadcasted_iota(jnp.int32, (1, LANES), 1) < M_HEADDIM

    for g in range(M_NGROUPS):
        gs = slice(g * M_GN, (g + 1) * M_GN)
        bsl = slice(M_D_INNER + g * M_D_STATE, M_D_INNER + (g + 1) * M_D_STATE)
        csl = slice(M_D_INNER + M_GN + g * M_D_STATE, M_D_INNER + M_GN + (g + 1) * M_D_STATE)
        b_f = xact[:, bsl]
        b_g = b_f.astype(BF16)
        c_g = xact[:, csl].astype(BF16)
        b_gt = b_f.T.astype(BF16)
        cb = _dot_nt(c_g, b_g)
        ydiag, e_parts, dec_parts = [], [], []
        for p in range(4):
            h0 = g * 8 + 2 * p
            xp = xact[:, h0 * M_HEADDIM:(h0 + 2) * M_HEADDIM]
            col0 = acum[:, h0:h0 + 1]
            col1 = acum[:, h0 + 1:h0 + 2]
            dtp = jnp.where(lane_lo, dt[:, h0:h0 + 1], dt[:, h0 + 1:h0 + 2])
            ap = jnp.where(lane_lo, col0, col1)
            xdt = xp * dtp
            e_parts.append(jnp.exp2(ap))
            dec = jnp.exp2(ap[Q - 1:Q, :] - ap)
            dec_parts.append((xdt * dec).astype(BF16))
            yp = None
            for hh, col, keep in ((h0, col0, lane_lo), (h0 + 1, col1, jnp.logical_not(lane_lo))):
                seg = col - acum_t[hh:hh + 1, :]
                lm = jnp.exp2(jnp.where(causal, seg, NEG_INF))
                m = (cb * lm).astype(BF16)
                t = _dot(m, jnp.where(keep, xdt, 0.0).astype(BF16))
                yp = t if yp is None else yp + t
            ydiag.append(yp)
        e_g = jnp.concatenate(e_parts, axis=1)
        xdec_g = jnp.concatenate(dec_parts, axis=1)
        h_prev = hT_ref[g]
        y_off = _dot(c_g, h_prev.astype(BF16)) * e_g
        hT_ref[g] = h_prev * e_g[Q - 1:Q, :] + _dot(b_gt, xdec_g)
        y = jnp.concatenate(ydiag, axis=1) + y_off + xact[:, gs] * dskip_ref[:, gs]
        yg = y * _silu(zx_ref[0, :, gs] if zfull is None else zfull[:, gs])
        ms = jnp.mean(yg * yg, axis=-1, keepdims=True)
        ybuf_ref[:, gs] = (yg * lax.rsqrt(ms + EPS) * nw_ref[:, gs]).astype(BF16)

    tb = -(-Tv // BF16_ROWS) * BF16_ROWS
    out_ref[0] = h_ref[0] + gate_ref[0] * _dot(ybuf_ref[0:tb, :], wout_ref[...])[0:Tv]

    @pl.when(c == pl.num_programs(1) - 1)
    def _():
        for g in range(M_NGROUPS):
            for p in range(npair):
                h0 = g * 2 * npair + 2 * p
                pair = hT_ref[g, :, p * LANES:(p + 1) * LANES].T
                hout_ref[0, h0:h0 + 2] = pair.reshape(2, M_HEADDIM, M_D_STATE)


def _ssd_mixer(zx, dtr, decay, h0, dskip, nw, w_out, h, gate):
    B, L, W = zx.shape
    Dm = h.shape[-1]
    Q = SSD_CHUNK
    Tv = min(L, Q)
    nc = L // Tv
    const2 = lambda b, c: (0, 0)
    row3 = lambda b, c: (b, c, 0)
    state_spec = pl.BlockSpec((1, M_HEADS, M_HEADDIM, M_D_STATE), lambda b, c: (b, 0, 0, 0))
    scratch = [pltpu.VMEM((M_NGROUPS, M_D_STATE, M_GN), F32), pltpu.VMEM((Q, M_D_INNER), BF16)]
    if Tv == Q:
        decay_specs = [pl.BlockSpec((1, Q, LANES), row3),
                       pl.BlockSpec((1, Q, LANES), lambda b, c: (b * nc + c, 0, 0))]
    else:
        decay_specs = [pl.BlockSpec((1, LANES), const2)] * 2
        scratch += [pltpu.VMEM((Q, W), F32), pltpu.VMEM((Q, LANES), F32)]
    return pl.pallas_call(
        functools.partial(_ssd_kernel, Q=Q, Tv=Tv),
        grid=(B, nc),
        in_specs=[pl.BlockSpec((1, Tv, W), row3),
                  pl.BlockSpec((1, Tv, LANES), row3)]
        + decay_specs
        + [state_spec,
           pl.BlockSpec((1, M_D_INNER), const2),
           pl.BlockSpec((1, M_D_INNER), const2),
           pl.BlockSpec((M_D_INNER, Dm), const2, pipeline_mode=pl.Buffered(1)),
           pl.BlockSpec((1, Tv, Dm), row3),
           pl.BlockSpec((1, 1, Dm), lambda b, c: (b, 0, 0))],
        out_specs=[pl.BlockSpec((1, Tv, Dm), row3), state_spec],
        out_shape=[jax.ShapeDtypeStruct((B, L, Dm), F32),
                   jax.ShapeDtypeStruct((B, M_HEADS, M_HEADDIM, M_D_STATE), F32)],
        scratch_shapes=scratch,
        compiler_params=_cparams(("parallel", "arbitrary")),
        name="ssd_mixer",
    )(zx, dtr, *decay, h0, dskip, nw, w_out, h, gate)


def _proj_res_kernel(a_ref, w_ref, h_ref, gate_ref, o_ref):
    o_ref[...] = h_ref[...] + gate_ref[0] * _dot(a_ref[...], w_ref[...])


def _proj_residual(a, w, h, gate, *, tm):
    R, K = a.shape
    Dm = w.shape[1]
    nmod, rm, _ = gate.shape
    tiles_per_mod = R // nmod // tm
    return pl.pallas_call(
        _proj_res_kernel,
        grid=(R // tm,),
        in_specs=[pl.BlockSpec((tm, K), lambda i: (i, 0)),
                  pl.BlockSpec((K, Dm), lambda i: (0, 0)),
                  pl.BlockSpec((tm, Dm), lambda i: (i, 0)),
                  pl.BlockSpec((1, rm, Dm), lambda i: (i // tiles_per_mod, 0, 0))],
        out_specs=pl.BlockSpec((tm, Dm), lambda i: (i, 0)),
        out_shape=jax.ShapeDtypeStruct((R, Dm), F32),
        compiler_params=_cparams(("parallel",)),
        name="proj_residual",
    )(a, w, h, gate)


def _attn_out_kernel(o0_ref, o1_ref, o2_ref, l0_ref, l1_ref, l2_ref, e_ref, w_ref, h_ref,
                     gate_ref, out_ref, nat_o, nat_l, *, tm):
    nlc = ATT_OUT // LANES
    for g, (o_ref, l_ref) in enumerate(((o0_ref, l0_ref), (o1_ref, l1_ref), (o2_ref, l2_ref))):
        d = DIL_PAIRS[g][1]
        rows = tm // d
        for r in range(d):
            sl = pl.ds(0, tm) if d == 1 else pl.ds(r, rows, stride=d)
            nat_l.at[g][sl, :] = l_ref[0, r]
            for lc in range(nlc):
                nat_o.at[g * nlc + lc][sl, :] = o_ref[0, r, lc]
    l0, l1, l2 = nat_l[0], nat_l[1], nat_l[2]
    mx = jnp.maximum(jnp.maximum(l0, l1), l2)
    w0, w1, w2 = jnp.exp2(l0 - mx), jnp.exp2(l1 - mx), jnp.exp2(l2 - mx)
    inv = 1.0 / (w0 + w1 + w2)
    alphas = []
    for wg in (w0, w1, w2):
        a = wg * inv
        hi = a.astype(BF16)
        lo = (a - hi.astype(F32)).astype(BF16)
        alphas.append(_dot(hi, e_ref[...]) + _dot(lo, e_ref[...]))
    pieces = []
    for lc in range(nlc):
        ls = slice(lc * LANES, (lc + 1) * LANES)
        t = alphas[0][:, ls] * nat_o[lc]
        for g in range(1, len(DIL_PAIRS)):
            t = t + alphas[g][:, ls] * nat_o[g * nlc + lc]
        pieces.append(t.astype(BF16))
    comb = jnp.concatenate(pieces, axis=1)
    out_ref[...] = h_ref[...] + gate_ref[0] * _dot(comb, w_ref[...])


def _attn_out(o_parts, lse_parts, expand, w, h, gate, *, B, tm):
    R, Dm = h.shape
    seq_tiles = R // B // tm
    nlc = ATT_OUT // LANES
    ng = len(DIL_PAIRS)
    row = lambda i: (i, 0)
    o_specs = [pl.BlockSpec((1, d, nlc, tm // d, LANES),
                            lambda i: (i // seq_tiles, 0, 0, i % seq_tiles, 0)) for _, d in DIL_PAIRS]
    l_specs = [pl.BlockSpec((1, d, tm // d, LANES),
                            lambda i: (i // seq_tiles, 0, i % seq_tiles, 0)) for _, d in DIL_PAIRS]
    return pl.pallas_call(
        functools.partial(_attn_out_kernel, tm=tm),
        grid=(R // tm,),
        in_specs=o_specs + l_specs
        + [pl.BlockSpec((LANES, ATT_OUT), lambda i: (0, 0)),
           pl.BlockSpec((ATT_OUT, Dm), lambda i: (0, 0)),
           pl.BlockSpec((tm, Dm), row),
           pl.BlockSpec((1, 1, Dm), lambda i: (i // seq_tiles, 0, 0))],
        out_specs=pl.BlockSpec((tm, Dm), row),
        out_shape=jax.ShapeDtypeStruct((R, Dm), F32),
        scratch_shapes=[pltpu.VMEM((ng * nlc, tm, LANES), F32), pltpu.VMEM((ng, tm, LANES), F32)],
        compiler_params=_cparams(("parallel",)),
        name="attn_out",
    )(*o_parts, *lse_parts, expand, w, h, gate)


def _ffn_kernel(*refs, tm, seq_tiles, step_mode, has_final):
    h_ref, g_ref, sh_ref, sc_ref, gate_ref, wup_ref, cw_ref, cb_ref, wd_ref = refs[:9]
    k = 9
    if step_mode:
        pp_ref = refs[k]
        k += 1
    if has_final:
        fg_ref = refs[k]
        k += 1
    out_ref, u_ref, act_ref = refs[k:k + 3]
    k += 3
    if not step_mode:
        carry_ref, fix_ref = refs[k:k + 2]
    i = pl.program_id(0)
    KC = FFN_CONV - 1
    tc = FFN_TC

    x = h_ref[...]
    ms = jnp.mean(x * x, axis=-1, keepdims=True)
    xn = (x * lax.rsqrt(ms + EPS) * g_ref[...] * (1.0 + sc_ref[0]) + sh_ref[0]).astype(BF16)

    if step_mode:
        t = lax.broadcasted_iota(jnp.int32, (tm, 1), 0) & (SUBLANES - 1)
    else:
        @pl.when(i % seq_tiles == 0)
        def _():
            carry_ref[...] = jnp.zeros_like(carry_ref)

    def conv(cols, part):
        u = _dot(xn, wup_ref[:, cols])
        w = [cw_ref[kk:kk + 1, cols] for kk in range(FFN_CONV)]
        b = cb_ref[:, cols]
        if step_mode:
            u_ref[:, cols] = u
            pp = pp_ref[:, cols]
            acc = u * w[KC]
            for s in range(1, FFN_CONV):
                term = jnp.where(t >= s, pltpu.roll(u, s, 0), pltpu.roll(pp, tm + s - SUBLANES, 0))
                acc = acc + term * w[KC - s]
            return acc + b
        fix_ref[part, 0:SUBLANES, :] = carry_ref[:, cols]
        fix_ref[part, SUBLANES:2 * SUBLANES, :] = u[0:SUBLANES]
        last = u[tm - SUBLANES:tm]
        carry_ref[:, cols] = last
        u_ref[0, :, cols] = last
        acc = u * w[KC] + b
        fix = fix_ref[part, SUBLANES:2 * SUBLANES, :] * w[KC] + b
        for s in range(1, FFN_CONV):
            acc = acc + pltpu.roll(u, s, 0) * w[KC - s]
            fix = fix + fix_ref[part, pl.ds(SUBLANES - s, SUBLANES), :] * w[KC - s]
        return jnp.concatenate([fix, acc[SUBLANES:]], axis=0)

    for c in range(D_FF // tc):
        cg = conv(slice(c * tc, (c + 1) * tc), 0)
        cv = conv(slice(D_FF + c * tc, D_FF + (c + 1) * tc), 1)
        act_ref[:, c * tc:(c + 1) * tc] = (_silu(cg) * cv).astype(BF16)

    hn = x + gate_ref[0] * _dot(act_ref[...], wd_ref[...])
    if has_final:
        ms = jnp.mean(hn * hn, axis=-1, keepdims=True)
        hn = hn * lax.rsqrt(ms + EPS) * fg_ref[...]
    out_ref[...] = hn


def _conv_ffn(h, g, shift, scale, gate, w_up, cw, cb, w_down, *, tm, seq_rows,
              prev=None, final_g=None):
    R, Dm = h.shape
    nmod, rm, _ = shift.shape
    tiles_per_mod = R // nmod // tm
    step_mode = prev is not None
    seq_tiles = 1 if step_mode else seq_rows // tm
    ntiles = R // tm
    once = pl.Buffered(1)
    const = lambda i: (0, 0)
    mod_spec = pl.BlockSpec((1, rm, Dm), lambda i: (i // tiles_per_mod, 0, 0))
    row_spec = pl.BlockSpec((tm, Dm), lambda i: (i, 0))
    in_specs = [row_spec, pl.BlockSpec((1, Dm), const), mod_spec, mod_spec, mod_spec,
                pl.BlockSpec((Dm, 2 * D_FF), const, pipeline_mode=once),
                pl.BlockSpec((FFN_CONV, 2 * D_FF), const, pipeline_mode=once),
                pl.BlockSpec((1, 2 * D_FF), const, pipeline_mode=once),
                pl.BlockSpec((D_FF, Dm), const, pipeline_mode=once)]
    args = [h, g, shift, scale, gate, w_up, cw, cb, w_down]
    scratch = [pltpu.VMEM((tm, D_FF), BF16)]
    if step_mode:
        in_specs.append(pl.BlockSpec((tm, 2 * D_FF), lambda i: (i, 0)))
        args.append(prev)
        u_spec = pl.BlockSpec((tm, 2 * D_FF), lambda i: (i, 0))
        u_shape = jax.ShapeDtypeStruct((R, 2 * D_FF), F32)
    else:
        u_spec = pl.BlockSpec((1, SUBLANES, 2 * D_FF), lambda i: (i, 0, 0))
        u_shape = jax.ShapeDtypeStruct((ntiles, SUBLANES, 2 * D_FF), F32)
        scratch += [pltpu.VMEM((SUBLANES, 2 * D_FF), F32),
                    pltpu.VMEM((2, 2 * SUBLANES, FFN_TC), F32)]
    if final_g is not None:
        in_specs.append(pl.BlockSpec((1, Dm), const))
        args.append(final_g)
    return pl.pallas_call(
        functools.partial(_ffn_kernel, tm=tm, seq_tiles=seq_tiles,
                          step_mode=step_mode, has_final=final_g is not None),
        grid=(ntiles,),
        in_specs=in_specs,
        out_specs=[row_spec, u_spec],
        out_shape=[jax.ShapeDtypeStruct((R, Dm), F32), u_shape],
        scratch_shapes=scratch,
        compiler_params=_cparams(("arbitrary",), vmem_mb=56),
        name="conv_ffn",
    )(*args)


def _band_attn_kernel(q_ref, kc_ref, kp_ref, vc_ref, vp_ref, o_ref, lse_ref, *, nbk):
    c = pl.program_id(2)
    BL = ATT_BLOCK
    lane = lax.broadcasted_iota(jnp.int32, (1, LANES), 1)
    i = lax.broadcasted_iota(jnp.int32, (BL, BL), 0)
    j = lax.broadcasted_iota(jnp.int32, (BL, BL), 1)
    cur_ok = j <= i
    prev_ok = j >= i
    prev_ok0 = prev_ok & (c > 0)
    lane_lo = lane < HEAD_DIM
    keeps = (lane_lo, jnp.logical_not(lane_lo))
    zero = jnp.zeros((BL, LANES), BF16)
    units = [(sb, h) for sb in range(nbk) for h in range(HEADS_PER_GROUP)]

    def kv_prev(cur_ref, prev_ref, sb, ps):
        if sb == 0:
            return prev_ref[0, 0, :, ps]
        return cur_ref[0, 0, (sb - 1) * BL:sb * BL, ps]

    scores = {}
    for sb, h in units:
        ps = slice((h // 2) * LANES, (h // 2 + 1) * LANES)
        rows = slice(sb * BL, (sb + 1) * BL)
        qh = jnp.where(keeps[h % 2], q_ref[0, 0, rows, ps], zero)
        s_p = jnp.where(prev_ok0 if sb == 0 else prev_ok, _dot_nt(qh, kv_prev(kc_ref, kp_ref, sb, ps)), NEG_INF)
        s_c = jnp.where(cur_ok, _dot_nt(qh, kc_ref[0, 0, rows, ps]), NEG_INF)
        scores[sb, h] = (s_p, s_c)
    probs = {}
    for sb in range(nbk):
        lse = jnp.zeros((BL, LANES), F32)
        for h in range(HEADS_PER_GROUP):
            s_p, s_c = scores[sb, h]
            m = jnp.max(jnp.maximum(s_p, s_c), axis=-1, keepdims=True)
            p_p = jnp.exp2(s_p - m)
            p_c = jnp.exp2(s_c - m)
            l = jnp.sum(p_p + p_c, axis=-1, keepdims=True)
            lse = jnp.where(lane == h, m + jnp.log(l) * LOG2E, lse)
            probs[sb, h] = (p_p.astype(BF16), p_c.astype(BF16), jnp.where(keeps[h % 2], 1.0 / l, 0.0))
        lse_ref[0, 0, sb * BL:(sb + 1) * BL, :] = lse
    for sb in range(nbk):
        rows = slice(sb * BL, (sb + 1) * BL)
        for p in range(HEADS_PER_GROUP // 2):
            ps = slice(p * LANES, (p + 1) * LANES)
            vc, vp = vc_ref[0, 0, rows, ps], kv_prev(vc_ref, vp_ref, sb, ps)
            o_pair = None
            for hh in range(2):
                p_p, p_c, _ = probs[sb, 2 * p + hh]
                o = _dot(p_p, jnp.where(keeps[hh], vp, zero)) + _dot(p_c, jnp.where(keeps[hh], vc, zero))
                o_pair = o if o_pair is None else o_pair + o
            o_ref[0, 0, p, rows, :] = o_pair * (probs[sb, 2 * p][2] + probs[sb, 2 * p + 1][2])


def _band_attention(q, k, v):
    B, d, n, W = q.shape
    nbk = BAND_BLOCKS
    blk = nbk * ATT_BLOCK
    cur = pl.BlockSpec((1, 1, blk, W), lambda b, r, c: (b, r, c, 0))
    prv = pl.BlockSpec((1, 1, ATT_BLOCK, W), lambda b, r, c: (b, r, jnp.maximum(c * nbk - 1, 0), 0))
    return pl.pallas_call(
        functools.partial(_band_attn_kernel, nbk=nbk),
        grid=(B, d, n // blk),
        in_specs=[cur, cur, prv, cur, prv],
        out_specs=[pl.BlockSpec((1, 1, W // LANES, blk, LANES), lambda b, r, c: (b, r, 0, c, 0)),
                   pl.BlockSpec((1, 1, blk, LANES), lambda b, r, c: (b, r, c, 0))],
        out_shape=[jax.ShapeDtypeStruct((B, d, W // LANES, n, LANES), F32),
                   jax.ShapeDtypeStruct((B, d, n, LANES), F32)],
        compiler_params=_cparams(("parallel", "parallel", "arbitrary")),
        name="band_attention",
    )(q, k, k, v, v)


def _dec_attn_kernel(q_ref, kn_ref, vn_ref, k0_ref, v0_ref, k1_ref, v1_ref, k2_ref, v2_ref,
                     o_ref, *, T, HB):
    s = pl.program_id(1)
    caches = ((k0_ref, v0_ref), (k1_ref, v1_ref), (k2_ref, v2_ref))
    t = lax.broadcasted_iota(jnp.int32, (T, 1), 0)
    off = (pl.program_id(0) % (LANES // T)) * T
    u = lax.broadcasted_iota(jnp.int32, (1, LANES), 1) - off
    masks = []
    for g, (win, dil) in enumerate(DIL_PAIRS):
        p = lax.broadcasted_iota(jnp.int32, (1, win), 1)
        dist = win + t - p
        cmask = (dist <= win) & ((dist & (dil - 1)) == 0)
        nmask = (u >= 0) & (u <= t) & (((t - u) & (dil - 1)) == 0)
        masks.append((cmask, nmask))
    units = [(hh, g) for hh in range(HB) for g in range(len(DIL_PAIRS))]
    scores = {}
    for hh, g in units:
        head = g * HEADS_PER_GROUP + s * HB + hh
        cmask, nmask = masks[g]
        q = q_ref[0, head].astype(BF16)
        sc = jnp.where(cmask, _dot(q, caches[g][0][0, hh].astype(BF16)), NEG_INF)
        sn = jnp.where(nmask, _dot(q, kn_ref[head].astype(BF16)), NEG_INF)
        scores[hh, g] = (sc, sn)
    probs = {}
    for hh, g in units:
        sc, sn = scores[hh, g]
        m = jnp.maximum(jnp.max(sc, axis=-1, keepdims=True), jnp.max(sn, axis=-1, keepdims=True))
        pc = jnp.exp2(sc - m)
        pn = jnp.exp2(sn - m)
        l = jnp.sum(pc, axis=-1, keepdims=True) + jnp.sum(pn, axis=-1, keepdims=True)
        probs[hh, g] = (m, l, pc.astype(BF16), pn.astype(BF16))
    outs = {}
    for hh, g in units:
        head = g * HEADS_PER_GROUP + s * HB + hh
        m, l, pc, pn = probs[hh, g]
        outs[hh, g] = (_dot_nt(pc, caches[g][1][0, hh].astype(BF16))
                       + _dot_nt(pn, vn_ref[head].astype(BF16)))
    for hh in range(HB):
        ms = [probs[hh, g][0] for g in range(len(DIL_PAIRS))]
        mx = jnp.maximum(jnp.maximum(ms[0], ms[1]), ms[2])
        num = None
        den = None
        for g in range(len(DIL_PAIRS)):
            w = jnp.exp2(ms[g] - mx)
            num = w * outs[hh, g] if num is None else num + w * outs[hh, g]
            den = w * probs[hh, g][1] if den is None else den + w * probs[hh, g][1]
        o_ref[0, hh] = num / den


def _decode_attention(qh, knt, vnt, cache_kt, cache_vt):
    B, H, T, E = qh.shape
    per_tile = LANES // T
    P = cache_kt.shape[-1]
    HB = 8
    nhb = HEADS_PER_GROUP // HB
    specs = []
    for g, (win, dil) in enumerate(DIL_PAIRS):
        imap = functools.partial(lambda b, s, g, last: (b, g * nhb + s, 0, last),
                                 g=g, last=P // win - 1)
        specs += [pl.BlockSpec((1, HB, E, win), imap)] * 2
    full_q = pl.BlockSpec((1, H, T, E), lambda b, s: (b, 0, 0, 0))
    full_n = pl.BlockSpec((H, E, LANES), lambda b, s: (0, 0, b // per_tile))
    return pl.pallas_call(
        functools.partial(_dec_attn_kernel, T=T, HB=HB),
        grid=(B, nhb),
        in_specs=[full_q, full_n, full_n] + specs,
        out_specs=pl.BlockSpec((1, HB, T, E), lambda b, s: (b, s, 0, 0)),
        out_shape=jax.ShapeDtypeStruct((B, HEADS_PER_GROUP, T, E), F32),
        compiler_params=_cparams(("parallel", "arbitrary"), vmem_mb=56),
        name="decode_attention",
    )(qh, knt, vnt, cache_kt, cache_vt, cache_kt, cache_vt, cache_kt, cache_vt)


def _rope_tables(pos):
    half = HEAD_DIM // 2
    inv = ROPE_THETA ** (-jnp.arange(half, dtype=F32) * (2.0 / HEAD_DIM))
    ang = pos.astype(F32)[:, None] * inv[None, :]
    cos, sin = jnp.cos(ang), jnp.sin(ang)
    cos_t = jnp.concatenate([cos, cos, cos, cos], axis=1)
    sin_t = jnp.concatenate([-sin, sin, -sin, sin], axis=1)
    return cos_t, sin_t


def _prep_params(p):
    w = {}
    w_in = p['m_w_in']
    w['w_in_zx'] = w_in[:, :, :M_D_INNER + M_CONV_DIM].astype(BF16)
    w['w_in_dt'] = jnp.pad(w_in[:, :, M_D_INNER + M_CONV_DIM:],
                           ((0, 0), (0, 0), (0, LANES - M_HEADS))).astype(BF16)
    w['dt_bias'] = jnp.pad(p['m_dt_bias'], ((0, 0), (0, LANES - M_HEADS)))[:, None, :]
    w['a_log'] = jnp.pad(p['m_A_log'], ((0, 0), (0, LANES - M_HEADS)))[:, None, :]
    w['d_skip'] = jnp.repeat(p['m_D'], M_HEADDIM, axis=1)[:, None, :]
    w['m_norm'] = p['m_norm'][:, None, :]
    w['conv_w'] = p['m_conv_w']
    w['conv_b'] = p['m_conv_b'][:, None, :]
    w['w_out'] = p['m_w_out'].astype(BF16)
    w['w_q'] = (p['w_q'] * (HEAD_DIM ** -0.5 * LOG2E)).astype(BF16)
    w['w_kv'] = p['w_kv'].astype(BF16)
    w['w_kv_t'] = p['w_kv'].T.astype(BF16)
    w['w_o'] = p['w_o'].astype(BF16)
    w['ffn_up'] = p['ffn_w_up'].astype(BF16)
    w['ffn_down'] = p['ffn_w_down'].astype(BF16)
    w['ffn_cw'] = p['ffn_conv_w']
    w['ffn_cb'] = p['ffn_conv_b'][:, None, :]
    head = jnp.arange(ATT_OUT) // HEAD_DIM
    w['expand'] = (jnp.arange(LANES)[:, None] == head[None, :]).astype(BF16)
    return w


def _trunk(x, mods, kvmod, gate_b, pos, ssm0, conv0, ffn0, kv_past, p, w, *, tm, step):
    B, T, Dm = x.shape
    R = B * T
    h = x.reshape(R, Dm)
    cos_t, sin_t = _rope_tables(pos)
    cos_r = jnp.tile(cos_t, (B, 1))
    sin_r = jnp.tile(sin_t, (B, 1))
    ssm_out, conv_out, ffn_out = [], [], []
    k_new = v_new = new_t = kv_split = None
    for i in range(DEPTH):
        sh1, sc1, g1, sh2, sc2, g2 = mods[i]
        if i < N_A:
            tmi = min(tm, 512)
            inproj = functools.partial(_in_proj, h, p['norm_mix'][i][None], sh1, sc1, w['w_in_zx'][i],
                                       w['w_in_dt'][i], w['conv_w'][i], w['conv_b'][i],
                                       w['dt_bias'][i], w['a_log'][i], tm=tmi, seq_rows=T)
            if step:
                prev = jnp.pad(conv0[i], ((0, 0), (SUBLANES - (M_CONV - 1), 0), (0, 0)))
                zx, dtr, u = inproj(prev=prev.reshape(R, M_CONV_DIM))
                conv_out.append(u.reshape(B, T, M_CONV_DIM)[:, T - (M_CONV - 1):])
                decay = (w['dt_bias'][i], w['a_log'][i])
            else:
                zx, dtr, u, acum, acum_t = inproj()
                seq_tiles = T // tmi
                conv_out.append(u[seq_tiles - 1::seq_tiles, SUBLANES - (M_CONV - 1):])
                decay = (acum.reshape(B, T, LANES), acum_t)
            h3, h_t = _ssd_mixer(zx.reshape(B, T, -1), dtr.reshape(B, T, LANES), decay, ssm0[i],
                                 w['d_skip'][i], w['m_norm'][i], w['w_out'][i],
                                 h.reshape(B, T, Dm), gate_b[i])
            ssm_out.append(h_t)
            h = h3.reshape(R, Dm)
        else:
            jb = i - N_A
            if step:
                q = _norm_mod_matmul(h, p['norm_mix'][i][None], sh1, sc1, w['w_q'][jb],
                                     (cos_r, sin_r), tm=tm, tn=1024, out_dtype=F32,
                                     n_rope=ATT_WIDTH // 1024)
                qh = q.reshape(B, T, ATT_HEADS, HEAD_DIM).transpose(0, 2, 1, 3)
                o = _decode_attention(qh, *new_t, *kv_past)
                o = o.transpose(0, 2, 1, 3).reshape(R, ATT_OUT).astype(BF16)
                h = _proj_residual(o, w['w_o'][jb], h, g1, tm=tm)
            else:
                q_split = _norm_mod_matmul_split(h, p['norm_mix'][i][None], sh1, sc1, w['w_q'][jb],
                                                 cos_t, sin_t, B=B, tm=SPLIT_TM, n_rope=len(DIL_PAIRS))
                o_parts, lse_parts = [], []
                for g in range(len(DIL_PAIRS)):
                    og, lg = _band_attention(q_split[g], *kv_split[g])
                    o_parts.append(og)
                    lse_parts.append(lg)
                h = _attn_out(o_parts, lse_parts, w['expand'], w['w_o'][jb], h, g1, B=B, tm=SPLIT_TM)
        last = i == DEPTH - 1
        fin = p['final_norm'][None] if last else None
        if step:
            prev = jnp.pad(ffn0[i], ((0, 0), (SUBLANES - (FFN_CONV - 1), 0), (0, 0))).reshape(R, 2 * D_FF)
            h, u = _conv_ffn(h, p['norm_ffn'][i][None], sh2, sc2, g2, w['ffn_up'][i], w['ffn_cw'][i],
                             w['ffn_cb'][i], w['ffn_down'][i], tm=tm, seq_rows=T, prev=prev, final_g=fin)
            ffn_out.append(u.reshape(B, T, 2 * D_FF)[:, T - (FFN_CONV - 1):])
        else:
            tmf = min(tm, 512)
            h, u = _conv_ffn(h, p['norm_ffn'][i][None], sh2, sc2, g2, w['ffn_up'][i], w['ffn_cw'][i],
                             w['ffn_cb'][i], w['ffn_down'][i], tm=tmf, seq_rows=T, final_g=fin)
            seq_tiles = T // tmf
            ffn_out.append(u[seq_tiles - 1::seq_tiles, SUBLANES - (FFN_CONV - 1):])
        if i == N_A - 1:
            ksh, ksc = kvmod
            keep = min(WINDOW_MAX, T)
            if step:
                h_tail, nb_k, tm_k = h, 1, tm
                rope_k = (jnp.tile(cos_t[:, :HEAD_DIM].T, (1, B)), jnp.tile(sin_t[:, :HEAD_DIM].T, (1, B)))
            else:
                ng = len(DIL_PAIRS)
                kvs = _norm_mod_matmul_split(h, p['kv_norm'][None], ksh, ksc, w['w_kv'], cos_t, sin_t,
                                             B=B, tm=SPLIT_TM, n_rope=ng)
                kv_split = [(kvs[g], kvs[ng + g]) for g in range(ng)]
                h_tail = h.reshape(B, T, Dm)[:, T - keep:].reshape(B * keep, Dm)
                nb_k, tm_k = B, min(tm, keep)
                rope_k = (cos_t[T - keep:, :HEAD_DIM].T, sin_t[T - keep:, :HEAD_DIM].T)
            kvt = _norm_mod_matmul_t(h_tail, p['kv_norm'][None], ksh, ksc, w['w_kv_t'], rope_k,
                                     nb=nb_k, tm=tm_k, tn=1024, n_rope=ATT_WIDTH // 1024)
            if step:
                kt, vt = kvt[0, :ATT_WIDTH], kvt[0, ATT_WIDTH:]
                new_t = [a.reshape(ATT_HEADS, HEAD_DIM, R) for a in (kt, vt)]
                k_new, v_new = [a.reshape(ATT_HEADS, HEAD_DIM, B, T).transpose(2, 3, 0, 1)
                                for a in (kt, vt)]
            else:
                k_new, v_new = [a.reshape(B, ATT_HEADS, HEAD_DIM, keep).transpose(0, 3, 1, 2)
                                for a in (kvt[:, :ATT_WIDTH], kvt[:, ATT_WIDTH:])]
    return (h.reshape(B, T, Dm), jnp.stack(ssm_out), jnp.stack(conv_out), jnp.stack(ffn_out),
            k_new, v_new)


def kernel(x_prompt, x_sample, state_ssm, state_conv, state_ffn_conv, cache_k, cache_v, c_prompt, c_sample, ada_w, ada_b, norm_mix, norm_ffn, m_w_in, m_conv_w, m_conv_b, m_dt_bias, m_A_log, m_D, m_norm, m_w_out, kv_norm, kv_ada_w, kv_ada_b, w_kv, w_q, w_o, ffn_w_up, ffn_conv_w, ffn_conv_b, ffn_w_down, final_norm):
    p = dict(norm_mix=norm_mix, norm_ffn=norm_ffn, m_w_in=m_w_in, m_conv_w=m_conv_w,
             m_conv_b=m_conv_b, m_dt_bias=m_dt_bias, m_A_log=m_A_log, m_D=m_D, m_norm=m_norm,
             m_w_out=m_w_out, kv_norm=kv_norm, w_kv=w_kv, w_q=w_q, w_o=w_o, ffn_w_up=ffn_w_up,
             ffn_conv_w=ffn_conv_w, ffn_conv_b=ffn_conv_b, ffn_w_down=ffn_w_down,
             final_norm=final_norm)
    w = _prep_params(p)
    Bp, S, Dm = x_prompt.shape
    Bs, T, _ = x_sample.shape

    nrow = Bp + Bs
    npad = -(-nrow // SUBLANES) * SUBLANES
    c_all = jnp.pad(jnp.concatenate([c_prompt, c_sample], axis=0), ((0, npad - nrow), (0, 0)))
    mod = _ada_linear(c_all, ada_w, ada_b[:, None, :])
    kvm = _ada_linear(c_all, kv_ada_w[None], kv_ada_b[None, None, :])[0]

    def mod_prompt(m):
        return [a[:, None, :] for a in jnp.split(m, m.shape[-1] // Dm, axis=-1)]

    def mod_sample(m):
        return [jnp.repeat(a, T, axis=0)[None] for a in jnp.split(m, m.shape[-1] // Dm, axis=-1)]

    mods_p = [mod_prompt(mod[i, :Bp]) for i in range(DEPTH)]
    mods_s = [mod_sample(mod[i, Bp:nrow]) for i in range(DEPTH)]
    kvm_p = mod_prompt(kvm[:Bp])
    kvm_s = mod_sample(kvm[Bp:nrow])
    gate_p = [mods_p[i][2] for i in range(N_A)]
    gate_s = [mod_prompt(mod[i, Bp:nrow])[2] for i in range(N_A)]

    ssm0 = jnp.zeros((N_A, Bp, M_HEADS, M_HEADDIM, M_D_STATE), state_ssm.dtype)
    y_p, ssm_p, conv_p, ffn_p, k_p, v_p = _trunk(
        x_prompt, mods_p, kvm_p, gate_p, jnp.arange(S, dtype=jnp.int32), ssm0, None, None, None, p, w,
        tm=1024, step=False)
    cache_t = (cache_k.transpose(0, 2, 3, 1), cache_v.transpose(0, 2, 3, 1))
    y_s, ssm_s, conv_s, ffn_s, k_s, v_s = _trunk(
        x_sample, mods_s, kvm_s, gate_s, PAST_LEN + jnp.arange(T, dtype=jnp.int32), state_ssm,
        state_conv, state_ffn_conv, cache_t, p, w, tm=Bs * T, step=True)
    return (y_p, y_s, ssm_p, ssm_s, conv_p, conv_s, ffn_p, ffn_s, k_p, k_s, v_p, v_s)
```

```python
import functools

import jax
import jax.numpy as jnp
from jax import lax
from jax.experimental import pallas as pl
from jax.experimental.pallas import tpu as pltpu

F32 = jnp.float32
BF16 = jnp.bfloat16

D_MODEL = 1024
DEPTH = 4
N_A = 2
M_D_INNER = 2048
M_HEADDIM = 64
M_HEADS = 32
M_NGROUPS = 4
M_D_STATE = 128
M_CONV = 4
M_GN = 512
M_CONV_DIM = 3072
HEAD_DIM = 64
HEADS_PER_GROUP = 16
DIL_PAIRS = ((128, 1), (512, 4), (2048, 16))
ATT_HEADS = 48
ATT_WIDTH = 3072
ATT_OUT = 1024
WINDOW_MAX = 2048
ATT_BLOCK = 128
ROPE_THETA = 10000.0
D_FF = 2816
FFN_CONV = 3
EPS = 1e-6
SSD_CHUNK = 128
SSD_SHORT_CHUNK = 32
PAST_LEN = 8192

LANES = 128
SUBLANES = 8
BF16_ROWS = 16
LOG2E = 1.4426950408889634
FFN_TC = 256
INPROJ_TC = 256
BAND_BLOCKS = 2
SPLIT_TM = 512
NEG_INF = float("-inf")


def _cparams(sem, vmem_mb=48):
    return pltpu.CompilerParams(dimension_semantics=sem,
                                vmem_limit_bytes=vmem_mb * 1024 * 1024)


def _silu(x):
    return x * jax.nn.sigmoid(x)


def _dot(a, b):
    return jnp.dot(a, b, preferred_element_type=F32)


def _dot_nt(a, b):
    return lax.dot_general(a, b, (((1,), (1,)), ((), ())), preferred_element_type=F32)


def _ada_kernel(c_ref, w_ref, b_ref, o_ref):
    cs = _silu(c_ref[...]).astype(BF16)
    o_ref[0] = _dot(cs, w_ref[0].astype(BF16)) + b_ref[0]


def _ada_linear(c, w, b, tn=1024):
    L, K, N = w.shape
    M = c.shape[0]
    return pl.pallas_call(
        _ada_kernel,
        grid=(L, N // tn),
        in_specs=[pl.BlockSpec((M, K), lambda l, j: (0, 0)),
                  pl.BlockSpec((1, K, tn), lambda l, j: (l, 0, j)),
                  pl.BlockSpec((1, 1, tn), lambda l, j: (l, 0, j))],
        out_specs=pl.BlockSpec((1, M, tn), lambda l, j: (l, 0, j)),
        out_shape=jax.ShapeDtypeStruct((L, M, N), F32),
        compiler_params=_cparams(("parallel", "parallel")),
        name="ada_linear",
    )(c, w, b)


def _rope_tile(acc, cos, sin):
    pieces = []
    first_half = (lax.broadcasted_iota(jnp.int32, (1, LANES), 1) & (HEAD_DIM - 1)) < (HEAD_DIM // 2)
    for c in range(acc.shape[1] // LANES):
        xc = acc[:, c * LANES:(c + 1) * LANES]
        partner = jnp.where(first_half,
                            pltpu.roll(xc, LANES - HEAD_DIM // 2, 1),
                            pltpu.roll(xc, HEAD_DIM // 2, 1))
        pieces.append(xc * cos + partner * sin)
    return jnp.concatenate(pieces, axis=1)


def _nmm_kernel(h_ref, g_ref, sh_ref, sc_ref, w_ref, cos_ref, sin_ref, o_ref, xn_ref, *, n_rope):
    j = pl.program_id(1)

    @pl.when(j == 0)
    def _():
        x = h_ref[...]
        ms = jnp.mean(x * x, axis=-1, keepdims=True)
        y = x * lax.rsqrt(ms + EPS) * g_ref[...]
        xn_ref[...] = (y * (1.0 + sc_ref[0]) + sh_ref[0]).astype(BF16)

    acc = _dot(xn_ref[...], w_ref[...])

    @pl.when(j < n_rope)
    def _():
        o_ref[...] = _rope_tile(acc, cos_ref[...], sin_ref[...]).astype(o_ref.dtype)

    @pl.when(j >= n_rope)
    def _():
        o_ref[...] = acc.astype(o_ref.dtype)


def _mod_spec(mk, rows, tm, arity=1):
    arr, k = mk
    nmod, rm, _ = arr.shape
    tiles_per_mod = rows // nmod // tm
    if arity == 1:
        return pl.BlockSpec((1, rm, D_MODEL), lambda i: (i // tiles_per_mod, 0, k))
    return pl.BlockSpec((1, rm, D_MODEL), lambda i, j: (i // tiles_per_mod, 0, k))


def _norm_mod_matmul(h, g, shift, scale, w, rope, *, tm, tn, out_dtype, n_rope):
    R, Dm = h.shape
    N = w.shape[1]
    tab_spec = pl.BlockSpec((tm, LANES), lambda i, j: (i, 0))
    return pl.pallas_call(
        functools.partial(_nmm_kernel, n_rope=n_rope),
        grid=(R // tm, N // tn),
        in_specs=[pl.BlockSpec((tm, Dm), lambda i, j: (i, 0)),
                  pl.BlockSpec((1, Dm), lambda i, j: (0, 0)),
                  _mod_spec(shift, R, tm, 2), _mod_spec(scale, R, tm, 2),
                  pl.BlockSpec((Dm, tn), lambda i, j: (0, j)),
                  tab_spec, tab_spec],
        out_specs=pl.BlockSpec((tm, tn), lambda i, j: (i, j)),
        out_shape=jax.ShapeDtypeStruct((R, N), out_dtype),
        scratch_shapes=[pltpu.VMEM((tm, Dm), BF16)],
        compiler_params=_cparams(("parallel", "arbitrary")),
        name="norm_mod_matmul",
    )(h, g, shift[0], scale[0], w, *rope)


def _nmm_t_kernel(h_ref, g_ref, sh_ref, sc_ref, wt_ref, cos_ref, sin_ref, o_ref, xn_ref, *, n_rope):
    j = pl.program_id(1)

    @pl.when(j == 0)
    def _():
        x = h_ref[...]
        ms = jnp.mean(x * x, axis=-1, keepdims=True)
        y = x * lax.rsqrt(ms + EPS) * g_ref[...]
        xn_ref[...] = (y * (1.0 + sc_ref[0]) + sh_ref[0]).astype(BF16)

    acc = _dot_nt(wt_ref[...], xn_ref[...])

    @pl.when(j < n_rope)
    def _():
        cos, sin = cos_ref[...], sin_ref[...]
        half = HEAD_DIM // 2
        pieces = []
        for hb in range(acc.shape[0] // HEAD_DIM):
            x = acc[hb * HEAD_DIM:(hb + 1) * HEAD_DIM]
            partner = jnp.concatenate([x[half:], x[:half]], axis=0)
            pieces.append(x * cos + partner * sin)
        o_ref[0] = jnp.concatenate(pieces, axis=0)

    @pl.when(j >= n_rope)
    def _():
        o_ref[0] = acc


def _norm_mod_matmul_t(h, g, shift, scale, wt, rope_t, *, nb, tm, tn, n_rope):
    R, Dm = h.shape
    N = wt.shape[0]
    cols = R // nb
    tiles_per_b = cols // tm
    tab_spec = pl.BlockSpec((HEAD_DIM, tm), lambda i, j: (0, i % tiles_per_b))
    return pl.pallas_call(
        functools.partial(_nmm_t_kernel, n_rope=n_rope),
        grid=(R // tm, N // tn),
        in_specs=[pl.BlockSpec((tm, Dm), lambda i, j: (i, 0)),
                  pl.BlockSpec((1, Dm), lambda i, j: (0, 0)),
                  _mod_spec(shift, R, tm, 2), _mod_spec(scale, R, tm, 2),
                  pl.BlockSpec((tn, Dm), lambda i, j: (j, 0)),
                  tab_spec, tab_spec],
        out_specs=pl.BlockSpec((1, tn, tm), lambda i, j: (i // tiles_per_b, j, i % tiles_per_b)),
        out_shape=jax.ShapeDtypeStruct((nb, N, cols), F32),
        scratch_shapes=[pltpu.VMEM((tm, Dm), BF16)],
        compiler_params=_cparams(("parallel", "arbitrary")),
        name="norm_mod_matmul_t",
    )(h, g, shift[0], scale[0], wt, *rope_t)


def _nmm_split_kernel(*refs, n_rope, n_out, tm):
    h_ref, g_ref, sh_ref, sc_ref, w_ref, cos_ref, sin_ref = refs[:7]
    outs = refs[7:7 + n_out]
    xf_ref, xn_ref = refs[7 + n_out:]
    ng = len(DIL_PAIRS)
    W = ATT_OUT

    x = h_ref[...]
    ms = jnp.mean(x * x, axis=-1, keepdims=True)
    y = x * lax.rsqrt(ms + EPS) * g_ref[...]
    xn = y * (1.0 + sc_ref[0]) + sh_ref[0]
    nlc = xn.shape[1] // LANES
    for lc in range(nlc):
        xf_ref[lc] = xn[:, lc * LANES:(lc + 1) * LANES]
    for g, (win, d) in enumerate(DIL_PAIRS):
        rows = tm // d
        if d == 1:
            xn_ref[g] = xn.astype(BF16)
        else:
            for r in range(d):
                for lc in range(nlc):
                    xn_ref[g, r * rows:(r + 1) * rows, lc * LANES:(lc + 1) * LANES] = (
                        xf_ref.at[lc][pl.ds(r, rows, stride=d), :].astype(BF16))

    for k in range(n_out):
        g = k % ng
        acc = _dot(xn_ref[g], w_ref[:, k * W:(k + 1) * W])
        val = _rope_tile(acc, cos_ref[g], sin_ref[g]) if k < n_rope else acc
        d = DIL_PAIRS[g][1]
        rows = tm // d
        for r in range(d):
            outs[k][0, r] = val[r * rows:(r + 1) * rows].astype(outs[k].dtype)


def _split_rows(x, d, tm):
    T, C = x.shape
    return x.reshape(T // tm, tm // d, d, C).transpose(0, 2, 1, 3).reshape(T, C)


def _norm_mod_matmul_split(h, g, shift, scale, w, cos_t, sin_t, *, B, tm, n_rope):
    R, Dm = h.shape
    S = R // B
    W = ATT_OUT
    n_out = w.shape[1] // W
    seq_tiles = S // tm
    ng = len(DIL_PAIRS)
    cos_g = jnp.stack([_split_rows(cos_t, d, tm) for _, d in DIL_PAIRS])
    sin_g = jnp.stack([_split_rows(sin_t, d, tm) for _, d in DIL_PAIRS])
    tab_spec = pl.BlockSpec((ng, tm, LANES), lambda i: (0, i % seq_tiles, 0))
    out_specs, out_shape = [], []
    for k in range(n_out):
        d = DIL_PAIRS[k % ng][1]
        out_specs.append(pl.BlockSpec((1, d, tm // d, W),
                                      lambda i: (i // seq_tiles, 0, i % seq_tiles, 0)))
        out_shape.append(jax.ShapeDtypeStruct((B, d, S // d, W), BF16))
    return pl.pallas_call(
        functools.partial(_nmm_split_kernel, n_rope=n_rope, n_out=n_out, tm=tm),
        grid=(R // tm,),
        in_specs=[pl.BlockSpec((tm, Dm), lambda i: (i, 0)),
                  pl.BlockSpec((1, Dm), lambda i: (0, 0)),
                  _mod_spec(shift, R, tm), _mod_spec(scale, R, tm),
                  pl.BlockSpec((Dm, n_out * W), lambda i: (0, 0), pipeline_mode=pl.Buffered(1)),
                  tab_spec, tab_spec],
        out_specs=out_specs,
        out_shape=out_shape,
        scratch_shapes=[pltpu.VMEM((Dm // LANES, tm, LANES), F32), pltpu.VMEM((ng, tm, Dm), BF16)],
        compiler_params=_cparams(("parallel",), vmem_mb=56),
        name="norm_mod_matmul_split",
    )(h, g, shift[0], scale[0], w, cos_g, sin_g)


def _decay_terms(dt_raw, dtb, alog, valid_rows=None):
    Q = dt_raw.shape[0]
    dt = dt_raw + dtb
    dt = jnp.maximum(dt, 0.0) + jnp.log(1.0 + jnp.exp(-jnp.abs(dt)))
    if valid_rows is not None:
        dt = jnp.where(lax.broadcasted_iota(jnp.int32, (Q, 1), 0) < valid_rows, dt, 0.0)
    a = dt * (-jnp.exp(alog) * LOG2E)
    ri = lax.broadcasted_iota(jnp.int32, (Q, Q), 0)
    ci = lax.broadcasted_iota(jnp.int32, (Q, Q), 1)
    tril = jnp.where(ri >= ci, 1.0, 0.0).astype(F32)
    acum = jnp.dot(tril, a, preferred_element_type=F32, precision=lax.Precision.HIGHEST)
    return dt, acum


def _inproj_kernel(*refs, tm, seq_tiles, step_mode):
    h_ref, g_ref, sh_ref, sc_ref, w_ref, wdt_ref, cw_ref, cb_ref = refs[:8]
    k = 8
    if step_mode:
        pp_ref = refs[k]
        k += 1
    else:
        dtb_ref, alog_ref = refs[k:k + 2]
        k += 2
    zx_ref, dt_ref, u_ref = refs[k:k + 3]
    k += 3
    if not step_mode:
        ac_ref, act_ref, carry_ref, fix_ref = refs[k:k + 4]
    i = pl.program_id(0)
    KC = M_CONV - 1
    tc = INPROJ_TC

    x = h_ref[...]
    ms = jnp.mean(x * x, axis=-1, keepdims=True)
    xn = (x * lax.rsqrt(ms + EPS) * g_ref[...] * (1.0 + sc_ref[0]) + sh_ref[0]).astype(BF16)
    dt_raw = _dot(xn, wdt_ref[...])
    if step_mode:
        dt_ref[...] = dt_raw
    else:
        for ck in range(tm // SSD_CHUNK):
            rows = slice(ck * SSD_CHUNK, (ck + 1) * SSD_CHUNK)
            dt, acum = _decay_terms(dt_raw[rows], dtb_ref[...], alog_ref[...])
            dt_ref[rows, :] = dt
            ac_ref[rows, :] = acum
            act_ref[ck] = acum.T
    if step_mode:
        t = lax.broadcasted_iota(jnp.int32, (tm, 1), 0) & (SUBLANES - 1)
    else:
        @pl.when(i % seq_tiles == 0)
        def _():
            carry_ref[...] = jnp.zeros_like(carry_ref)

    nz, nx = M_D_INNER // tc, M_CONV_DIM // tc

    def xbc_dot(c):
        return _dot(xn, w_ref[:, M_D_INNER + c * tc:M_D_INNER + (c + 1) * tc])

    u_next = xbc_dot(0)
    for c in range(nx):
        cols = slice(c * tc, (c + 1) * tc)
        u = u_next
        if c + 1 < nx:
            u_next = xbc_dot(c + 1)
        for zc in range(c * nz // nx, (c + 1) * nz // nx):
            zx_ref[:, zc * tc:(zc + 1) * tc] = _dot(xn, w_ref[:, zc * tc:(zc + 1) * tc])
        w = [cw_ref[kk:kk + 1, cols] for kk in range(M_CONV)]
        b = cb_ref[:, cols]
        if step_mode:
            u_ref[:, cols] = u
            pp = pp_ref[:, cols]
            acc = u * w[KC] + b
            for s in range(1, M_CONV):
                term = jnp.where(t >= s, pltpu.roll(u, s, 0), pltpu.roll(pp, tm + s - SUBLANES, 0))
                acc = acc + term * w[KC - s]
        else:
            fix_ref[0:SUBLANES, :] = carry_ref[:, cols]
            fix_ref[SUBLANES:2 * SUBLANES, :] = u[0:SUBLANES]
            last = u[tm - SUBLANES:tm]
            carry_ref[:, cols] = last
            u_ref[0, :, cols] = last
            acc = u * w[KC] + b
            fix = fix_ref[SUBLANES:2 * SUBLANES, :] * w[KC] + b
            for s in range(1, M_CONV):
                acc = acc + pltpu.roll(u, s, 0) * w[KC - s]
                fix = fix + fix_ref[pl.ds(SUBLANES - s, SUBLANES), :] * w[KC - s]
            acc = jnp.concatenate([fix, acc[SUBLANES:]], axis=0)
        zx_ref[:, M_D_INNER + c * tc:M_D_INNER + (c + 1) * tc] = _silu(acc)


def _in_proj(h, g, shift, scale, w_zx, w_dt, cw, cb, dtb, alog, *, tm, seq_rows, prev=None):
    R, Dm = h.shape
    N = w_zx.shape[1]
    step_mode = prev is not None
    seq_tiles = 1 if step_mode else seq_rows // tm
    ntiles = R // tm
    once = pl.Buffered(1)
    const = lambda i: (0, 0)
    in_specs = [pl.BlockSpec((tm, Dm), lambda i: (i, 0)), pl.BlockSpec((1, Dm), const),
                _mod_spec(shift, R, tm), _mod_spec(scale, R, tm),
                pl.BlockSpec((Dm, N), const, pipeline_mode=once),
                pl.BlockSpec((Dm, LANES), const, pipeline_mode=once),
                pl.BlockSpec((M_CONV, M_CONV_DIM), const, pipeline_mode=once),
                pl.BlockSpec((1, M_CONV_DIM), const, pipeline_mode=once)]
    args = [h, g, shift[0], scale[0], w_zx, w_dt, cw, cb]
    scratch = []
    lane_spec = pl.BlockSpec((tm, LANES), lambda i: (i, 0))
    out_specs = [pl.BlockSpec((tm, N), lambda i: (i, 0)), lane_spec]
    out_shape = [jax.ShapeDtypeStruct((R, N), F32), jax.ShapeDtypeStruct((R, LANES), F32)]
    if step_mode:
        in_specs.append(pl.BlockSpec((tm, M_CONV_DIM), lambda i: (i, 0)))
        args.append(prev)
        out_specs.append(pl.BlockSpec((tm, M_CONV_DIM), lambda i: (i, 0)))
        out_shape.append(jax.ShapeDtypeStruct((R, M_CONV_DIM), F32))
    else:
        in_specs += [pl.BlockSpec((1, LANES), const)] * 2
        args += [dtb, alog]
        cpt = tm // SSD_CHUNK
        out_specs += [pl.BlockSpec((1, SUBLANES, M_CONV_DIM), lambda i: (i, 0, 0)), lane_spec,
                      pl.BlockSpec((cpt, SSD_CHUNK, LANES), lambda i: (i, 0, 0))]
        out_shape += [jax.ShapeDtypeStruct((ntiles, SUBLANES, M_CONV_DIM), F32),
                      jax.ShapeDtypeStruct((R, LANES), F32),
                      jax.ShapeDtypeStruct((R // SSD_CHUNK, SSD_CHUNK, LANES), F32)]
        scratch = [pltpu.VMEM((SUBLANES, M_CONV_DIM), F32),
                   pltpu.VMEM((2 * SUBLANES, INPROJ_TC), F32)]
    return pl.pallas_call(
        functools.partial(_inproj_kernel, tm=tm, seq_tiles=seq_tiles, step_mode=step_mode),
        grid=(ntiles,),
        in_specs=in_specs,
        out_specs=out_specs,
        out_shape=out_shape,
        scratch_shapes=scratch,
        compiler_params=_cparams(("arbitrary",), vmem_mb=56),
        name="in_proj",
    )(*args)


def _ssd_kernel(*refs, Q, Tv):
    pre = Tv == Q
    zx_ref, dt_ref = refs[:2]
    k = 2
    if pre:
        ac_ref, act_ref = refs[k:k + 2]
    else:
        dtb_ref, alog_ref = refs[k:k + 2]
    k += 2
    h0_ref, dskip_ref, nw_ref, wout_ref, h_ref, gate_ref, out_ref, hout_ref, hT_ref, ybuf_ref = refs[k:k + 10]
    k += 10
    c = pl.program_id(1)
    npair = M_HEADS // M_NGROUPS // 2

    @pl.when(c == 0)
    def _():
        for g in range(M_NGROUPS):
            for p in range(npair):
                h0 = g * 2 * npair + 2 * p
                pair = h0_ref[0, h0:h0 + 2].reshape(2 * M_HEADDIM, M_D_STATE)
                hT_ref[g, :, p * LANES:(p + 1) * LANES] = pair.T

    if pre:
        zfull = None
        xact = zx_ref[0, :, M_D_INNER:]
        dt, acum, acum_t = dt_ref[0], ac_ref[0], act_ref[0]
    else:
        pad_ref, dtp_ref = refs[k:k + 2]
        pad_ref[...] = jnp.zeros_like(pad_ref)
        pad_ref[0:Tv, :] = zx_ref[0]
        dtp_ref[...] = jnp.zeros_like(dtp_ref)
        dtp_ref[0:Tv, :] = dt_ref[0]
        zfull = pad_ref[:, 0:M_D_INNER]
        xact = pad_ref[:, M_D_INNER:]
        dt, acum = _decay_terms(dtp_ref[...], dtb_ref[...], alog_ref[...], Tv)
        acum_t = acum.T
    causal = (lax.broadcasted_iota(jnp.int32, (Q, Q), 0)
              >= lax.broadcasted_iota(jnp.int32, (Q, Q), 1))
    lane_lo = lax.broadcasted_iota(jnp.int32, (1, LANES), 1) < M_HEADDIM

    groups = range(M_NGROUPS)
    keeps = (lane_lo, jnp.logical_not(lane_lo))
    c_gs, b_gts, cbs = [], [], []
    for g in groups:
        b_f = xact[:, M_D_INNER + g * M_D_STATE:M_D_INNER + (g + 1) * M_D_STATE]
        c_g = xact[:, M_D_INNER + M_GN + g * M_D_STATE:M_D_INNER + M_GN + (g + 1) * M_D_STATE].astype(BF16)
        c_gs.append(c_g)
        b_gts.append(b_f.T.astype(BF16))
        cbs.append(_dot_nt(c_g, b_f.astype(BF16)))
    xdts, es, xdecs = [], [], []
    for pi in range(M_HEADS // 2):
        h0 = 2 * pi
        xp = xact[:, h0 * M_HEADDIM:(h0 + 2) * M_HEADDIM]
        dtp = jnp.where(lane_lo, dt[:, h0:h0 + 1], dt[:, h0 + 1:h0 + 2])
        ap = jnp.where(lane_lo, acum[:, h0:h0 + 1], acum[:, h0 + 1:h0 + 2])
        xdt = xp * dtp
        xdts.append(xdt)
        es.append(jnp.exp2(ap))
        xdecs.append((xdt * jnp.exp2(ap[Q - 1:Q, :] - ap)).astype(BF16))
    mats = []
    for hh in range(M_HEADS):
        seg = acum[:, hh:hh + 1] - acum_t[hh:hh + 1, :]
        lm = jnp.exp2(jnp.where(causal, seg, NEG_INF))
        mats.append((cbs[hh // (2 * npair)] * lm).astype(BF16))
    ydiag = []
    for pi in range(M_HEADS // 2):
        x0 = jnp.where(keeps[0], xdts[pi], 0.0).astype(BF16)
        x1 = jnp.where(keeps[1], xdts[pi], 0.0).astype(BF16)
        if Q % LANES == 0:
            lhs = jnp.concatenate([mats[2 * pi], mats[2 * pi + 1]], axis=1)
            ydiag.append(_dot(lhs, jnp.concatenate([x0, x1], axis=0)))
        else:
            ydiag.append(_dot(mats[2 * pi], x0) + _dot(mats[2 * pi + 1], x1))
    y_offs = []
    for g in groups:
        e_g = jnp.concatenate(es[g * npair:(g + 1) * npair], axis=1)
        xdec_g = jnp.concatenate(xdecs[g * npair:(g + 1) * npair], axis=1)
        h_prev = hT_ref[g]
        y_offs.append(_dot(c_gs[g], h_prev.astype(BF16)) * e_g)
        hT_ref[g] = h_prev * e_g[Q - 1:Q, :] + _dot(b_gts[g], xdec_g)
    for g in groups:
        gs = slice(g * M_GN, (g + 1) * M_GN)
        y = (jnp.concatenate(ydiag[g * npair:(g + 1) * npair], axis=1) + y_offs[g]
             + xact[:, gs] * dskip_ref[:, gs])
        yg = y * _silu(zx_ref[0, :, gs] if zfull is None else zfull[:, gs])
        ms = jnp.mean(yg * yg, axis=-1, keepdims=True)
        ybuf_ref[:, gs] = (yg * lax.rsqrt(ms + EPS) * nw_ref[:, gs]).astype(BF16)

    tb = -(-Tv // BF16_ROWS) * BF16_ROWS
    out_ref[0] = h_ref[0] + gate_ref[0] * _dot(ybuf_ref[0:tb, :], wout_ref[...])[0:Tv]

    @pl.when(c == pl.num_programs(1) - 1)
    def _():
        for g in range(M_NGROUPS):
            for p in range(npair):
                h0 = g * 2 * npair + 2 * p
                pair = hT_ref[g, :, p * LANES:(p + 1) * LANES].T
                hout_ref[0, h0:h0 + 2] = pair.reshape(2, M_HEADDIM, M_D_STATE)


def _ssd_mixer(zx, dtr, decay, h0, dskip, nw, w_out, h, gate):
    B, L, W = zx.shape
    Dm = h.shape[-1]
    Q = SSD_CHUNK if L >= SSD_CHUNK else SSD_SHORT_CHUNK
    Tv = min(L, Q)
    nc = L // Tv
    const2 = lambda b, c: (0, 0)
    row3 = lambda b, c: (b, c, 0)
    state_spec = pl.BlockSpec((1, M_HEADS, M_HEADDIM, M_D_STATE), lambda b, c: (b, 0, 0, 0))
    scratch = [pltpu.VMEM((M_NGROUPS, M_D_STATE, M_GN), F32), pltpu.VMEM((Q, M_D_INNER), BF16)]
    if Tv == Q:
        decay_specs = [pl.BlockSpec((1, Q, LANES), row3),
                       pl.BlockSpec((1, Q, LANES), lambda b, c: (b * nc + c, 0, 0))]
    else:
        decay_specs = [pl.BlockSpec((1, LANES), const2)] * 2
        scratch += [pltpu.VMEM((Q, W), F32), pltpu.VMEM((Q, LANES), F32)]
    return pl.pallas_call(
        functools.partial(_ssd_kernel, Q=Q, Tv=Tv),
        grid=(B, nc),
        in_specs=[pl.BlockSpec((1, Tv, W), row3),
                  pl.BlockSpec((1, Tv, LANES), row3)]
        + decay_specs
        + [state_spec,
           pl.BlockSpec((1, M_D_INNER), const2),
           pl.BlockSpec((1, M_D_INNER), const2),
           pl.BlockSpec((M_D_INNER, Dm), const2, pipeline_mode=pl.Buffered(1)),
           pl.BlockSpec((1, Tv, Dm), row3),
           _mod_spec(gate, B, 1, 2)],
        out_specs=[pl.BlockSpec((1, Tv, Dm), row3), state_spec],
        out_shape=[jax.ShapeDtypeStruct((B, L, Dm), F32),
                   jax.ShapeDtypeStruct((B, M_HEADS, M_HEADDIM, M_D_STATE), F32)],
        scratch_shapes=scratch,
        compiler_params=_cparams(("parallel", "arbitrary")),
        name="ssd_mixer",
    )(zx, dtr, *decay, h0, dskip, nw, w_out, h, gate[0])


def _proj_res_kernel(a_ref, w_ref, h_ref, gate_ref, o_ref):
    o_ref[...] = h_ref[...] + gate_ref[0] * _dot(a_ref[...], w_ref[...])


def _proj_residual(a, w, h, gate, *, tm):
    R, K = a.shape
    Dm = w.shape[1]
    return pl.pallas_call(
        _proj_res_kernel,
        grid=(R // tm,),
        in_specs=[pl.BlockSpec((tm, K), lambda i: (i, 0)),
                  pl.BlockSpec((K, Dm), lambda i: (0, 0)),
                  pl.BlockSpec((tm, Dm), lambda i: (i, 0)),
                  _mod_spec(gate, R, tm)],
        out_specs=pl.BlockSpec((tm, Dm), lambda i: (i, 0)),
        out_shape=jax.ShapeDtypeStruct((R, Dm), F32),
        compiler_params=_cparams(("parallel",)),
        name="proj_residual",
    )(a, w, h, gate[0])


def _attn_out_kernel(o0_ref, o1_ref, o2_ref, l0_ref, l1_ref, l2_ref, e_ref, w_ref, h_ref,
                     gate_ref, out_ref, nat_o, nat_l, *, tm):
    nlc = ATT_OUT // LANES
    assert DIL_PAIRS[0][1] == 1
    for g, (o_ref, l_ref) in enumerate(((o0_ref, l0_ref), (o1_ref, l1_ref), (o2_ref, l2_ref))):
        d = DIL_PAIRS[g][1]
        rows = tm // d
        if d == 1:
            continue
        for r in range(d):
            sl = pl.ds(r, rows, stride=d)
            nat_l.at[g][sl, :] = l_ref[0, r]
            for lc in range(nlc):
                nat_o.at[g * nlc + lc][sl, :] = o_ref[0, r, lc]
    l0, l1, l2 = l0_ref[0, 0], nat_l[1], nat_l[2]
    mx = jnp.maximum(jnp.maximum(l0, l1), l2)
    w0, w1, w2 = jnp.exp2(l0 - mx), jnp.exp2(l1 - mx), jnp.exp2(l2 - mx)
    inv = 1.0 / (w0 + w1 + w2)
    alphas = []
    for wg in (w0, w1, w2):
        a = wg * inv
        hi = a.astype(BF16)
        lo = (a - hi.astype(F32)).astype(BF16)
        alphas.append(_dot(hi, e_ref[...]) + _dot(lo, e_ref[...]))
    pieces = []
    for lc in range(nlc):
        ls = slice(lc * LANES, (lc + 1) * LANES)
        t = alphas[0][:, ls] * o0_ref[0, 0, lc]
        for g in range(1, len(DIL_PAIRS)):
            t = t + alphas[g][:, ls] * nat_o[g * nlc + lc]
        pieces.append(t.astype(BF16))
    comb = jnp.concatenate(pieces, axis=1)
    out_ref[...] = h_ref[...] + gate_ref[0] * _dot(comb, w_ref[...])


def _attn_out(o_parts, lse_parts, expand, w, h, gate, *, B, tm):
    R, Dm = h.shape
    seq_tiles = R // B // tm
    nlc = ATT_OUT // LANES
    ng = len(DIL_PAIRS)
    row = lambda i: (i, 0)
    o_specs = [pl.BlockSpec((1, d, nlc, tm // d, LANES),
                            lambda i: (i // seq_tiles, 0, 0, i % seq_tiles, 0)) for _, d in DIL_PAIRS]
    l_specs = [pl.BlockSpec((1, d, tm // d, LANES),
                            lambda i: (i // seq_tiles, 0, i % seq_tiles, 0)) for _, d in DIL_PAIRS]
    return pl.pallas_call(
        functools.partial(_attn_out_kernel, tm=tm),
        grid=(R // tm,),
        in_specs=o_specs + l_specs
        + [pl.BlockSpec((LANES, ATT_OUT), lambda i: (0, 0)),
           pl.BlockSpec((ATT_OUT, Dm), lambda i: (0, 0)),
           pl.BlockSpec((tm, Dm), row),
           _mod_spec(gate, R, tm)],
        out_specs=pl.BlockSpec((tm, Dm), row),
        out_shape=jax.ShapeDtypeStruct((R, Dm), F32),
        scratch_shapes=[pltpu.VMEM((ng * nlc, tm, LANES), F32), pltpu.VMEM((ng, tm, LANES), F32)],
        compiler_params=_cparams(("parallel",)),
        name="attn_out",
    )(*o_parts, *lse_parts, expand, w, h, gate[0])


def _ffn_kernel(*refs, tm, seq_tiles, step_mode, has_final):
    h_ref, g_ref, sh_ref, sc_ref, gate_ref, wup_ref, cw_ref, cb_ref, wd_ref = refs[:9]
    k = 9
    if step_mode:
        pp_ref = refs[k]
        k += 1
    if has_final:
        fg_ref = refs[k]
        k += 1
    out_ref, u_ref, act_ref = refs[k:k + 3]
    k += 3
    if not step_mode:
        carry_ref, fix_ref = refs[k:k + 2]
    i = pl.program_id(0)
    KC = FFN_CONV - 1
    tc = FFN_TC

    x = h_ref[...]
    ms = jnp.mean(x * x, axis=-1, keepdims=True)
    xn = (x * lax.rsqrt(ms + EPS) * g_ref[...] * (1.0 + sc_ref[0]) + sh_ref[0]).astype(BF16)

    if step_mode:
        t = lax.broadcasted_iota(jnp.int32, (tm, 1), 0) & (SUBLANES - 1)
    else:
        @pl.when(i % seq_tiles == 0)
        def _():
            carry_ref[...] = jnp.zeros_like(carry_ref)

    def conv(cols, part):
        u = _dot(xn, wup_ref[:, cols])
        w = [cw_ref[kk:kk + 1, cols] for kk in range(FFN_CONV)]
        b = cb_ref[:, cols]
        if step_mode:
            u_ref[:, cols] = u
            pp = pp_ref[:, cols]
            acc = u * w[KC]
            for s in range(1, FFN_CONV):
                term = jnp.where(t >= s, pltpu.roll(u, s, 0), pltpu.roll(pp, tm + s - SUBLANES, 0))
                acc = acc + term * w[KC - s]
            return acc + b
        fix_ref[part, 0:SUBLANES, :] = carry_ref[:, cols]
        fix_ref[part, SUBLANES:2 * SUBLANES, :] = u[0:SUBLANES]
        last = u[tm - SUBLANES:tm]
        carry_ref[:, cols] = last
        u_ref[0, :, cols] = last
        acc = u * w[KC] + b
        fix = fix_ref[part, SUBLANES:2 * SUBLANES, :] * w[KC] + b
        for s in range(1, FFN_CONV):
            acc = acc + pltpu.roll(u, s, 0) * w[KC - s]
            fix = fix + fix_ref[part, pl.ds(SUBLANES - s, SUBLANES), :] * w[KC - s]
        return jnp.concatenate([fix, acc[SUBLANES:]], axis=0)

    for c in range(D_FF // tc):
        cg = conv(slice(c * tc, (c + 1) * tc), 0)
        cv = conv(slice(D_FF + c * tc, D_FF + (c + 1) * tc), 1)
        act_ref[:, c * tc:(c + 1) * tc] = (_silu(cg) * cv).astype(BF16)

    hn = x + gate_ref[0] * _dot(act_ref[...], wd_ref[...])
    if has_final:
        ms = jnp.mean(hn * hn, axis=-1, keepdims=True)
        hn = hn * lax.rsqrt(ms + EPS) * fg_ref[...]
    out_ref[...] = hn


def _conv_ffn(h, g, shift, scale, gate, w_up, cw, cb, w_down, *, tm, seq_rows,
              prev=None, final_g=None):
    R, Dm = h.shape
    step_mode = prev is not None
    seq_tiles = 1 if step_mode else seq_rows // tm
    ntiles = R // tm
    once = pl.Buffered(1)
    const = lambda i: (0, 0)
    row_spec = pl.BlockSpec((tm, Dm), lambda i: (i, 0))
    in_specs = [row_spec, pl.BlockSpec((1, Dm), const),
                _mod_spec(shift, R, tm), _mod_spec(scale, R, tm), _mod_spec(gate, R, tm),
                pl.BlockSpec((Dm, 2 * D_FF), const, pipeline_mode=once),
                pl.BlockSpec((FFN_CONV, 2 * D_FF), const, pipeline_mode=once),
                pl.BlockSpec((1, 2 * D_FF), const, pipeline_mode=once),
                pl.BlockSpec((D_FF, Dm), const, pipeline_mode=once)]
    args = [h, g, shift[0], scale[0], gate[0], w_up, cw, cb, w_down]
    scratch = [pltpu.VMEM((tm, D_FF), BF16)]
    if step_mode:
        in_specs.append(pl.BlockSpec((tm, 2 * D_FF), lambda i: (i, 0)))
        args.append(prev)
        u_spec = pl.BlockSpec((tm, 2 * D_FF), lambda i: (i, 0))
        u_shape = jax.ShapeDtypeStruct((R, 2 * D_FF), F32)
    else:
        u_spec = pl.BlockSpec((1, SUBLANES, 2 * D_FF), lambda i: (i, 0, 0))
        u_shape = jax.ShapeDtypeStruct((ntiles, SUBLANES, 2 * D_FF), F32)
        scratch += [pltpu.VMEM((SUBLANES, 2 * D_FF), F32),
                    pltpu.VMEM((2, 2 * SUBLANES, FFN_TC), F32)]
    if final_g is not None:
        in_specs.append(pl.BlockSpec((1, Dm), const))
        args.append(final_g)
    return pl.pallas_call(
        functools.partial(_ffn_kernel, tm=tm, seq_tiles=seq_tiles,
                          step_mode=step_mode, has_final=final_g is not None),
        grid=(ntiles,),
        in_specs=in_specs,
        out_specs=[row_spec, u_spec],
        out_shape=[jax.ShapeDtypeStruct((R, Dm), F32), u_shape],
        scratch_shapes=scratch,
        compiler_params=_cparams(("arbitrary",), vmem_mb=56),
        name="conv_ffn",
    )(*args)


def _band_attn_kernel(q_ref, kc_ref, kp_ref, vc_ref, vp_ref, bias_ref, o_ref, lse_ref, bias0_ref, *, nbk):
    c = pl.program_id(2)
    BL = ATT_BLOCK
    lane = lax.broadcasted_iota(jnp.int32, (1, LANES), 1)
    bias0_ref[...] = jnp.where(c > 0, bias_ref[0], NEG_INF)
    lane_lo = lane < HEAD_DIM
    keeps = (lane_lo, jnp.logical_not(lane_lo))
    zero = jnp.zeros((BL, LANES), BF16)
    units = [(sb, h) for sb in range(nbk) for h in range(HEADS_PER_GROUP)]

    def kv_prev(cur_ref, prev_ref, sb, ps):
        if sb == 0:
            return prev_ref[0, 0, :, ps]
        return cur_ref[0, 0, (sb - 1) * BL:sb * BL, ps]

    scores = {}
    for sb, h in units:
        ps = slice((h // 2) * LANES, (h // 2 + 1) * LANES)
        rows = slice(sb * BL, (sb + 1) * BL)
        qh = jnp.where(keeps[h % 2], q_ref[0, 0, rows, ps], zero)
        s_p = _dot_nt(qh, kv_prev(kc_ref, kp_ref, sb, ps)) + (bias0_ref[...] if sb == 0 else bias_ref[0])
        s_c = _dot_nt(qh, kc_ref[0, 0, rows, ps]) + bias_ref[1]
        scores[sb, h] = (s_p, s_c)
    probs = {}
    for sb in range(nbk):
        lse = jnp.zeros((BL, LANES), F32)
        for h in range(HEADS_PER_GROUP):
            s_p, s_c = scores[sb, h]
            m = jnp.max(jnp.maximum(s_p, s_c), axis=-1, keepdims=True)
            p_p = jnp.exp2(s_p - m)
            p_c = jnp.exp2(s_c - m)
            l = jnp.sum(p_p + p_c, axis=-1, keepdims=True)
            lse = jnp.where(lane == h, m + jnp.log(l) * LOG2E, lse)
            probs[sb, h] = (p_p.astype(BF16), p_c.astype(BF16), 1.0 / l)
        lse_ref[0, 0, sb * BL:(sb + 1) * BL, :] = lse
    for sb in range(nbk):
        rows = slice(sb * BL, (sb + 1) * BL)
        for p in range(HEADS_PER_GROUP // 2):
            ps = slice(p * LANES, (p + 1) * LANES)
            vc, vp = vc_ref[0, 0, rows, ps], kv_prev(vc_ref, vp_ref, sb, ps)
            o_pair = None
            for hh in range(2):
                p_p, p_c, _ = probs[sb, 2 * p + hh]
                o = _dot(p_p, jnp.where(keeps[hh], vp, zero)) + _dot(p_c, jnp.where(keeps[hh], vc, zero))
                o_pair = o if o_pair is None else o_pair + o
            scale = jnp.where(lane_lo, probs[sb, 2 * p][2], probs[sb, 2 * p + 1][2])
            o_ref[0, 0, p, rows, :] = o_pair * scale


def _band_attention(q, k, v):
    B, d, n, W = q.shape
    nbk = BAND_BLOCKS
    blk = nbk * ATT_BLOCK
    cur = pl.BlockSpec((1, 1, blk, W), lambda b, r, c: (b, r, c, 0))
    prv = pl.BlockSpec((1, 1, ATT_BLOCK, W), lambda b, r, c: (b, r, jnp.maximum(c * nbk - 1, 0), 0))
    i = jnp.arange(ATT_BLOCK)[:, None]
    j = jnp.arange(ATT_BLOCK)[None, :]
    bias = jnp.where(jnp.stack([j >= i, j <= i]), 0.0, NEG_INF).astype(F32)
    return pl.pallas_call(
        functools.partial(_band_attn_kernel, nbk=nbk),
        grid=(B, d, n // blk),
        in_specs=[cur, cur, prv, cur, prv,
                  pl.BlockSpec((2, ATT_BLOCK, ATT_BLOCK), lambda b, r, c: (0, 0, 0))],
        scratch_shapes=[pltpu.VMEM((ATT_BLOCK, ATT_BLOCK), F32)],
        out_specs=[pl.BlockSpec((1, 1, W // LANES, blk, LANES), lambda b, r, c: (b, r, 0, c, 0)),
                   pl.BlockSpec((1, 1, blk, LANES), lambda b, r, c: (b, r, c, 0))],
        out_shape=[jax.ShapeDtypeStruct((B, d, W // LANES, n, LANES), F32),
                   jax.ShapeDtypeStruct((B, d, n, LANES), F32)],
        compiler_params=_cparams(("parallel", "parallel", "arbitrary")),
        name="band_attention",
    )(q, k, k, v, v, bias)


def _dec_attn_kernel(q_ref, kn_ref, vn_ref, k0_ref, v0_ref, k1_ref, v1_ref, k2_ref, v2_ref,
                     o_ref, *, T, HB):
    s = pl.program_id(1)
    caches = ((k0_ref, v0_ref), (k1_ref, v1_ref), (k2_ref, v2_ref))
    t = lax.broadcasted_iota(jnp.int32, (T, 1), 0)
    off = (pl.program_id(0) % (LANES // T)) * T
    u = lax.broadcasted_iota(jnp.int32, (1, LANES), 1) - off
    masks = []
    for g, (win, dil) in enumerate(DIL_PAIRS):
        p = lax.broadcasted_iota(jnp.int32, (1, win), 1)
        dist = win + t - p
        cmask = (dist <= win) & ((dist & (dil - 1)) == 0)
        nmask = (u >= 0) & (u <= t) & (((t - u) & (dil - 1)) == 0)
        masks.append((cmask, nmask))
    units = [(hh, g) for hh in range(HB) for g in range(len(DIL_PAIRS))]
    scores = {}
    for hh, g in units:
        head = g * HEADS_PER_GROUP + s * HB + hh
        cmask, nmask = masks[g]
        q = q_ref[0, head].astype(BF16)
        sc = jnp.where(cmask, _dot(q, caches[g][0][0, hh].astype(BF16)), NEG_INF)
        sn = jnp.where(nmask, _dot(q, kn_ref[head].astype(BF16)), NEG_INF)
        scores[hh, g] = (sc, sn)
    probs = {}
    for hh, g in units:
        sc, sn = scores[hh, g]
        m = jnp.maximum(jnp.max(sc, axis=-1, keepdims=True), jnp.max(sn, axis=-1, keepdims=True))
        pc = jnp.exp2(sc - m)
        pn = jnp.exp2(sn - m)
        l = jnp.sum(pc, axis=-1, keepdims=True) + jnp.sum(pn, axis=-1, keepdims=True)
        probs[hh, g] = (m, l, pc.astype(BF16), pn.astype(BF16))
    outs = {}
    for hh, g in units:
        head = g * HEADS_PER_GROUP + s * HB + hh
        m, l, pc, pn = probs[hh, g]
        outs[hh, g] = (_dot_nt(pc, caches[g][1][0, hh].astype(BF16))
                       + _dot_nt(pn, vn_ref[head].astype(BF16)))
    for hh in range(HB):
        ms = [probs[hh, g][0] for g in range(len(DIL_PAIRS))]
        mx = jnp.maximum(jnp.maximum(ms[0], ms[1]), ms[2])
        num = None
        den = None
        for g in range(len(DIL_PAIRS)):
            w = jnp.exp2(ms[g] - mx)
            num = w * outs[hh, g] if num is None else num + w * outs[hh, g]
            den = w * probs[hh, g][1] if den is None else den + w * probs[hh, g][1]
        o_ref[0, hh] = num / den


def _decode_attention(qh, knt, vnt, cache_kt, cache_vt):
    B, H, T, E = qh.shape
    per_tile = LANES // T
    P = cache_kt.shape[-1]
    HB = 8
    nhb = HEADS_PER_GROUP // HB
    specs = []
    for g, (win, dil) in enumerate(DIL_PAIRS):
        imap = functools.partial(lambda b, s, g, last: (b, g * nhb + s, 0, last),
                                 g=g, last=P // win - 1)
        specs += [pl.BlockSpec((1, HB, E, win), imap)] * 2
    full_q = pl.BlockSpec((1, H, T, E), lambda b, s: (b, 0, 0, 0))
    full_n = pl.BlockSpec((H, E, LANES), lambda b, s: (0, 0, b // per_tile))
    return pl.pallas_call(
        functools.partial(_dec_attn_kernel, T=T, HB=HB),
        grid=(B, nhb),
        in_specs=[full_q, full_n, full_n] + specs,
        out_specs=pl.BlockSpec((1, HB, T, E), lambda b, s: (b, s, 0, 0)),
        out_shape=jax.ShapeDtypeStruct((B, HEADS_PER_GROUP, T, E), F32),
        compiler_params=_cparams(("parallel", "arbitrary"), vmem_mb=56),
        name="decode_attention",
    )(qh, knt, vnt, cache_kt, cache_vt, cache_kt, cache_vt, cache_kt, cache_vt)


def _rope_tables(pos):
    half = HEAD_DIM // 2
    inv = ROPE_THETA ** (-jnp.arange(half, dtype=F32) * (2.0 / HEAD_DIM))
    ang = pos.astype(F32)[:, None] * inv[None, :]
    cos, sin = jnp.cos(ang), jnp.sin(ang)
    cos_t = jnp.concatenate([cos, cos, cos, cos], axis=1)
    sin_t = jnp.concatenate([-sin, sin, -sin, sin], axis=1)
    return cos_t, sin_t


def _prep_params(p):
    w = {}
    w_in = p['m_w_in']
    w['w_in_zx'] = w_in[:, :, :M_D_INNER + M_CONV_DIM].astype(BF16)
    w['w_in_dt'] = jnp.pad(w_in[:, :, M_D_INNER + M_CONV_DIM:],
                           ((0, 0), (0, 0), (0, LANES - M_HEADS))).astype(BF16)
    w['dt_bias'] = jnp.pad(p['m_dt_bias'], ((0, 0), (0, LANES - M_HEADS)))[:, None, :]
    w['a_log'] = jnp.pad(p['m_A_log'], ((0, 0), (0, LANES - M_HEADS)))[:, None, :]
    w['d_skip'] = jnp.repeat(p['m_D'], M_HEADDIM, axis=1)[:, None, :]
    w['m_norm'] = p['m_norm'][:, None, :]
    w['conv_w'] = p['m_conv_w']
    w['conv_b'] = p['m_conv_b'][:, None, :]
    w['w_out'] = p['m_w_out'].astype(BF16)
    w['w_q'] = (p['w_q'] * (HEAD_DIM ** -0.5 * LOG2E)).astype(BF16)
    w['w_kv'] = p['w_kv'].astype(BF16)
    w['w_kv_t'] = p['w_kv'].T.astype(BF16)
    w['w_o'] = p['w_o'].astype(BF16)
    w['ffn_up'] = p['ffn_w_up'].astype(BF16)
    w['ffn_down'] = p['ffn_w_down'].astype(BF16)
    w['ffn_cw'] = p['ffn_conv_w']
    w['ffn_cb'] = p['ffn_conv_b'][:, None, :]
    head = jnp.arange(ATT_OUT) // HEAD_DIM
    w['expand'] = (jnp.arange(LANES)[:, None] == head[None, :]).astype(BF16)
    return w


def _trunk(x, mods, kvmod, gate_b, pos, ssm0, conv0, ffn0, kv_past, p, w, *, tm, step):
    B, T, Dm = x.shape
    R = B * T
    h = x.reshape(R, Dm)
    cos_t, sin_t = _rope_tables(pos)
    cos_r = jnp.tile(cos_t, (B, 1))
    sin_r = jnp.tile(sin_t, (B, 1))
    ssm_out, conv_out, ffn_out = [], [], []
    k_new = v_new = new_t = kv_split = None
    for i in range(DEPTH):
        sh1, sc1, g1, sh2, sc2, g2 = [(mods[i], k) for k in range(6)]
        if i < N_A:
            tmi = min(tm, 512)
            inproj = functools.partial(_in_proj, h, p['norm_mix'][i][None], sh1, sc1, w['w_in_zx'][i],
                                       w['w_in_dt'][i], w['conv_w'][i], w['conv_b'][i],
                                       w['dt_bias'][i], w['a_log'][i], tm=tmi, seq_rows=T)
            if step:
                prev = jnp.pad(conv0[i], ((0, 0), (SUBLANES - (M_CONV - 1), 0), (0, 0)))
                zx, dtr, u = inproj(prev=prev.reshape(R, M_CONV_DIM))
                conv_out.append(u.reshape(B, T, M_CONV_DIM)[:, T - (M_CONV - 1):])
                decay = (w['dt_bias'][i], w['a_log'][i])
            else:
                zx, dtr, u, acum, acum_t = inproj()
                seq_tiles = T // tmi
                conv_out.append(u[seq_tiles - 1::seq_tiles, SUBLANES - (M_CONV - 1):])
                decay = (acum.reshape(B, T, LANES), acum_t)
            h3, h_t = _ssd_mixer(zx.reshape(B, T, -1), dtr.reshape(B, T, LANES), decay, ssm0[i],
                                 w['d_skip'][i], w['m_norm'][i], w['w_out'][i],
                                 h.reshape(B, T, Dm), gate_b[i])
            ssm_out.append(h_t)
            h = h3.reshape(R, Dm)
        else:
            jb = i - N_A
            if step:
                q = _norm_mod_matmul(h, p['norm_mix'][i][None], sh1, sc1, w['w_q'][jb],
                                     (cos_r, sin_r), tm=tm, tn=1024, out_dtype=F32,
                                     n_rope=ATT_WIDTH // 1024)
                qh = q.reshape(B, T, ATT_HEADS, HEAD_DIM).transpose(0, 2, 1, 3)
                o = _decode_attention(qh, *new_t, *kv_past)
                o = o.transpose(0, 2, 1, 3).reshape(R, ATT_OUT).astype(BF16)
                h = _proj_residual(o, w['w_o'][jb], h, g1, tm=tm)
            else:
                q_split = _norm_mod_matmul_split(h, p['norm_mix'][i][None], sh1, sc1, w['w_q'][jb],
                                                 cos_t, sin_t, B=B, tm=SPLIT_TM, n_rope=len(DIL_PAIRS))
                o_parts, lse_parts = [], []
                for g in range(len(DIL_PAIRS)):
                    og, lg = _band_attention(q_split[g], *kv_split[g])
                    o_parts.append(og)
                    lse_parts.append(lg)
                h = _attn_out(o_parts, lse_parts, w['expand'], w['w_o'][jb], h, g1, B=B, tm=SPLIT_TM)
        last = i == DEPTH - 1
        fin = p['final_norm'][None] if last else None
        if step:
            prev = jnp.pad(ffn0[i], ((0, 0), (SUBLANES - (FFN_CONV - 1), 0), (0, 0))).reshape(R, 2 * D_FF)
            h, u = _conv_ffn(h, p['norm_ffn'][i][None], sh2, sc2, g2, w['ffn_up'][i], w['ffn_cw'][i],
                             w['ffn_cb'][i], w['ffn_down'][i], tm=tm, seq_rows=T, prev=prev, final_g=fin)
            ffn_out.append(u.reshape(B, T, 2 * D_FF)[:, T - (FFN_CONV - 1):])
        else:
            tmf = min(tm, 512)
            h, u = _conv_ffn(h, p['norm_ffn'][i][None], sh2, sc2, g2, w['ffn_up'][i], w['ffn_cw'][i],
                             w['ffn_cb'][i], w['ffn_down'][i], tm=tmf, seq_rows=T, final_g=fin)
            seq_tiles = T // tmf
            ffn_out.append(u[seq_tiles - 1::seq_tiles, SUBLANES - (FFN_CONV - 1):])
        if i == N_A - 1:
            ksh, ksc = (kvmod, 0), (kvmod, 1)
            keep = min(WINDOW_MAX, T)
            if step:
                h_tail, nb_k, tm_k = h, 1, tm
                rope_k = (jnp.tile(cos_t[:, :HEAD_DIM].T, (1, B)), jnp.tile(sin_t[:, :HEAD_DIM].T, (1, B)))
            else:
                ng = len(DIL_PAIRS)
                kvs = _norm_mod_matmul_split(h, p['kv_norm'][None], ksh, ksc, w['w_kv'], cos_t, sin_t,
                                             B=B, tm=SPLIT_TM, n_rope=ng)
                kv_split = [(kvs[g], kvs[ng + g]) for g in range(ng)]
                h_tail = h.reshape(B, T, Dm)[:, T - keep:].reshape(B * keep, Dm)
                nb_k, tm_k = B, min(tm, keep)
                rope_k = (cos_t[T - keep:, :HEAD_DIM].T, sin_t[T - keep:, :HEAD_DIM].T)
            kvt = _norm_mod_matmul_t(h_tail, p['kv_norm'][None], ksh, ksc, w['w_kv_t'], rope_k,
                                     nb=nb_k, tm=tm_k, tn=1024, n_rope=ATT_WIDTH // 1024)
            if step:
                kt, vt = kvt[0, :ATT_WIDTH], kvt[0, ATT_WIDTH:]
                new_t = [a.reshape(ATT_HEADS, HEAD_DIM, R) for a in (kt, vt)]
                k_new, v_new = [a.reshape(ATT_HEADS, HEAD_DIM, B, T).transpose(2, 3, 0, 1)
                                for a in (kt, vt)]
            else:
                k_new, v_new = [a.reshape(B, ATT_HEADS, HEAD_DIM, keep).transpose(0, 3, 1, 2)
                                for a in (kvt[:, :ATT_WIDTH], kvt[:, ATT_WIDTH:])]
    return (h.reshape(B, T, Dm), jnp.stack(ssm_out), jnp.stack(conv_out), jnp.stack(ffn_out),
            k_new, v_new)


def kernel(x_prompt, x_sample, state_ssm, state_conv, state_ffn_conv, cache_k, cache_v, c_prompt, c_sample, ada_w, ada_b, norm_mix, norm_ffn, m_w_in, m_conv_w, m_conv_b, m_dt_bias, m_A_log, m_D, m_norm, m_w_out, kv_norm, kv_ada_w, kv_ada_b, w_kv, w_q, w_o, ffn_w_up, ffn_conv_w, ffn_conv_b, ffn_w_down, final_norm):
    p = dict(norm_mix=norm_mix, norm_ffn=norm_ffn, m_w_in=m_w_in, m_conv_w=m_conv_w,
             m_conv_b=m_conv_b, m_dt_bias=m_dt_bias, m_A_log=m_A_log, m_D=m_D, m_norm=m_norm,
             m_w_out=m_w_out, kv_norm=kv_norm, w_kv=w_kv, w_q=w_q, w_o=w_o, ffn_w_up=ffn_w_up,
             ffn_conv_w=ffn_conv_w, ffn_conv_b=ffn_conv_b, ffn_w_down=ffn_w_down,
             final_norm=final_norm)
    w = _prep_params(p)
    Bp, S, Dm = x_prompt.shape
    Bs, T, _ = x_sample.shape

    nrow = Bp + Bs
    npad = -(-nrow // SUBLANES) * SUBLANES
    c_all = jnp.pad(jnp.concatenate([c_prompt, c_sample], axis=0), ((0, npad - nrow), (0, 0)))
    mod = _ada_linear(c_all, ada_w, ada_b[:, None, :])
    kvm = _ada_linear(c_all, kv_ada_w[None], kv_ada_b[None, None, :])[0]

    def per_seq(m):
        return m[:, None, :]

    def per_token(m):
        return jnp.repeat(m, T, axis=0)[None]

    mods_p = [per_seq(mod[i, :Bp]) for i in range(DEPTH)]
    mods_s = [per_token(mod[i, Bp:nrow]) for i in range(DEPTH)]
    kvm_p = per_seq(kvm[:Bp])
    kvm_s = per_token(kvm[Bp:nrow])
    gate_p = [(mods_p[i], 2) for i in range(N_A)]
    gate_s = [(per_seq(mod[i, Bp:nrow]), 2) for i in range(N_A)]

    ssm0 = jnp.zeros((N_A, Bp, M_HEADS, M_HEADDIM, M_D_STATE), state_ssm.dtype)
    y_p, ssm_p, conv_p, ffn_p, k_p, v_p = _trunk(
        x_prompt, mods_p, kvm_p, gate_p, jnp.arange(S, dtype=jnp.int32), ssm0, None, None, None, p, w,
        tm=1024, step=False)
    cache_t = (cache_k.transpose(0, 2, 3, 1), cache_v.transpose(0, 2, 3, 1))
    y_s, ssm_s, conv_s, ffn_s, k_s, v_s = _trunk(
        x_sample, mods_s, kvm_s, gate_s, PAST_LEN + jnp.arange(T, dtype=jnp.int32), state_ssm,
        state_conv, state_ffn_conv, cache_t, p, w, tm=Bs * T, step=True)
    return (y_p, y_s, ssm_p, ssm_s, conv_p, conv_s, ffn_p, ffn_s, k_p, k_s, v_p, v_s)
```

```python
import functools

import jax
import jax.numpy as jnp
from jax import lax
from jax.experimental import pallas as pl
from jax.experimental.pallas import tpu as pltpu

F32 = jnp.float32
BF16 = jnp.bfloat16

D_MODEL = 1024
DEPTH = 4
N_A = 2
M_D_INNER = 2048
M_HEADDIM = 64
M_HEADS = 32
M_NGROUPS = 4
M_D_STATE = 128
M_CONV = 4
M_GN = 512
M_CONV_DIM = 3072
HEAD_DIM = 64
HEADS_PER_GROUP = 16
DIL_PAIRS = ((128, 1), (512, 4), (2048, 16))
ATT_HEADS = 48
ATT_WIDTH = 3072
ATT_OUT = 1024
WINDOW_MAX = 2048
ATT_BLOCK = 128
ROPE_THETA = 10000.0
D_FF = 2816
FFN_CONV = 3
EPS = 1e-6
SSD_CHUNK = 128
SSD_SHORT_CHUNK = 32
PAST_LEN = 8192

LANES = 128
SUBLANES = 8
BF16_ROWS = 16
LOG2E = 1.4426950408889634
FFN_TC = 256
INPROJ_TC = 256
BAND_BLOCKS = 2
SPLIT_TM = 512
NEG_INF = float("-inf")


def _cparams(sem, vmem_mb=48):
    return pltpu.CompilerParams(dimension_semantics=sem,
                                vmem_limit_bytes=vmem_mb * 1024 * 1024)


def _silu(x):
    return x * jax.nn.sigmoid(x)


def _dot(a, b):
    return jnp.dot(a, b, preferred_element_type=F32)


def _dot_nt(a, b):
    return lax.dot_general(a, b, (((1,), (1,)), ((), ())), preferred_element_type=F32)


def _ada_kernel(c_ref, w_ref, b_ref, o_ref):
    cs = _silu(c_ref[...]).astype(BF16)
    o_ref[0] = _dot(cs, w_ref[0].astype(BF16)) + b_ref[0]


def _ada_linear(c, w, b, tn=1024):
    L, K, N = w.shape
    M = c.shape[0]
    return pl.pallas_call(
        _ada_kernel,
        grid=(L, N // tn),
        in_specs=[pl.BlockSpec((M, K), lambda l, j: (0, 0)),
                  pl.BlockSpec((1, K, tn), lambda l, j: (l, 0, j)),
                  pl.BlockSpec((1, 1, tn), lambda l, j: (l, 0, j))],
        out_specs=pl.BlockSpec((1, M, tn), lambda l, j: (l, 0, j)),
        out_shape=jax.ShapeDtypeStruct((L, M, N), F32),
        compiler_params=_cparams(("parallel", "parallel")),
        name="ada_linear",
    )(c, w, b)


def _rope_tile(acc, cos, sin):
    pieces = []
    first_half = (lax.broadcasted_iota(jnp.int32, (1, LANES), 1) & (HEAD_DIM - 1)) < (HEAD_DIM // 2)
    for c in range(acc.shape[1] // LANES):
        xc = acc[:, c * LANES:(c + 1) * LANES]
        partner = jnp.where(first_half,
                            pltpu.roll(xc, LANES - HEAD_DIM // 2, 1),
                            pltpu.roll(xc, HEAD_DIM // 2, 1))
        pieces.append(xc * cos + partner * sin)
    return jnp.concatenate(pieces, axis=1)


def _nmm_kernel(h_ref, g_ref, sh_ref, sc_ref, w_ref, cos_ref, sin_ref, o_ref, xn_ref, *, n_rope):
    j = pl.program_id(1)

    @pl.when(j == 0)
    def _():
        x = h_ref[...]
        ms = jnp.mean(x * x, axis=-1, keepdims=True)
        y = x * lax.rsqrt(ms + EPS) * g_ref[...]
        xn_ref[...] = (y * (1.0 + sc_ref[0]) + sh_ref[0]).astype(BF16)

    acc = _dot(xn_ref[...], w_ref[...])

    @pl.when(j < n_rope)
    def _():
        o_ref[...] = _rope_tile(acc, cos_ref[...], sin_ref[...]).astype(o_ref.dtype)

    @pl.when(j >= n_rope)
    def _():
        o_ref[...] = acc.astype(o_ref.dtype)


def _mod_spec(mk, rows, tm, arity=1):
    arr, k = mk
    nmod, rm, _ = arr.shape
    tiles_per_mod = rows // nmod // tm
    if arity == 1:
        return pl.BlockSpec((1, rm, D_MODEL), lambda i: (i // tiles_per_mod, 0, k))
    return pl.BlockSpec((1, rm, D_MODEL), lambda i, j: (i // tiles_per_mod, 0, k))


def _norm_mod_matmul(h, g, shift, scale, w, rope, *, tm, tn, out_dtype, n_rope):
    R, Dm = h.shape
    N = w.shape[1]
    tab_spec = pl.BlockSpec((tm, LANES), lambda i, j: (i, 0))
    return pl.pallas_call(
        functools.partial(_nmm_kernel, n_rope=n_rope),
        grid=(R // tm, N // tn),
        in_specs=[pl.BlockSpec((tm, Dm), lambda i, j: (i, 0)),
                  pl.BlockSpec((1, Dm), lambda i, j: (0, 0)),
                  _mod_spec(shift, R, tm, 2), _mod_spec(scale, R, tm, 2),
                  pl.BlockSpec((Dm, tn), lambda i, j: (0, j)),
                  tab_spec, tab_spec],
        out_specs=pl.BlockSpec((tm, tn), lambda i, j: (i, j)),
        out_shape=jax.ShapeDtypeStruct((R, N), out_dtype),
        scratch_shapes=[pltpu.VMEM((tm, Dm), BF16)],
        compiler_params=_cparams(("parallel", "arbitrary")),
        name="norm_mod_matmul",
    )(h, g, shift[0], scale[0], w, *rope)


def _nmm_t_kernel(h_ref, g_ref, sh_ref, sc_ref, wt_ref, cos_ref, sin_ref, o_ref, xn_ref, *, n_rope):
    j = pl.program_id(1)

    @pl.when(j == 0)
    def _():
        x = h_ref[...]
        ms = jnp.mean(x * x, axis=-1, keepdims=True)
        y = x * lax.rsqrt(ms + EPS) * g_ref[...]
        xn_ref[...] = (y * (1.0 + sc_ref[0]) + sh_ref[0]).astype(BF16)

    acc = _dot_nt(wt_ref[...], xn_ref[...])

    @pl.when(j < n_rope)
    def _():
        cos, sin = cos_ref[...], sin_ref[...]
        half = HEAD_DIM // 2
        pieces = []
        for hb in range(acc.shape[0] // HEAD_DIM):
            x = acc[hb * HEAD_DIM:(hb + 1) * HEAD_DIM]
            partner = jnp.concatenate([x[half:], x[:half]], axis=0)
            pieces.append(x * cos + partner * sin)
        o_ref[0, 0] = jnp.concatenate(pieces, axis=0)

    @pl.when(j >= n_rope)
    def _():
        o_ref[0, 0] = acc


def _norm_mod_matmul_t(h, g, shift, scale, wt, rope_t, *, nb, tm, tn, n_rope):
    R, Dm = h.shape
    N = wt.shape[0]
    cols = R // nb
    tiles_per_b = cols // tm
    nh = N // 2 // tn
    tab_spec = pl.BlockSpec((HEAD_DIM, tm), lambda i, j: (0, i % tiles_per_b))
    return pl.pallas_call(
        functools.partial(_nmm_t_kernel, n_rope=n_rope),
        grid=(R // tm, N // tn),
        in_specs=[pl.BlockSpec((tm, Dm), lambda i, j: (i, 0)),
                  pl.BlockSpec((1, Dm), lambda i, j: (0, 0)),
                  _mod_spec(shift, R, tm, 2), _mod_spec(scale, R, tm, 2),
                  pl.BlockSpec((tn, Dm), lambda i, j: (j, 0)),
                  tab_spec, tab_spec],
        out_specs=pl.BlockSpec((1, 1, tn, tm),
                               lambda i, j: (j // nh, i // tiles_per_b, j % nh, i % tiles_per_b)),
        out_shape=jax.ShapeDtypeStruct((2, nb, N // 2, cols), F32),
        scratch_shapes=[pltpu.VMEM((tm, Dm), BF16)],
        compiler_params=_cparams(("parallel", "arbitrary")),
        name="norm_mod_matmul_t",
    )(h, g, shift[0], scale[0], wt, *rope_t)


def _nmm_split_kernel(*refs, n_rope, n_out, tm):
    h_ref, g_ref, sh_ref, sc_ref, w_ref, cos_ref, sin_ref = refs[:7]
    outs = refs[7:7 + n_out]
    xf_ref, xn_ref = refs[7 + n_out:]
    ng = len(DIL_PAIRS)
    W = ATT_OUT

    x = h_ref[...]
    ms = jnp.mean(x * x, axis=-1, keepdims=True)
    y = x * lax.rsqrt(ms + EPS) * g_ref[...]
    xn = y * (1.0 + sc_ref[0]) + sh_ref[0]
    nlc = xn.shape[1] // LANES
    for lc in range(nlc):
        xf_ref[lc] = xn[:, lc * LANES:(lc + 1) * LANES]
    for g, (win, d) in enumerate(DIL_PAIRS):
        rows = tm // d
        if d == 1:
            xn_ref[g] = xn.astype(BF16)
        else:
            for r in range(d):
                for lc in range(nlc):
                    xn_ref[g, r * rows:(r + 1) * rows, lc * LANES:(lc + 1) * LANES] = (
                        xf_ref.at[lc][pl.ds(r, rows, stride=d), :].astype(BF16))

    for k in range(n_out):
        g = k % ng
        acc = _dot(xn_ref[g], w_ref[:, k * W:(k + 1) * W])
        val = _rope_tile(acc, cos_ref[g], sin_ref[g]) if k < n_rope else acc
        d = DIL_PAIRS[g][1]
        rows = tm // d
        for r in range(d):
            outs[k][0, r] = val[r * rows:(r + 1) * rows].astype(outs[k].dtype)


def _split_rows(x, d, tm):
    T, C = x.shape
    return x.reshape(T // tm, tm // d, d, C).transpose(0, 2, 1, 3).reshape(T, C)


def _norm_mod_matmul_split(h, g, shift, scale, w, cos_t, sin_t, *, B, tm, n_rope):
    R, Dm = h.shape
    S = R // B
    W = ATT_OUT
    n_out = w.shape[1] // W
    seq_tiles = S // tm
    ng = len(DIL_PAIRS)
    cos_g = jnp.stack([_split_rows(cos_t, d, tm) for _, d in DIL_PAIRS])
    sin_g = jnp.stack([_split_rows(sin_t, d, tm) for _, d in DIL_PAIRS])
    tab_spec = pl.BlockSpec((ng, tm, LANES), lambda i: (0, i % seq_tiles, 0))
    out_specs, out_shape = [], []
    for k in range(n_out):
        d = DIL_PAIRS[k % ng][1]
        out_specs.append(pl.BlockSpec((1, d, tm // d, W),
                                      lambda i: (i // seq_tiles, 0, i % seq_tiles, 0)))
        out_shape.append(jax.ShapeDtypeStruct((B, d, S // d, W), BF16))
    return pl.pallas_call(
        functools.partial(_nmm_split_kernel, n_rope=n_rope, n_out=n_out, tm=tm),
        grid=(R // tm,),
        in_specs=[pl.BlockSpec((tm, Dm), lambda i: (i, 0)),
                  pl.BlockSpec((1, Dm), lambda i: (0, 0)),
                  _mod_spec(shift, R, tm), _mod_spec(scale, R, tm),
                  pl.BlockSpec((Dm, n_out * W), lambda i: (0, 0), pipeline_mode=pl.Buffered(1)),
                  tab_spec, tab_spec],
        out_specs=out_specs,
        out_shape=out_shape,
        scratch_shapes=[pltpu.VMEM((Dm // LANES, tm, LANES), F32), pltpu.VMEM((ng, tm, Dm), BF16)],
        compiler_params=_cparams(("parallel",), vmem_mb=56),
        name="norm_mod_matmul_split",
    )(h, g, shift[0], scale[0], w, cos_g, sin_g)


def _decay_terms(dt_raw, dtb, alog, valid_rows=None):
    Q = dt_raw.shape[0]
    dt = dt_raw + dtb
    dt = jnp.maximum(dt, 0.0) + jnp.log(1.0 + jnp.exp(-jnp.abs(dt)))
    if valid_rows is not None:
        dt = jnp.where(lax.broadcasted_iota(jnp.int32, (Q, 1), 0) < valid_rows, dt, 0.0)
    a = dt * (-jnp.exp(alog) * LOG2E)
    ri = lax.broadcasted_iota(jnp.int32, (Q, Q), 0)
    ci = lax.broadcasted_iota(jnp.int32, (Q, Q), 1)
    tril = jnp.where(ri >= ci, 1.0, 0.0).astype(F32)
    acum = jnp.dot(tril, a, preferred_element_type=F32, precision=lax.Precision.HIGHEST)
    return dt, acum


def _inproj_kernel(*refs, tm, seq_tiles, step_mode):
    h_ref, g_ref, sh_ref, sc_ref, w_ref, wdt_ref, cw_ref, cb_ref = refs[:8]
    k = 8
    if step_mode:
        pp_ref = refs[k]
        k += 1
    else:
        dtb_ref, alog_ref = refs[k:k + 2]
        k += 2
    zx_ref, dt_ref, u_ref = refs[k:k + 3]
    k += 3
    if not step_mode:
        ac_ref, act_ref, carry_ref, fix_ref = refs[k:k + 4]
    i = pl.program_id(0)
    KC = M_CONV - 1
    tc = INPROJ_TC

    x = h_ref[...]
    ms = jnp.mean(x * x, axis=-1, keepdims=True)
    xn = (x * lax.rsqrt(ms + EPS) * g_ref[...] * (1.0 + sc_ref[0]) + sh_ref[0]).astype(BF16)
    dt_raw = _dot(xn, wdt_ref[...])
    if step_mode:
        dt_ref[...] = dt_raw
    else:
        for ck in range(tm // SSD_CHUNK):
            rows = slice(ck * SSD_CHUNK, (ck + 1) * SSD_CHUNK)
            dt, acum = _decay_terms(dt_raw[rows], dtb_ref[...], alog_ref[...])
            dt_ref[rows, :] = dt
            ac_ref[rows, :] = acum
            act_ref[ck] = acum.T
    if step_mode:
        t = lax.broadcasted_iota(jnp.int32, (tm, 1), 0) & (SUBLANES - 1)
    else:
        @pl.when(i % seq_tiles == 0)
        def _():
            carry_ref[...] = jnp.zeros_like(carry_ref)

    nz, nx = M_D_INNER // tc, M_CONV_DIM // tc

    def xbc_dot(c):
        return _dot(xn, w_ref[:, M_D_INNER + c * tc:M_D_INNER + (c + 1) * tc])

    u_next = xbc_dot(0)
    for c in range(nx):
        cols = slice(c * tc, (c + 1) * tc)
        u = u_next
        if c + 1 < nx:
            u_next = xbc_dot(c + 1)
        for zc in range(c * nz // nx, (c + 1) * nz // nx):
            zx_ref[:, zc * tc:(zc + 1) * tc] = _dot(xn, w_ref[:, zc * tc:(zc + 1) * tc])
        w = [cw_ref[kk:kk + 1, cols] for kk in range(M_CONV)]
        b = cb_ref[:, cols]
        if step_mode:
            u_ref[:, cols] = u
            pp = pp_ref[:, cols]
            acc = u * w[KC] + b
            for s in range(1, M_CONV):
                term = jnp.where(t >= s, pltpu.roll(u, s, 0), pltpu.roll(pp, tm + s - SUBLANES, 0))
                acc = acc + term * w[KC - s]
        else:
            fix_ref[0:SUBLANES, :] = carry_ref[:, cols]
            fix_ref[SUBLANES:2 * SUBLANES, :] = u[0:SUBLANES]
            last = u[tm - SUBLANES:tm]
            carry_ref[:, cols] = last
            u_ref[0, :, cols] = last
            acc = u * w[KC] + b
            fix = fix_ref[SUBLANES:2 * SUBLANES, :] * w[KC] + b
            for s in range(1, M_CONV):
                acc = acc + pltpu.roll(u, s, 0) * w[KC - s]
                fix = fix + fix_ref[pl.ds(SUBLANES - s, SUBLANES), :] * w[KC - s]
            acc = jnp.concatenate([fix, acc[SUBLANES:]], axis=0)
        zx_ref[:, M_D_INNER + c * tc:M_D_INNER + (c + 1) * tc] = _silu(acc)


def _in_proj(h, g, shift, scale, w_zx, w_dt, cw, cb, dtb, alog, *, tm, seq_rows, prev=None):
    R, Dm = h.shape
    N = w_zx.shape[1]
    step_mode = prev is not None
    seq_tiles = 1 if step_mode else seq_rows // tm
    ntiles = R // tm
    once = pl.Buffered(1)
    const = lambda i: (0, 0)
    in_specs = [pl.BlockSpec((tm, Dm), lambda i: (i, 0)), pl.BlockSpec((1, Dm), const),
                _mod_spec(shift, R, tm), _mod_spec(scale, R, tm),
                pl.BlockSpec((Dm, N), const, pipeline_mode=once),
                pl.BlockSpec((Dm, LANES), const, pipeline_mode=once),
                pl.BlockSpec((M_CONV, M_CONV_DIM), const, pipeline_mode=once),
                pl.BlockSpec((1, M_CONV_DIM), const, pipeline_mode=once)]
    args = [h, g, shift[0], scale[0], w_zx, w_dt, cw, cb]
    scratch = []
    lane_spec = pl.BlockSpec((tm, LANES), lambda i: (i, 0))
    out_specs = [pl.BlockSpec((tm, N), lambda i: (i, 0)), lane_spec]
    out_shape = [jax.ShapeDtypeStruct((R, N), F32), jax.ShapeDtypeStruct((R, LANES), F32)]
    if step_mode:
        in_specs.append(pl.BlockSpec((tm, M_CONV_DIM), lambda i: (i, 0)))
        args.append(prev)
        out_specs.append(pl.BlockSpec((tm, M_CONV_DIM), lambda i: (i, 0)))
        out_shape.append(jax.ShapeDtypeStruct((R, M_CONV_DIM), F32))
    else:
        in_specs += [pl.BlockSpec((1, LANES), const)] * 2
        args += [dtb, alog]
        cpt = tm // SSD_CHUNK
        out_specs += [pl.BlockSpec((1, SUBLANES, M_CONV_DIM), lambda i: (i, 0, 0)), lane_spec,
                      pl.BlockSpec((cpt, SSD_CHUNK, LANES), lambda i: (i, 0, 0))]
        out_shape += [jax.ShapeDtypeStruct((ntiles, SUBLANES, M_CONV_DIM), F32),
                      jax.ShapeDtypeStruct((R, LANES), F32),
                      jax.ShapeDtypeStruct((R // SSD_CHUNK, SSD_CHUNK, LANES), F32)]
        scratch = [pltpu.VMEM((SUBLANES, M_CONV_DIM), F32),
                   pltpu.VMEM((2 * SUBLANES, INPROJ_TC), F32)]
    return pl.pallas_call(
        functools.partial(_inproj_kernel, tm=tm, seq_tiles=seq_tiles, step_mode=step_mode),
        grid=(ntiles,),
        in_specs=in_specs,
        out_specs=out_specs,
        out_shape=out_shape,
        scratch_shapes=scratch,
        compiler_params=_cparams(("arbitrary",), vmem_mb=56),
        name="in_proj",
    )(*args)


def _ssd_kernel(*refs, Q, Tv, aliased):
    pre = Tv == Q
    zx_ref, dt_ref = refs[:2]
    k = 2
    if pre:
        ac_ref, act_ref = refs[k:k + 2]
    else:
        dtb_ref, alog_ref = refs[k:k + 2]
    k += 2
    h0_ref, dskip_ref, nw_ref, wout_ref, h_ref, gate_ref = refs[k:k + 6]
    k += 6 + (1 if aliased else 0)
    out_ref, hout_ref, hT_ref, ybuf_ref = refs[k:k + 4]
    k += 4
    c = pl.program_id(1)
    npair = M_HEADS // M_NGROUPS // 2

    @pl.when(c == 0)
    def _():
        for g in range(M_NGROUPS):
            for p in range(npair):
                h0 = g * 2 * npair + 2 * p
                pair = h0_ref[0, 0, h0:h0 + 2].reshape(2 * M_HEADDIM, M_D_STATE)
                hT_ref[g, :, p * LANES:(p + 1) * LANES] = pair.T

    if pre:
        zfull = None
        xact = zx_ref[0, :, M_D_INNER:]
        dt, acum, acum_t = dt_ref[0], ac_ref[0], act_ref[0]
    else:
        pad_ref, dtp_ref = refs[k:k + 2]
        pad_ref[...] = jnp.zeros_like(pad_ref)
        pad_ref[0:Tv, :] = zx_ref[0]
        dtp_ref[...] = jnp.zeros_like(dtp_ref)
        dtp_ref[0:Tv, :] = dt_ref[0]
        zfull = pad_ref[:, 0:M_D_INNER]
        xact = pad_ref[:, M_D_INNER:]
        dt, acum = _decay_terms(dtp_ref[...], dtb_ref[...], alog_ref[...], Tv)
        acum_t = acum.T
    causal = (lax.broadcasted_iota(jnp.int32, (Q, Q), 0)
              >= lax.broadcasted_iota(jnp.int32, (Q, Q), 1))
    lane_lo = lax.broadcasted_iota(jnp.int32, (1, LANES), 1) < M_HEADDIM

    groups = range(M_NGROUPS)
    keeps = (lane_lo, jnp.logical_not(lane_lo))
    c_gs, b_gts, cbs = [], [], []
    for g in groups:
        b_f = xact[:, M_D_INNER + g * M_D_STATE:M_D_INNER + (g + 1) * M_D_STATE]
        c_g = xact[:, M_D_INNER + M_GN + g * M_D_STATE:M_D_INNER + M_GN + (g + 1) * M_D_STATE].astype(BF16)
        c_gs.append(c_g)
        b_gts.append(b_f.T.astype(BF16))
        cbs.append(_dot_nt(c_g, b_f.astype(BF16)))
    xdts, es, xdecs = [], [], []
    for pi in range(M_HEADS // 2):
        h0 = 2 * pi
        xp = xact[:, h0 * M_HEADDIM:(h0 + 2) * M_HEADDIM]
        dtp = jnp.where(lane_lo, dt[:, h0:h0 + 1], dt[:, h0 + 1:h0 + 2])
        ap = jnp.where(lane_lo, acum[:, h0:h0 + 1], acum[:, h0 + 1:h0 + 2])
        xdt = xp * dtp
        xdts.append(xdt)
        es.append(jnp.exp2(ap))
        xdecs.append((xdt * jnp.exp2(ap[Q - 1:Q, :] - ap)).astype(BF16))
    mats = []
    for hh in range(M_HEADS):
        seg = acum[:, hh:hh + 1] - acum_t[hh:hh + 1, :]
        lm = jnp.exp2(jnp.where(causal, seg, NEG_INF))
        mats.append((cbs[hh // (2 * npair)] * lm).astype(BF16))
    ydiag = []
    for pi in range(M_HEADS // 2):
        x0 = jnp.where(keeps[0], xdts[pi], 0.0).astype(BF16)
        x1 = jnp.where(keeps[1], xdts[pi], 0.0).astype(BF16)
        if Q % LANES == 0:
            lhs = jnp.concatenate([mats[2 * pi], mats[2 * pi + 1]], axis=1)
            ydiag.append(_dot(lhs, jnp.concatenate([x0, x1], axis=0)))
        else:
            ydiag.append(_dot(mats[2 * pi], x0) + _dot(mats[2 * pi + 1], x1))
    y_offs = []
    for g in groups:
        e_g = jnp.concatenate(es[g * npair:(g + 1) * npair], axis=1)
        xdec_g = jnp.concatenate(xdecs[g * npair:(g + 1) * npair], axis=1)
        h_prev = hT_ref[g]
        y_offs.append(_dot(c_gs[g], h_prev.astype(BF16)) * e_g)
        hT_ref[g] = h_prev * e_g[Q - 1:Q, :] + _dot(b_gts[g], xdec_g)
    for g in groups:
        gs = slice(g * M_GN, (g + 1) * M_GN)
        y = (jnp.concatenate(ydiag[g * npair:(g + 1) * npair], axis=1) + y_offs[g]
             + xact[:, gs] * dskip_ref[:, gs])
        yg = y * _silu(zx_ref[0, :, gs] if zfull is None else zfull[:, gs])
        ms = jnp.mean(yg * yg, axis=-1, keepdims=True)
        ybuf_ref[:, gs] = (yg * lax.rsqrt(ms + EPS) * nw_ref[:, gs]).astype(BF16)

    tb = -(-Tv // BF16_ROWS) * BF16_ROWS
    out_ref[0] = h_ref[0] + gate_ref[0] * _dot(ybuf_ref[0:tb, :], wout_ref[...])[0:Tv]

    @pl.when(c == pl.num_programs(1) - 1)
    def _():
        for g in range(M_NGROUPS):
            for p in range(npair):
                h0 = g * 2 * npair + 2 * p
                pair = hT_ref[g, :, p * LANES:(p + 1) * LANES].T
                hout_ref[0, 0, h0:h0 + 2] = pair.reshape(2, M_HEADDIM, M_D_STATE)


def _ssd_mixer(zx, dtr, decay, h0_all, layer, dskip, nw, w_out, h, gate, states=None):
    B, L, W = zx.shape
    Dm = h.shape[-1]
    Q = SSD_CHUNK if L >= SSD_CHUNK else SSD_SHORT_CHUNK
    Tv = min(L, Q)
    nc = L // Tv
    const2 = lambda b, c: (0, 0)
    row3 = lambda b, c: (b, c, 0)
    state_spec = pl.BlockSpec((1, 1, M_HEADS, M_HEADDIM, M_D_STATE), lambda b, c: (layer, b, 0, 0, 0))
    aliased = states is not None
    extra_specs = [pl.BlockSpec(memory_space=pl.ANY)] if aliased else []
    extra_args = [states] if aliased else []
    n_in = 4 + 6
    scratch = [pltpu.VMEM((M_NGROUPS, M_D_STATE, M_GN), F32), pltpu.VMEM((Q, M_D_INNER), BF16)]
    if Tv == Q:
        decay_specs = [pl.BlockSpec((1, Q, LANES), row3),
                       pl.BlockSpec((1, Q, LANES), lambda b, c: (b * nc + c, 0, 0))]
    else:
        decay_specs = [pl.BlockSpec((1, LANES), const2)] * 2
        scratch += [pltpu.VMEM((Q, W), F32), pltpu.VMEM((Q, LANES), F32)]
    return pl.pallas_call(
        functools.partial(_ssd_kernel, Q=Q, Tv=Tv, aliased=aliased),
        grid=(B, nc),
        in_specs=[pl.BlockSpec((1, Tv, W), row3),
                  pl.BlockSpec((1, Tv, LANES), row3)]
        + decay_specs
        + [state_spec,
           pl.BlockSpec((1, M_D_INNER), const2),
           pl.BlockSpec((1, M_D_INNER), const2),
           pl.BlockSpec((M_D_INNER, Dm), const2, pipeline_mode=pl.Buffered(1)),
           pl.BlockSpec((1, Tv, Dm), row3),
           _mod_spec(gate, B, 1, 2)]
        + extra_specs,
        out_specs=[pl.BlockSpec((1, Tv, Dm), row3), state_spec],
        out_shape=[jax.ShapeDtypeStruct((B, L, Dm), F32),
                   jax.ShapeDtypeStruct(h0_all.shape, F32)],
        input_output_aliases={n_in: 1} if aliased else {},
        scratch_shapes=scratch,
        compiler_params=_cparams(("parallel", "arbitrary")),
        name="ssd_mixer",
    )(zx, dtr, *decay, h0_all, dskip, nw, w_out, h, gate[0], *extra_args)


def _proj_res_kernel(a_ref, w_ref, h_ref, gate_ref, o_ref):
    o_ref[...] = h_ref[...] + gate_ref[0] * _dot(a_ref[...], w_ref[...])


def _proj_residual(a, w, h, gate, *, tm):
    R, K = a.shape
    Dm = w.shape[1]
    return pl.pallas_call(
        _proj_res_kernel,
        grid=(R // tm,),
        in_specs=[pl.BlockSpec((tm, K), lambda i: (i, 0)),
                  pl.BlockSpec((K, Dm), lambda i: (0, 0)),
                  pl.BlockSpec((tm, Dm), lambda i: (i, 0)),
                  _mod_spec(gate, R, tm)],
        out_specs=pl.BlockSpec((tm, Dm), lambda i: (i, 0)),
        out_shape=jax.ShapeDtypeStruct((R, Dm), F32),
        compiler_params=_cparams(("parallel",)),
        name="proj_residual",
    )(a, w, h, gate[0])


def _attn_out_kernel(o0_ref, o1_ref, o2_ref, l0_ref, l1_ref, l2_ref, e_ref, w_ref, h_ref,
                     gate_ref, out_ref, nat_o, nat_l, *, tm):
    nlc = ATT_OUT // LANES
    assert DIL_PAIRS[0][1] == 1
    for g, (o_ref, l_ref) in enumerate(((o0_ref, l0_ref), (o1_ref, l1_ref), (o2_ref, l2_ref))):
        d = DIL_PAIRS[g][1]
        rows = tm // d
        if d == 1:
            continue
        for r in range(d):
            sl = pl.ds(r, rows, stride=d)
            nat_l.at[g][sl, :] = l_ref[0, r]
            for lc in range(nlc):
                nat_o.at[g * nlc + lc][sl, :] = o_ref[0, r, lc]
    l0, l1, l2 = l0_ref[0, 0], nat_l[1], nat_l[2]
    mx = jnp.maximum(jnp.maximum(l0, l1), l2)
    w0, w1, w2 = jnp.exp2(l0 - mx), jnp.exp2(l1 - mx), jnp.exp2(l2 - mx)
    inv = 1.0 / (w0 + w1 + w2)
    alphas = []
    for wg in (w0, w1, w2):
        a = wg * inv
        hi = a.astype(BF16)
        lo = (a - hi.astype(F32)).astype(BF16)
        alphas.append(_dot(hi, e_ref[...]) + _dot(lo, e_ref[...]))
    pieces = []
    for lc in range(nlc):
        ls = slice(lc * LANES, (lc + 1) * LANES)
        t = alphas[0][:, ls] * o0_ref[0, 0, lc]
        for g in range(1, len(DIL_PAIRS)):
            t = t + alphas[g][:, ls] * nat_o[g * nlc + lc]
        pieces.append(t.astype(BF16))
    comb = jnp.concatenate(pieces, axis=1)
    out_ref[...] = h_ref[...] + gate_ref[0] * _dot(comb, w_ref[...])


def _attn_out(o_parts, lse_parts, expand, w, h, gate, *, B, tm):
    R, Dm = h.shape
    seq_tiles = R // B // tm
    nlc = ATT_OUT // LANES
    ng = len(DIL_PAIRS)
    row = lambda i: (i, 0)
    o_specs = [pl.BlockSpec((1, d, nlc, tm // d, LANES),
                            lambda i: (i // seq_tiles, 0, 0, i % seq_tiles, 0)) for _, d in DIL_PAIRS]
    l_specs = [pl.BlockSpec((1, d, tm // d, LANES),
                            lambda i: (i // seq_tiles, 0, i % seq_tiles, 0)) for _, d in DIL_PAIRS]
    return pl.pallas_call(
        functools.partial(_attn_out_kernel, tm=tm),
        grid=(R // tm,),
        in_specs=o_specs + l_specs
        + [pl.BlockSpec((LANES, ATT_OUT), lambda i: (0, 0)),
           pl.BlockSpec((ATT_OUT, Dm), lambda i: (0, 0)),
           pl.BlockSpec((tm, Dm), row),
           _mod_spec(gate, R, tm)],
        out_specs=pl.BlockSpec((tm, Dm), row),
        out_shape=jax.ShapeDtypeStruct((R, Dm), F32),
        scratch_shapes=[pltpu.VMEM((ng * nlc, tm, LANES), F32), pltpu.VMEM((ng, tm, LANES), F32)],
        compiler_params=_cparams(("parallel",)),
        name="attn_out",
    )(*o_parts, *lse_parts, expand, w, h, gate[0])


def _ffn_kernel(*refs, tm, seq_tiles, step_mode, has_final):
    h_ref, g_ref, sh_ref, sc_ref, gate_ref, wup_ref, cw_ref, cb_ref, wd_ref = refs[:9]
    k = 9
    if step_mode:
        pp_ref = refs[k]
        k += 1
    if has_final:
        fg_ref = refs[k]
        k += 1
    out_ref, u_ref, act_ref = refs[k:k + 3]
    k += 3
    if not step_mode:
        carry_ref, fix_ref = refs[k:k + 2]
    i = pl.program_id(0)
    KC = FFN_CONV - 1
    tc = FFN_TC

    x = h_ref[...]
    ms = jnp.mean(x * x, axis=-1, keepdims=True)
    xn = (x * lax.rsqrt(ms + EPS) * g_ref[...] * (1.0 + sc_ref[0]) + sh_ref[0]).astype(BF16)

    if step_mode:
        t = lax.broadcasted_iota(jnp.int32, (tm, 1), 0) & (SUBLANES - 1)
    else:
        @pl.when(i % seq_tiles == 0)
        def _():
            carry_ref[...] = jnp.zeros_like(carry_ref)

    def up(c):
        return (_dot(xn, wup_ref[:, c * tc:(c + 1) * tc]),
                _dot(xn, wup_ref[:, D_FF + c * tc:D_FF + (c + 1) * tc]))

    def conv(u, cols, part):
        w = [cw_ref[kk:kk + 1, cols] for kk in range(FFN_CONV)]
        b = cb_ref[:, cols]
        if step_mode:
            u_ref[:, cols] = u
            pp = pp_ref[:, cols]
            acc = u * w[KC]
            for s in range(1, FFN_CONV):
                term = jnp.where(t >= s, pltpu.roll(u, s, 0), pltpu.roll(pp, tm + s - SUBLANES, 0))
                acc = acc + term * w[KC - s]
            return acc + b
        fix_ref[part, 0:SUBLANES, :] = carry_ref[:, cols]
        fix_ref[part, SUBLANES:2 * SUBLANES, :] = u[0:SUBLANES]
        last = u[tm - SUBLANES:tm]
        carry_ref[:, cols] = last
        u_ref[0, :, cols] = last
        acc = u * w[KC] + b
        fix = fix_ref[part, SUBLANES:2 * SUBLANES, :] * w[KC] + b
        for s in range(1, FFN_CONV):
            acc = acc + pltpu.roll(u, s, 0) * w[KC - s]
            fix = fix + fix_ref[part, pl.ds(SUBLANES - s, SUBLANES), :] * w[KC - s]
        return jnp.concatenate([fix, acc[SUBLANES:]], axis=0)

    for c in range(D_FF // tc):
        ug, uv = up(c)
        cg = conv(ug, slice(c * tc, (c + 1) * tc), 0)
        cv = conv(uv, slice(D_FF + c * tc, D_FF + (c + 1) * tc), 1)
        act_ref[:, c * tc:(c + 1) * tc] = (_silu(cg) * cv).astype(BF16)

    hn = x + gate_ref[0] * _dot(act_ref[...], wd_ref[...])
    if has_final:
        ms = jnp.mean(hn * hn, axis=-1, keepdims=True)
        hn = hn * lax.rsqrt(ms + EPS) * fg_ref[...]
    out_ref[...] = hn


def _conv_ffn(h, g, shift, scale, gate, w_up, cw, cb, w_down, *, tm, seq_rows,
              prev=None, final_g=None):
    R, Dm = h.shape
    step_mode = prev is not None
    seq_tiles = 1 if step_mode else seq_rows // tm
    ntiles = R // tm
    once = pl.Buffered(1)
    const = lambda i: (0, 0)
    row_spec = pl.BlockSpec((tm, Dm), lambda i: (i, 0))
    in_specs = [row_spec, pl.BlockSpec((1, Dm), const),
                _mod_spec(shift, R, tm), _mod_spec(scale, R, tm), _mod_spec(gate, R, tm),
                pl.BlockSpec((Dm, 2 * D_FF), const, pipeline_mode=once),
                pl.BlockSpec((FFN_CONV, 2 * D_FF), const, pipeline_mode=once),
                pl.BlockSpec((1, 2 * D_FF), const, pipeline_mode=once),
                pl.BlockSpec((D_FF, Dm), const, pipeline_mode=once)]
    args = [h, g, shift[0], scale[0], gate[0], w_up, cw, cb, w_down]
    scratch = [pltpu.VMEM((tm, D_FF), BF16)]
    if step_mode:
        in_specs.append(pl.BlockSpec((tm, 2 * D_FF), lambda i: (i, 0)))
        args.append(prev)
        u_spec = pl.BlockSpec((tm, 2 * D_FF), lambda i: (i, 0))
        u_shape = jax.ShapeDtypeStruct((R, 2 * D_FF), F32)
    else:
        u_spec = pl.BlockSpec((1, SUBLANES, 2 * D_FF), lambda i: (i, 0, 0))
        u_shape = jax.ShapeDtypeStruct((ntiles, SUBLANES, 2 * D_FF), F32)
        scratch += [pltpu.VMEM((SUBLANES, 2 * D_FF), F32),
                    pltpu.VMEM((2, 2 * SUBLANES, FFN_TC), F32)]
    if final_g is not None:
        in_specs.append(pl.BlockSpec((1, Dm), const))
        args.append(final_g)
    return pl.pallas_call(
        functools.partial(_ffn_kernel, tm=tm, seq_tiles=seq_tiles,
                          step_mode=step_mode, has_final=final_g is not None),
        grid=(ntiles,),
        in_specs=in_specs,
        out_specs=[row_spec, u_spec],
        out_shape=[jax.ShapeDtypeStruct((R, Dm), F32), u_shape],
        scratch_shapes=scratch,
        compiler_params=_cparams(("arbitrary",), vmem_mb=56),
        name="conv_ffn",
    )(*args)


def _band_attn_kernel(q_ref, kc_ref, kp_ref, vc_ref, vp_ref, bias_ref, o_ref, lse_ref, bias0_ref, *, nbk):
    c = pl.program_id(2)
    BL = ATT_BLOCK
    lane = lax.broadcasted_iota(jnp.int32, (1, LANES), 1)
    bias0_ref[...] = jnp.where(c > 0, bias_ref[0], NEG_INF)
    lane_lo = lane < HEAD_DIM
    keeps = (lane_lo, jnp.logical_not(lane_lo))
    zero = jnp.zeros((BL, LANES), BF16)
    units = [(sb, h) for sb in range(nbk) for h in range(HEADS_PER_GROUP)]

    def kv_prev(cur_ref, prev_ref, sb, ps):
        if sb == 0:
            return prev_ref[0, 0, :, ps]
        return cur_ref[0, 0, (sb - 1) * BL:sb * BL, ps]

    scores = {}
    for sb, h in units:
        ps = slice((h // 2) * LANES, (h // 2 + 1) * LANES)
        rows = slice(sb * BL, (sb + 1) * BL)
        qh = jnp.where(keeps[h % 2], q_ref[0, 0, rows, ps], zero)
        s_p = _dot_nt(qh, kv_prev(kc_ref, kp_ref, sb, ps)) + (bias0_ref[...] if sb == 0 else bias_ref[0])
        s_c = _dot_nt(qh, kc_ref[0, 0, rows, ps]) + bias_ref[1]
        scores[sb, h] = (s_p, s_c)
    probs = {}
    for sb in range(nbk):
        lse = jnp.zeros((BL, LANES), F32)
        for h in range(HEADS_PER_GROUP):
            s_p, s_c = scores[sb, h]
            m = jnp.max(jnp.maximum(s_p, s_c), axis=-1, keepdims=True)
            p_p = jnp.exp2(s_p - m)
            p_c = jnp.exp2(s_c - m)
            l = jnp.sum(p_p + p_c, axis=-1, keepdims=True)
            lse = jnp.where(lane == h, m + jnp.log(l) * LOG2E, lse)
            probs[sb, h] = (p_p.astype(BF16), p_c.astype(BF16), 1.0 / l)
        lse_ref[0, 0, sb * BL:(sb + 1) * BL, :] = lse
    for sb in range(nbk):
        rows = slice(sb * BL, (sb + 1) * BL)
        for p in range(HEADS_PER_GROUP // 2):
            ps = slice(p * LANES, (p + 1) * LANES)
            vc, vp = vc_ref[0, 0, rows, ps], kv_prev(vc_ref, vp_ref, sb, ps)
            o_pair = None
            for hh in range(2):
                p_p, p_c, _ = probs[sb, 2 * p + hh]
                o = _dot(p_p, jnp.where(keeps[hh], vp, zero)) + _dot(p_c, jnp.where(keeps[hh], vc, zero))
                o_pair = o if o_pair is None else o_pair + o
            scale = jnp.where(lane_lo, probs[sb, 2 * p][2], probs[sb, 2 * p + 1][2])
            o_ref[0, 0, p, rows, :] = o_pair * scale


def _band_attention(q, k, v):
    B, d, n, W = q.shape
    nbk = BAND_BLOCKS
    blk = nbk * ATT_BLOCK
    cur = pl.BlockSpec((1, 1, blk, W), lambda b, r, c: (b, r, c, 0))
    prv = pl.BlockSpec((1, 1, ATT_BLOCK, W), lambda b, r, c: (b, r, jnp.maximum(c * nbk - 1, 0), 0))
    i = jnp.arange(ATT_BLOCK)[:, None]
    j = jnp.arange(ATT_BLOCK)[None, :]
    bias = jnp.where(jnp.stack([j >= i, j <= i]), 0.0, NEG_INF).astype(F32)
    return pl.pallas_call(
        functools.partial(_band_attn_kernel, nbk=nbk),
        grid=(B, d, n // blk),
        in_specs=[cur, cur, prv, cur, prv,
                  pl.BlockSpec((2, ATT_BLOCK, ATT_BLOCK), lambda b, r, c: (0, 0, 0))],
        scratch_shapes=[pltpu.VMEM((ATT_BLOCK, ATT_BLOCK), F32)],
        out_specs=[pl.BlockSpec((1, 1, W // LANES, blk, LANES), lambda b, r, c: (b, r, 0, c, 0)),
                   pl.BlockSpec((1, 1, blk, LANES), lambda b, r, c: (b, r, c, 0))],
        out_shape=[jax.ShapeDtypeStruct((B, d, W // LANES, n, LANES), F32),
                   jax.ShapeDtypeStruct((B, d, n, LANES), F32)],
        compiler_params=_cparams(("parallel", "parallel", "arbitrary")),
        name="band_attention",
    )(q, k, k, v, v, bias)


def _dec_attn_kernel(q_ref, kn_ref, vn_ref, k0_ref, v0_ref, k1_ref, v1_ref, k2_ref, v2_ref,
                     o_ref, *, T, HB):
    s = pl.program_id(1)
    caches = ((k0_ref, v0_ref), (k1_ref, v1_ref), (k2_ref, v2_ref))
    t = lax.broadcasted_iota(jnp.int32, (T, 1), 0)
    off = (pl.program_id(0) % (LANES // T)) * T
    u = lax.broadcasted_iota(jnp.int32, (1, LANES), 1) - off
    masks = []
    for g, (win, dil) in enumerate(DIL_PAIRS):
        p = lax.broadcasted_iota(jnp.int32, (1, win), 1)
        dist = win + t - p
        cmask = (dist <= win) & ((dist & (dil - 1)) == 0)
        nmask = (u >= 0) & (u <= t) & (((t - u) & (dil - 1)) == 0)
        masks.append((cmask, nmask))
    units = [(hh, g) for hh in range(HB) for g in range(len(DIL_PAIRS))]
    scores = {}
    for hh, g in units:
        head = g * HEADS_PER_GROUP + s * HB + hh
        cmask, nmask = masks[g]
        q = q_ref[0, head].astype(BF16)
        sc = jnp.where(cmask, _dot(q, caches[g][0][0, hh].astype(BF16)), NEG_INF)
        sn = jnp.where(nmask, _dot(q, kn_ref[head].astype(BF16)), NEG_INF)
        scores[hh, g] = (sc, sn)
    probs = {}
    for hh, g in units:
        sc, sn = scores[hh, g]
        m = jnp.maximum(jnp.max(sc, axis=-1, keepdims=True), jnp.max(sn, axis=-1, keepdims=True))
        pc = jnp.exp2(sc - m)
        pn = jnp.exp2(sn - m)
        l = jnp.sum(pc, axis=-1, keepdims=True) + jnp.sum(pn, axis=-1, keepdims=True)
        probs[hh, g] = (m, l, pc.astype(BF16), pn.astype(BF16))
    outs = {}
    for hh, g in units:
        head = g * HEADS_PER_GROUP + s * HB + hh
        m, l, pc, pn = probs[hh, g]
        outs[hh, g] = (_dot_nt(pc, caches[g][1][0, hh].astype(BF16))
                       + _dot_nt(pn, vn_ref[head].astype(BF16)))
    for hh in range(HB):
        ms = [probs[hh, g][0] for g in range(len(DIL_PAIRS))]
        mx = jnp.maximum(jnp.maximum(ms[0], ms[1]), ms[2])
        num = None
        den = None
        for g in range(len(DIL_PAIRS)):
            w = jnp.exp2(ms[g] - mx)
            num = w * outs[hh, g] if num is None else num + w * outs[hh, g]
            den = w * probs[hh, g][1] if den is None else den + w * probs[hh, g][1]
        o_ref[0, hh] = num / den


def _decode_attention(qh, knt, vnt, cache_kt, cache_vt):
    B, H, T, E = qh.shape
    per_tile = LANES // T
    P = cache_kt.shape[-1]
    HB = 8
    nhb = HEADS_PER_GROUP // HB
    specs = []
    for g, (win, dil) in enumerate(DIL_PAIRS):
        imap = functools.partial(lambda b, s, g, last: (b, g * nhb + s, 0, last),
                                 g=g, last=P // win - 1)
        specs += [pl.BlockSpec((1, HB, E, win), imap)] * 2
    full_q = pl.BlockSpec((1, H, T, E), lambda b, s: (b, 0, 0, 0))
    full_n = pl.BlockSpec((H, E, LANES), lambda b, s: (0, 0, b // per_tile))
    return pl.pallas_call(
        functools.partial(_dec_attn_kernel, T=T, HB=HB),
        grid=(B, nhb),
        in_specs=[full_q, full_n, full_n] + specs,
        out_specs=pl.BlockSpec((1, HB, T, E), lambda b, s: (b, s, 0, 0)),
        out_shape=jax.ShapeDtypeStruct((B, HEADS_PER_GROUP, T, E), F32),
        compiler_params=_cparams(("parallel", "arbitrary"), vmem_mb=56),
        name="decode_attention",
    )(qh, knt, vnt, cache_kt, cache_vt, cache_kt, cache_vt, cache_kt, cache_vt)


def _rope_tables(pos):
    half = HEAD_DIM // 2
    inv = ROPE_THETA ** (-jnp.arange(half, dtype=F32) * (2.0 / HEAD_DIM))
    ang = pos.astype(F32)[:, None] * inv[None, :]
    cos, sin = jnp.cos(ang), jnp.sin(ang)
    cos_t = jnp.concatenate([cos, cos, cos, cos], axis=1)
    sin_t = jnp.concatenate([-sin, sin, -sin, sin], axis=1)
    return cos_t, sin_t


def _prep_params(p):
    w = {}
    w_in = p['m_w_in']
    w['w_in_zx'] = w_in[:, :, :M_D_INNER + M_CONV_DIM].astype(BF16)
    w['w_in_dt'] = jnp.pad(w_in[:, :, M_D_INNER + M_CONV_DIM:],
                           ((0, 0), (0, 0), (0, LANES - M_HEADS))).astype(BF16)
    w['dt_bias'] = jnp.pad(p['m_dt_bias'], ((0, 0), (0, LANES - M_HEADS)))[:, None, :]
    w['a_log'] = jnp.pad(p['m_A_log'], ((0, 0), (0, LANES - M_HEADS)))[:, None, :]
    w['d_skip'] = jnp.repeat(p['m_D'], M_HEADDIM, axis=1)[:, None, :]
    w['m_norm'] = p['m_norm'][:, None, :]
    w['conv_w'] = p['m_conv_w']
    w['conv_b'] = p['m_conv_b'][:, None, :]
    w['w_out'] = p['m_w_out'].astype(BF16)
    w['w_q'] = (p['w_q'] * (HEAD_DIM ** -0.5 * LOG2E)).astype(BF16)
    w['w_kv'] = p['w_kv'].astype(BF16)
    w['w_kv_t'] = p['w_kv'].T.astype(BF16)
    w['w_o'] = p['w_o'].astype(BF16)
    w['ffn_up'] = p['ffn_w_up'].astype(BF16)
    w['ffn_down'] = p['ffn_w_down'].astype(BF16)
    w['ffn_cw'] = p['ffn_conv_w']
    w['ffn_cb'] = p['ffn_conv_b'][:, None, :]
    head = jnp.arange(ATT_OUT) // HEAD_DIM
    w['expand'] = (jnp.arange(LANES)[:, None] == head[None, :]).astype(BF16)
    return w


def _trunk(x, mods, kvmod, gate_b, pos, ssm0, conv0, ffn0, kv_past, p, w, *, tm, step):
    B, T, Dm = x.shape
    R = B * T
    h = x.reshape(R, Dm)
    cos_t, sin_t = _rope_tables(pos)
    cos_r = jnp.tile(cos_t, (B, 1))
    sin_r = jnp.tile(sin_t, (B, 1))
    conv_out, ffn_out = [], []
    ssm_states = None
    k_new = v_new = new_t = kv_split = None
    for i in range(DEPTH):
        sh1, sc1, g1, sh2, sc2, g2 = [(mods[i], k) for k in range(6)]
        if i < N_A:
            tmi = min(tm, 512)
            inproj = functools.partial(_in_proj, h, p['norm_mix'][i][None], sh1, sc1, w['w_in_zx'][i],
                                       w['w_in_dt'][i], w['conv_w'][i], w['conv_b'][i],
                                       w['dt_bias'][i], w['a_log'][i], tm=tmi, seq_rows=T)
            if step:
                prev = jnp.pad(conv0[i], ((0, 0), (SUBLANES - (M_CONV - 1), 0), (0, 0)))
                zx, dtr, u = inproj(prev=prev.reshape(R, M_CONV_DIM))
                conv_out.append(u.reshape(B, T, M_CONV_DIM)[:, T - (M_CONV - 1):])
                decay = (w['dt_bias'][i], w['a_log'][i])
            else:
                zx, dtr, u, acum, acum_t = inproj()
                seq_tiles = T // tmi
                conv_out.append(u[seq_tiles - 1::seq_tiles, SUBLANES - (M_CONV - 1):])
                decay = (acum.reshape(B, T, LANES), acum_t)
            h3, ssm_states = _ssd_mixer(zx.reshape(B, T, -1), dtr.reshape(B, T, LANES), decay, ssm0,
                                        i, w['d_skip'][i], w['m_norm'][i], w['w_out'][i],
                                        h.reshape(B, T, Dm), gate_b[i], states=ssm_states)
            h = h3.reshape(R, Dm)
        else:
            jb = i - N_A
            if step:
                q = _norm_mod_matmul(h, p['norm_mix'][i][None], sh1, sc1, w['w_q'][jb],
                                     (cos_r, sin_r), tm=tm, tn=1024, out_dtype=F32,
                                     n_rope=ATT_WIDTH // 1024)
                qh = q.reshape(B, T, ATT_HEADS, HEAD_DIM).transpose(0, 2, 1, 3)
                o = _decode_attention(qh, *new_t, *kv_past)
                o = o.transpose(0, 2, 1, 3).reshape(R, ATT_OUT).astype(BF16)
                h = _proj_residual(o, w['w_o'][jb], h, g1, tm=tm)
            else:
                q_split = _norm_mod_matmul_split(h, p['norm_mix'][i][None], sh1, sc1, w['w_q'][jb],
                                                 cos_t, sin_t, B=B, tm=SPLIT_TM, n_rope=len(DIL_PAIRS))
                o_parts, lse_parts = [], []
                for g in range(len(DIL_PAIRS)):
                    og, lg = _band_attention(q_split[g], *kv_split[g])
                    o_parts.append(og)
                    lse_parts.append(lg)
                h = _attn_out(o_parts, lse_parts, w['expand'], w['w_o'][jb], h, g1, B=B, tm=SPLIT_TM)
        last = i == DEPTH - 1
        fin = p['final_norm'][None] if last else None
        if step:
            prev = jnp.pad(ffn0[i], ((0, 0), (SUBLANES - (FFN_CONV - 1), 0), (0, 0))).reshape(R, 2 * D_FF)
            h, u = _conv_ffn(h, p['norm_ffn'][i][None], sh2, sc2, g2, w['ffn_up'][i], w['ffn_cw'][i],
                             w['ffn_cb'][i], w['ffn_down'][i], tm=tm, seq_rows=T, prev=prev, final_g=fin)
            ffn_out.append(u.reshape(B, T, 2 * D_FF)[:, T - (FFN_CONV - 1):])
        else:
            tmf = min(tm, 512)
            h, u = _conv_ffn(h, p['norm_ffn'][i][None], sh2, sc2, g2, w['ffn_up'][i], w['ffn_cw'][i],
                             w['ffn_cb'][i], w['ffn_down'][i], tm=tmf, seq_rows=T, final_g=fin)
            seq_tiles = T // tmf
            ffn_out.append(u[seq_tiles - 1::seq_tiles, SUBLANES - (FFN_CONV - 1):])
        if i == N_A - 1:
            ksh, ksc = (kvmod, 0), (kvmod, 1)
            keep = min(WINDOW_MAX, T)
            if step:
                h_tail, nb_k, tm_k = h, 1, tm
                rope_k = (jnp.tile(cos_t[:, :HEAD_DIM].T, (1, B)), jnp.tile(sin_t[:, :HEAD_DIM].T, (1, B)))
            else:
                ng = len(DIL_PAIRS)
                kvs = _norm_mod_matmul_split(h, p['kv_norm'][None], ksh, ksc, w['w_kv'], cos_t, sin_t,
                                             B=B, tm=SPLIT_TM, n_rope=ng)
                kv_split = [(kvs[g], kvs[ng + g]) for g in range(ng)]
                h_tail = h.reshape(B, T, Dm)[:, T - keep:].reshape(B * keep, Dm)
                nb_k, tm_k = B, min(tm, keep)
                rope_k = (cos_t[T - keep:, :HEAD_DIM].T, sin_t[T - keep:, :HEAD_DIM].T)
            kvt = _norm_mod_matmul_t(h_tail, p['kv_norm'][None], ksh, ksc, w['w_kv_t'], rope_k,
                                     nb=nb_k, tm=tm_k, tn=1024, n_rope=ATT_WIDTH // 1024)
            if step:
                kt, vt = kvt[0, 0], kvt[1, 0]
                new_t = [a.reshape(ATT_HEADS, HEAD_DIM, R) for a in (kt, vt)]
                k_new, v_new = [a.reshape(ATT_HEADS, HEAD_DIM, B, T).transpose(2, 3, 0, 1)
                                for a in (kt, vt)]
            else:
                k_new, v_new = [a.reshape(B, ATT_HEADS, HEAD_DIM, keep).transpose(0, 3, 1, 2)
                                for a in (kvt[0], kvt[1])]
    return (h.reshape(B, T, Dm), ssm_states, jnp.stack(conv_out), jnp.stack(ffn_out),
            k_new, v_new)


def kernel(x_prompt, x_sample, state_ssm, state_conv, state_ffn_conv, cache_k, cache_v, c_prompt, c_sample, ada_w, ada_b, norm_mix, norm_ffn, m_w_in, m_conv_w, m_conv_b, m_dt_bias, m_A_log, m_D, m_norm, m_w_out, kv_norm, kv_ada_w, kv_ada_b, w_kv, w_q, w_o, ffn_w_up, ffn_conv_w, ffn_conv_b, ffn_w_down, final_norm):
    p = dict(norm_mix=norm_mix, norm_ffn=norm_ffn, m_w_in=m_w_in, m_conv_w=m_conv_w,
             m_conv_b=m_conv_b, m_dt_bias=m_dt_bias, m_A_log=m_A_log, m_D=m_D, m_norm=m_norm,
             m_w_out=m_w_out, kv_norm=kv_norm, w_kv=w_kv, w_q=w_q, w_o=w_o, ffn_w_up=ffn_w_up,
             ffn_conv_w=ffn_conv_w, ffn_conv_b=ffn_conv_b, ffn_w_down=ffn_w_down,
             final_norm=final_norm)
    w = _prep_params(p)
    Bp, S, Dm = x_prompt.shape
    Bs, T, _ = x_sample.shape

    nrow = Bp + Bs
    npad = -(-nrow // SUBLANES) * SUBLANES
    c_all = jnp.pad(jnp.concatenate([c_prompt, c_sample], axis=0), ((0, npad - nrow), (0, 0)))
    mod = _ada_linear(c_all, ada_w, ada_b[:, None, :])
    kvm = _ada_linear(c_all, kv_ada_w[None], kv_ada_b[None, None, :])[0]

    def per_seq(m):
        return m[:, None, :]

    def per_token(m):
        return jnp.repeat(m, T, axis=0)[None]

    mods_p = [per_seq(mod[i, :Bp]) for i in range(DEPTH)]
    mods_s = [per_token(mod[i, Bp:nrow]) for i in range(DEPTH)]
    kvm_p = per_seq(kvm[:Bp])
    kvm_s = per_token(kvm[Bp:nrow])
    gate_p = [(mods_p[i], 2) for i in range(N_A)]
    gate_s = [(per_seq(mod[i, Bp:nrow]), 2) for i in range(N_A)]

    ssm0 = jnp.zeros((N_A, Bp, M_HEADS, M_HEADDIM, M_D_STATE), state_ssm.dtype)
    y_p, ssm_p, conv_p, ffn_p, k_p, v_p = _trunk(
        x_prompt, mods_p, kvm_p, gate_p, jnp.arange(S, dtype=jnp.int32), ssm0, None, None, None, p, w,
        tm=1024, step=False)
    cache_t = (cache_k.transpose(0, 2, 3, 1), cache_v.transpose(0, 2, 3, 1))
    y_s, ssm_s, conv_s, ffn_s, k_s, v_s = _trunk(
        x_sample, mods_s, kvm_s, gate_s, PAST_LEN + jnp.arange(T, dtype=jnp.int32), state_ssm,
        state_conv, state_ffn_conv, cache_t, p, w, tm=Bs * T, step=True)
    return (y_p, y_s, ssm_p, ssm_s, conv_p, conv_s, ffn_p, ffn_s, k_p, k_s, v_p, v_s)
```

```python
import functools
import math

import jax
import jax.numpy as jnp
from jax import lax
from jax.experimental import pallas as pl
from jax.experimental.pallas import tpu as pltpu

F32 = jnp.float32
BF16 = jnp.bfloat16

D_MODEL = 1024
DEPTH = 4
N_A = 2
M_D_INNER = 2048
M_HEADDIM = 64
M_HEADS = 32
M_NGROUPS = 4
M_D_STATE = 128
M_CONV = 4
M_GN = 512
M_CONV_DIM = 3072
HEAD_DIM = 64
HEADS_PER_GROUP = 16
DIL_PAIRS = ((128, 1), (512, 4), (2048, 16))
ATT_HEADS = 48
ATT_WIDTH = 3072
ATT_OUT = 1024
WINDOW_MAX = 2048
ATT_BLOCK = 128
ROPE_THETA = 10000.0
D_FF = 2816
FFN_CONV = 3
EPS = 1e-6
SSD_CHUNK = 128
SSD_SHORT_CHUNK = 32
PAST_LEN = 8192

LANES = 128
SUBLANES = 8
BF16_ROWS = 16
LOG2E = 1.4426950408889634
FFN_TC = 256
INPROJ_TC = 256
BAND_BLOCKS = 4
SPLIT_TM = 512
NEG_INF = float("-inf")


def _cparams(sem, vmem_mb=48):
    return pltpu.CompilerParams(dimension_semantics=sem,
                                vmem_limit_bytes=vmem_mb * 1024 * 1024)


def _silu(x):
    return x * jax.nn.sigmoid(x)


def _dot(a, b):
    return jnp.dot(a, b, preferred_element_type=F32)


def _dot_nt(a, b):
    return lax.dot_general(a, b, (((1,), (1,)), ((), ())), preferred_element_type=F32)


def _ada_kernel(c_ref, w_ref, b_ref, o_ref):
    cs = _silu(c_ref[...]).astype(BF16)
    o_ref[0] = _dot(cs, w_ref[0].astype(BF16)) + b_ref[0]


def _ada_linear(c, w, b, tn=1024):
    L, K, N = w.shape
    M = c.shape[0]
    return pl.pallas_call(
        _ada_kernel,
        grid=(L, N // tn),
        in_specs=[pl.BlockSpec((M, K), lambda l, j: (0, 0)),
                  pl.BlockSpec((1, K, tn), lambda l, j: (l, 0, j)),
                  pl.BlockSpec((1, 1, tn), lambda l, j: (l, 0, j))],
        out_specs=pl.BlockSpec((1, M, tn), lambda l, j: (l, 0, j)),
        out_shape=jax.ShapeDtypeStruct((L, M, N), F32),
        compiler_params=_cparams(("parallel", "parallel")),
        name="ada_linear",
    )(c, w, b)


def _rope_tile(acc, cos, sin):
    pieces = []
    first_half = (lax.broadcasted_iota(jnp.int32, (1, LANES), 1) & (HEAD_DIM - 1)) < (HEAD_DIM // 2)
    for c in range(acc.shape[1] // LANES):
        xc = acc[:, c * LANES:(c + 1) * LANES]
        partner = jnp.where(first_half,
                            pltpu.roll(xc, LANES - HEAD_DIM // 2, 1),
                            pltpu.roll(xc, HEAD_DIM // 2, 1))
        pieces.append(xc * cos + partner * sin)
    return jnp.concatenate(pieces, axis=1)


def _nmm_kernel(h_ref, g_ref, sh_ref, sc_ref, w_ref, cos_ref, sin_ref, o_ref, xn_ref, *, n_rope):
    j = pl.program_id(1)

    @pl.when(j == 0)
    def _():
        x = h_ref[...]
        ms = jnp.mean(x * x, axis=-1, keepdims=True)
        y = x * lax.rsqrt(ms + EPS) * g_ref[...]
        xn_ref[...] = (y * (1.0 + sc_ref[0]) + sh_ref[0]).astype(BF16)

    acc = _dot(xn_ref[...], w_ref[...])

    @pl.when(j < n_rope)
    def _():
        o_ref[...] = _rope_tile(acc, cos_ref[...], sin_ref[...]).astype(o_ref.dtype)

    @pl.when(j >= n_rope)
    def _():
        o_ref[...] = acc.astype(o_ref.dtype)


def _mod_spec(mk, rows, tm, arity=1):
    arr, k = mk
    nmod, rm, _ = arr.shape
    tiles_per_mod = rows // nmod // tm
    if arity == 1:
        return pl.BlockSpec((1, rm, D_MODEL), lambda i: (i // tiles_per_mod, 0, k))
    return pl.BlockSpec((1, rm, D_MODEL), lambda i, j: (i // tiles_per_mod, 0, k))


def _norm_mod_matmul(h, g, shift, scale, w, rope, *, tm, tn, out_dtype, n_rope):
    R, Dm = h.shape
    N = w.shape[1]
    tab_spec = pl.BlockSpec((tm, LANES), lambda i, j: (i, 0))
    return pl.pallas_call(
        functools.partial(_nmm_kernel, n_rope=n_rope),
        grid=(R // tm, N // tn),
        in_specs=[pl.BlockSpec((tm, Dm), lambda i, j: (i, 0)),
                  pl.BlockSpec((1, Dm), lambda i, j: (0, 0)),
                  _mod_spec(shift, R, tm, 2), _mod_spec(scale, R, tm, 2),
                  pl.BlockSpec((Dm, tn), lambda i, j: (0, j)),
                  tab_spec, tab_spec],
        out_specs=pl.BlockSpec((tm, tn), lambda i, j: (i, j)),
        out_shape=jax.ShapeDtypeStruct((R, N), out_dtype),
        scratch_shapes=[pltpu.VMEM((tm, Dm), BF16)],
        compiler_params=_cparams(("parallel", "arbitrary")),
        name="norm_mod_matmul",
    )(h, g, shift[0], scale[0], w, *rope)


def _nmm_t_kernel(h_ref, g_ref, sh_ref, sc_ref, wt_ref, cos_ref, sin_ref, ok_ref, ov_ref, xn_ref, *,
                  n_rope):
    j = pl.program_id(1)

    @pl.when(j == 0)
    def _():
        x = h_ref[...]
        ms = jnp.mean(x * x, axis=-1, keepdims=True)
        y = x * lax.rsqrt(ms + EPS) * g_ref[...]
        xn_ref[...] = (y * (1.0 + sc_ref[0]) + sh_ref[0]).astype(BF16)

    acc = _dot_nt(wt_ref[...], xn_ref[...])

    @pl.when(j < n_rope)
    def _():
        cos, sin = cos_ref[...], sin_ref[...]
        half = HEAD_DIM // 2
        pieces = []
        for hb in range(acc.shape[0] // HEAD_DIM):
            x = acc[hb * HEAD_DIM:(hb + 1) * HEAD_DIM]
            partner = jnp.concatenate([x[half:], x[:half]], axis=0)
            pieces.append(x * cos + partner * sin)
        ok_ref[0] = jnp.concatenate(pieces, axis=0)

    @pl.when(j >= n_rope)
    def _():
        ov_ref[0] = acc


def _norm_mod_matmul_t(h, g, shift, scale, wt, rope_t, *, nb, tm, tn, n_rope):
    R, Dm = h.shape
    N = wt.shape[0]
    cols = R // nb
    tiles_per_b = cols // tm
    nh = n_rope
    tab_spec = pl.BlockSpec((HEAD_DIM, tm), lambda i, j: (0, i % tiles_per_b))
    k_map = lambda i, j: (i // tiles_per_b, jnp.minimum(j, nh - 1), i % tiles_per_b)
    v_map = lambda i, j: (i // tiles_per_b, jnp.maximum(j - nh, 0), i % tiles_per_b)
    return pl.pallas_call(
        functools.partial(_nmm_t_kernel, n_rope=n_rope),
        grid=(R // tm, N // tn),
        in_specs=[pl.BlockSpec((tm, Dm), lambda i, j: (i, 0)),
                  pl.BlockSpec((1, Dm), lambda i, j: (0, 0)),
                  _mod_spec(shift, R, tm, 2), _mod_spec(scale, R, tm, 2),
                  pl.BlockSpec((tn, Dm), lambda i, j: (j, 0)),
                  tab_spec, tab_spec],
        out_specs=[pl.BlockSpec((1, tn, tm), k_map), pl.BlockSpec((1, tn, tm), v_map)],
        out_shape=[jax.ShapeDtypeStruct((nb, nh * tn, cols), F32),
                   jax.ShapeDtypeStruct((nb, N - nh * tn, cols), F32)],
        scratch_shapes=[pltpu.VMEM((tm, Dm), BF16)],
        compiler_params=_cparams(("parallel", "arbitrary")),
        name="norm_mod_matmul_t",
    )(h, g, shift[0], scale[0], wt, *rope_t)


def _nmm_split_kernel(*refs, n_rope, n_out, tm):
    h_ref, g_ref, sh_ref, sc_ref, w_ref, cos_ref, sin_ref = refs[:7]
    outs = refs[7:7 + n_out]
    xf_ref, xn_ref = refs[7 + n_out:]
    ng = len(DIL_PAIRS)
    W = ATT_OUT

    x = h_ref[...]
    ms = jnp.mean(x * x, axis=-1, keepdims=True)
    y = x * lax.rsqrt(ms + EPS) * g_ref[...]
    xn = y * (1.0 + sc_ref[0]) + sh_ref[0]
    nlc = xn.shape[1] // LANES
    for lc in range(nlc):
        xf_ref[lc] = xn[:, lc * LANES:(lc + 1) * LANES]
    for g, (win, d) in enumerate(DIL_PAIRS):
        rows = tm // d
        if d == 1:
            xn_ref[g] = xn.astype(BF16)
        else:
            for r in range(d):
                for lc in range(nlc):
                    xn_ref[g, r * rows:(r + 1) * rows, lc * LANES:(lc + 1) * LANES] = (
                        xf_ref.at[lc][pl.ds(r, rows, stride=d), :].astype(BF16))

    for k in range(n_out):
        g = k % ng
        acc = _dot(xn_ref[g], w_ref[:, k * W:(k + 1) * W])
        val = _rope_tile(acc, cos_ref[g], sin_ref[g]) if k < n_rope else acc
        d = DIL_PAIRS[g][1]
        rows = tm // d
        for r in range(d):
            outs[k][0, r] = val[r * rows:(r + 1) * rows].astype(outs[k].dtype)


def _split_rows(x, d, tm):
    T, C = x.shape
    return x.reshape(T // tm, tm // d, d, C).transpose(0, 2, 1, 3).reshape(T, C)


def _norm_mod_matmul_split(h, g, shift, scale, w, cos_t, sin_t, *, B, tm, n_rope):
    R, Dm = h.shape
    S = R // B
    W = ATT_OUT
    n_out = w.shape[1] // W
    seq_tiles = S // tm
    ng = len(DIL_PAIRS)
    cos_g = jnp.stack([_split_rows(cos_t, d, tm) for _, d in DIL_PAIRS])
    sin_g = jnp.stack([_split_rows(sin_t, d, tm) for _, d in DIL_PAIRS])
    tab_spec = pl.BlockSpec((ng, tm, LANES), lambda i: (0, i % seq_tiles, 0))
    out_specs, out_shape = [], []
    for k in range(n_out):
        d = DIL_PAIRS[k % ng][1]
        out_specs.append(pl.BlockSpec((1, d, tm // d, W),
                                      lambda i: (i // seq_tiles, 0, i % seq_tiles, 0)))
        out_shape.append(jax.ShapeDtypeStruct((B, d, S // d, W), BF16))
    return pl.pallas_call(
        functools.partial(_nmm_split_kernel, n_rope=n_rope, n_out=n_out, tm=tm),
        grid=(R // tm,),
        in_specs=[pl.BlockSpec((tm, Dm), lambda i: (i, 0)),
                  pl.BlockSpec((1, Dm), lambda i: (0, 0)),
                  _mod_spec(shift, R, tm), _mod_spec(scale, R, tm),
                  pl.BlockSpec((Dm, n_out * W), lambda i: (0, 0), pipeline_mode=pl.Buffered(1)),
                  tab_spec, tab_spec],
        out_specs=out_specs,
        out_shape=out_shape,
        scratch_shapes=[pltpu.VMEM((Dm // LANES, tm, LANES), F32), pltpu.VMEM((ng, tm, Dm), BF16)],
        compiler_params=_cparams(("parallel",), vmem_mb=56),
        name="norm_mod_matmul_split",
    )(h, g, shift[0], scale[0], w, cos_g, sin_g)


def _decay_terms(dt_raw, dtb, alog, valid_rows=None):
    Q = dt_raw.shape[0]
    dt = dt_raw + dtb
    dt = jnp.maximum(dt, 0.0) + jnp.log(1.0 + jnp.exp(-jnp.abs(dt)))
    if valid_rows is not None:
        dt = jnp.where(lax.broadcasted_iota(jnp.int32, (Q, 1), 0) < valid_rows, dt, 0.0)
    a = dt * (-jnp.exp(alog) * LOG2E)
    ri = lax.broadcasted_iota(jnp.int32, (Q, Q), 0)
    ci = lax.broadcasted_iota(jnp.int32, (Q, Q), 1)
    tril = jnp.where(ri >= ci, 1.0, 0.0).astype(F32)
    acum = jnp.dot(tril, a, preferred_element_type=F32, precision=lax.Precision.HIGHEST)
    return dt, acum


def _inproj_kernel(*refs, tm, seq_tiles, step_mode):
    h_ref, g_ref, sh_ref, sc_ref, w_ref, wdt_ref, cw_ref, cb_ref = refs[:8]
    k = 8
    if step_mode:
        pp_ref = refs[k]
        k += 1
    else:
        dtb_ref, alog_ref = refs[k:k + 2]
        k += 2
    zx_ref, dt_ref, u_ref = refs[k:k + 3]
    k += 3
    if not step_mode:
        ac_ref, act_ref, carry_ref, fix_ref = refs[k:k + 4]
    i = pl.program_id(0)
    KC = M_CONV - 1
    tc = INPROJ_TC

    x = h_ref[...]
    ms = jnp.mean(x * x, axis=-1, keepdims=True)
    xn = (x * lax.rsqrt(ms + EPS) * g_ref[...] * (1.0 + sc_ref[0]) + sh_ref[0]).astype(BF16)
    dt_raw = _dot(xn, wdt_ref[...])
    if step_mode:
        dt_ref[...] = dt_raw
    else:
        for ck in range(tm // SSD_CHUNK):
            rows = slice(ck * SSD_CHUNK, (ck + 1) * SSD_CHUNK)
            dt, acum = _decay_terms(dt_raw[rows], dtb_ref[...], alog_ref[...])
            dt_ref[rows, :] = dt
            ac_ref[rows, :] = acum
            act_ref[ck] = acum.T
    if step_mode:
        t = lax.broadcasted_iota(jnp.int32, (tm, 1), 0) & (SUBLANES - 1)
    else:
        @pl.when(i % seq_tiles == 0)
        def _():
            carry_ref[...] = jnp.zeros_like(carry_ref)

    nz, nx = M_D_INNER // tc, M_CONV_DIM // tc

    def xbc_dot(c):
        return _dot(xn, w_ref[:, M_D_INNER + c * tc:M_D_INNER + (c + 1) * tc])

    u_next = xbc_dot(0)
    for c in range(nx):
        cols = slice(c * tc, (c + 1) * tc)
        u = u_next
        if c + 1 < nx:
            u_next = xbc_dot(c + 1)
        for zc in range(c * nz // nx, (c + 1) * nz // nx):
            zx_ref[:, zc * tc:(zc + 1) * tc] = _dot(xn, w_ref[:, zc * tc:(zc + 1) * tc])
        w = [cw_ref[kk:kk + 1, cols] for kk in range(M_CONV)]
        b = cb_ref[:, cols]
        if step_mode:
            u_ref[:, cols] = u
            pp = pp_ref[:, cols]
            acc = u * w[KC] + b
            for s in range(1, M_CONV):
                term = jnp.where(t >= s, pltpu.roll(u, s, 0), pltpu.roll(pp, tm + s - SUBLANES, 0))
                acc = acc + term * w[KC - s]
        else:
            fix_ref[0:SUBLANES, :] = carry_ref[:, cols]
            fix_ref[SUBLANES:2 * SUBLANES, :] = u[0:SUBLANES]
            last = u[tm - SUBLANES:tm]
            carry_ref[:, cols] = last
            u_ref[0, :, cols] = last
            acc = u * w[KC] + b
            fix = fix_ref[SUBLANES:2 * SUBLANES, :] * w[KC] + b
            for s in range(1, M_CONV):
                acc = acc + pltpu.roll(u, s, 0) * w[KC - s]
                fix = fix + fix_ref[pl.ds(SUBLANES - s, SUBLANES), :] * w[KC - s]
            acc = jnp.concatenate([fix, acc[SUBLANES:]], axis=0)
        zx_ref[:, M_D_INNER + c * tc:M_D_INNER + (c + 1) * tc] = _silu(acc)


def _in_proj(h, g, shift, scale, w_zx, w_dt, cw, cb, dtb, alog, *, tm, seq_rows, prev=None):
    R, Dm = h.shape
    N = w_zx.shape[1]
    step_mode = prev is not None
    seq_tiles = 1 if step_mode else seq_rows // tm
    ntiles = R // tm
    once = pl.Buffered(1)
    const = lambda i: (0, 0)
    in_specs = [pl.BlockSpec((tm, Dm), lambda i: (i, 0)), pl.BlockSpec((1, Dm), const),
                _mod_spec(shift, R, tm), _mod_spec(scale, R, tm),
                pl.BlockSpec((Dm, N), const, pipeline_mode=once),
                pl.BlockSpec((Dm, LANES), const, pipeline_mode=once),
                pl.BlockSpec((M_CONV, M_CONV_DIM), const, pipeline_mode=once),
                pl.BlockSpec((1, M_CONV_DIM), const, pipeline_mode=once)]
    args = [h, g, shift[0], scale[0], w_zx, w_dt, cw, cb]
    scratch = []
    lane_spec = pl.BlockSpec((tm, LANES), lambda i: (i, 0))
    out_specs = [pl.BlockSpec((tm, N), lambda i: (i, 0)), lane_spec]
    out_shape = [jax.ShapeDtypeStruct((R, N), F32), jax.ShapeDtypeStruct((R, LANES), F32)]
    if step_mode:
        in_specs.append(pl.BlockSpec((tm, M_CONV_DIM), lambda i: (i, 0)))
        args.append(prev)
        out_specs.append(pl.BlockSpec((tm, M_CONV_DIM), lambda i: (i, 0)))
        out_shape.append(jax.ShapeDtypeStruct((R, M_CONV_DIM), F32))
    else:
        in_specs += [pl.BlockSpec((1, LANES), const)] * 2
        args += [dtb, alog]
        cpt = tm // SSD_CHUNK
        out_specs += [pl.BlockSpec((1, SUBLANES, M_CONV_DIM), lambda i: (i, 0, 0)), lane_spec,
                      pl.BlockSpec((cpt, SSD_CHUNK, LANES), lambda i: (i, 0, 0))]
        out_shape += [jax.ShapeDtypeStruct((ntiles, SUBLANES, M_CONV_DIM), F32),
                      jax.ShapeDtypeStruct((R, LANES), F32),
                      jax.ShapeDtypeStruct((R // SSD_CHUNK, SSD_CHUNK, LANES), F32)]
        scratch = [pltpu.VMEM((SUBLANES, M_CONV_DIM), F32),
                   pltpu.VMEM((2 * SUBLANES, INPROJ_TC), F32)]
    return pl.pallas_call(
        functools.partial(_inproj_kernel, tm=tm, seq_tiles=seq_tiles, step_mode=step_mode),
        grid=(ntiles,),
        in_specs=in_specs,
        out_specs=out_specs,
        out_shape=out_shape,
        scratch_shapes=scratch,
        compiler_params=_cparams(("arbitrary",), vmem_mb=56),
        name="in_proj",
    )(*args)


def _ssd_kernel(*refs, Q, Tv):
    pre = Tv == Q
    zx_ref, dt_ref = refs[:2]
    k = 2
    if pre:
        ac_ref, act_ref = refs[k:k + 2]
    else:
        dtb_ref, alog_ref = refs[k:k + 2]
    k += 2
    h0_ref, dskip_ref, nw_ref, wout_ref, h_ref, gate_ref = refs[k:k + 6]
    k += 6
    out_ref, hout_ref, hT_ref, ybuf_ref = refs[k:k + 4]
    k += 4
    c = pl.program_id(1)
    npair = M_HEADS // M_NGROUPS // 2

    @pl.when(c == 0)
    def _():
        for g in range(M_NGROUPS):
            for p in range(npair):
                h0 = g * 2 * npair + 2 * p
                pair = h0_ref[0, 0, h0:h0 + 2].reshape(2 * M_HEADDIM, M_D_STATE)
                hT_ref[g, :, p * LANES:(p + 1) * LANES] = pair.T

    if pre:
        zfull = None
        xact = zx_ref[0, :, M_D_INNER:]
        dt, acum, acum_t = dt_ref[0], ac_ref[0], act_ref[0]
    else:
        pad_ref, dtp_ref = refs[k:k + 2]
        pad_ref[...] = jnp.zeros_like(pad_ref)
        pad_ref[0:Tv, :] = zx_ref[0]
        dtp_ref[...] = jnp.zeros_like(dtp_ref)
        dtp_ref[0:Tv, :] = dt_ref[0]
        zfull = pad_ref[:, 0:M_D_INNER]
        xact = pad_ref[:, M_D_INNER:]
        dt, acum = _decay_terms(dtp_ref[...], dtb_ref[...], alog_ref[...], Tv)
        acum_t = acum.T
    causal = (lax.broadcasted_iota(jnp.int32, (Q, Q), 0)
              >= lax.broadcasted_iota(jnp.int32, (Q, Q), 1))
    lane_lo = lax.broadcasted_iota(jnp.int32, (1, LANES), 1) < M_HEADDIM

    groups = range(M_NGROUPS)
    keeps = (lane_lo, jnp.logical_not(lane_lo))
    c_gs, b_gts, cbs = [], [], []
    for g in groups:
        b_f = xact[:, M_D_INNER + g * M_D_STATE:M_D_INNER + (g + 1) * M_D_STATE]
        c_g = xact[:, M_D_INNER + M_GN + g * M_D_STATE:M_D_INNER + M_GN + (g + 1) * M_D_STATE].astype(BF16)
        c_gs.append(c_g)
        b_gts.append(b_f.T.astype(BF16))
        cbs.append(_dot_nt(c_g, b_f.astype(BF16)))
    xdts, es, xdecs = [], [], []
    for pi in range(M_HEADS // 2):
        h0 = 2 * pi
        xp = xact[:, h0 * M_HEADDIM:(h0 + 2) * M_HEADDIM]
        dtp = jnp.where(lane_lo, dt[:, h0:h0 + 1], dt[:, h0 + 1:h0 + 2])
        ap = jnp.where(lane_lo, acum[:, h0:h0 + 1], acum[:, h0 + 1:h0 + 2])
        xdt = xp * dtp
        xdts.append(xdt)
        es.append(jnp.exp2(ap))
        xdecs.append((xdt * jnp.exp2(ap[Q - 1:Q, :] - ap)).astype(BF16))
    mats = []
    for hh in range(M_HEADS):
        seg = acum[:, hh:hh + 1] - acum_t[hh:hh + 1, :]
        lm = jnp.exp2(jnp.where(causal, seg, NEG_INF))
        mats.append((cbs[hh // (2 * npair)] * lm).astype(BF16))
    ydiag = []
    for pi in range(M_HEADS // 2):
        x0 = jnp.where(keeps[0], xdts[pi], 0.0).astype(BF16)
        x1 = jnp.where(keeps[1], xdts[pi], 0.0).astype(BF16)
        if Q % LANES == 0:
            lhs = jnp.concatenate([mats[2 * pi], mats[2 * pi + 1]], axis=1)
            ydiag.append(_dot(lhs, jnp.concatenate([x0, x1], axis=0)))
        else:
            ydiag.append(_dot(mats[2 * pi], x0) + _dot(mats[2 * pi + 1], x1))
    y_offs = []
    for g in groups:
        e_g = jnp.concatenate(es[g * npair:(g + 1) * npair], axis=1)
        xdec_g = jnp.concatenate(xdecs[g * npair:(g + 1) * npair], axis=1)
        h_prev = hT_ref[g]
        y_offs.append(_dot(c_gs[g], h_prev.astype(BF16)) * e_g)
        hT_ref[g] = h_prev * e_g[Q - 1:Q, :] + _dot(b_gts[g], xdec_g)
    for g in groups:
        gs = slice(g * M_GN, (g + 1) * M_GN)
        y = (jnp.concatenate(ydiag[g * npair:(g + 1) * npair], axis=1) + y_offs[g]
             + xact[:, gs] * dskip_ref[:, gs])
        yg = y * _silu(zx_ref[0, :, gs] if zfull is None else zfull[:, gs])
        ms = jnp.mean(yg * yg, axis=-1, keepdims=True)
        ybuf_ref[:, gs] = (yg * lax.rsqrt(ms + EPS) * nw_ref[:, gs]).astype(BF16)

    tb = -(-Tv // BF16_ROWS) * BF16_ROWS
    out_ref[0] = h_ref[0] + gate_ref[0] * _dot(ybuf_ref[0:tb, :], wout_ref[...])[0:Tv]

    @pl.when(c == pl.num_programs(1) - 1)
    def _():
        for g in range(M_NGROUPS):
            for p in range(npair):
                h0 = g * 2 * npair + 2 * p
                pair = hT_ref[g, :, p * LANES:(p + 1) * LANES].T
                hout_ref[0, h0:h0 + 2] = pair.reshape(2, M_HEADDIM, M_D_STATE)


def _ssd_mixer(zx, dtr, decay, h0_all, layer, dskip, nw, w_out, h, gate):
    B, L, W = zx.shape
    Dm = h.shape[-1]
    Q = SSD_CHUNK if L >= SSD_CHUNK else SSD_SHORT_CHUNK
    Tv = min(L, Q)
    nc = L // Tv
    const2 = lambda b, c: (0, 0)
    row3 = lambda b, c: (b, c, 0)
    state_spec = pl.BlockSpec((1, 1, M_HEADS, M_HEADDIM, M_D_STATE), lambda b, c: (layer, b, 0, 0, 0))
    scratch = [pltpu.VMEM((M_NGROUPS, M_D_STATE, M_GN), F32), pltpu.VMEM((Q, M_D_INNER), BF16)]
    if Tv == Q:
        decay_specs = [pl.BlockSpec((1, Q, LANES), row3),
                       pl.BlockSpec((1, Q, LANES), lambda b, c: (b * nc + c, 0, 0))]
    else:
        decay_specs = [pl.BlockSpec((1, LANES), const2)] * 2
        scratch += [pltpu.VMEM((Q, W), F32), pltpu.VMEM((Q, LANES), F32)]
    return pl.pallas_call(
        functools.partial(_ssd_kernel, Q=Q, Tv=Tv),
        grid=(B, nc),
        in_specs=[pl.BlockSpec((1, Tv, W), row3),
                  pl.BlockSpec((1, Tv, LANES), row3)]
        + decay_specs
        + [state_spec,
           pl.BlockSpec((1, M_D_INNER), const2),
           pl.BlockSpec((1, M_D_INNER), const2),
           pl.BlockSpec((M_D_INNER, Dm), const2, pipeline_mode=pl.Buffered(1)),
           pl.BlockSpec((1, Tv, Dm), row3),
           _mod_spec(gate, B, 1, 2)],
        out_specs=[pl.BlockSpec((1, Tv, Dm), row3),
                   pl.BlockSpec((1, M_HEADS, M_HEADDIM, M_D_STATE), lambda b, c: (b, 0, 0, 0))],
        out_shape=[jax.ShapeDtypeStruct((B, L, Dm), F32),
                   jax.ShapeDtypeStruct(h0_all.shape[1:], F32)],
        scratch_shapes=scratch,
        compiler_params=_cparams(("parallel", "arbitrary")),
        name="ssd_mixer",
    )(zx, dtr, *decay, h0_all, dskip, nw, w_out, h, gate[0])


def _proj_res_kernel(a_ref, w_ref, h_ref, gate_ref, o_ref):
    o_ref[...] = h_ref[...] + gate_ref[0] * _dot(a_ref[...], w_ref[...])


def _proj_residual(a, w, h, gate, *, tm):
    R, K = a.shape
    Dm = w.shape[1]
    return pl.pallas_call(
        _proj_res_kernel,
        grid=(R // tm,),
        in_specs=[pl.BlockSpec((tm, K), lambda i: (i, 0)),
                  pl.BlockSpec((K, Dm), lambda i: (0, 0)),
                  pl.BlockSpec((tm, Dm), lambda i: (i, 0)),
                  _mod_spec(gate, R, tm)],
        out_specs=pl.BlockSpec((tm, Dm), lambda i: (i, 0)),
        out_shape=jax.ShapeDtypeStruct((R, Dm), F32),
        compiler_params=_cparams(("parallel",)),
        name="proj_residual",
    )(a, w, h, gate[0])


def _attn_out_kernel(o0_ref, o1_ref, o2_ref, l0_ref, l1_ref, l2_ref, e_ref, w_ref, h_ref,
                     gate_ref, out_ref, nat_o, nat_l, *, tm):
    nlc = ATT_OUT // LANES
    assert DIL_PAIRS[0][1] == 1
    for g, (o_ref, l_ref) in enumerate(((o0_ref, l0_ref), (o1_ref, l1_ref), (o2_ref, l2_ref))):
        d = DIL_PAIRS[g][1]
        rows = tm // d
        if d == 1:
            continue
        for r in range(d):
            sl = pl.ds(r, rows, stride=d)
            nat_l.at[g][sl, :] = l_ref[0, r]
            for lc in range(nlc):
                nat_o.at[g * nlc + lc][sl, :] = o_ref[0, r, lc]
    l0, l1, l2 = l0_ref[0, 0], nat_l[1], nat_l[2]
    mx = jnp.maximum(jnp.maximum(l0, l1), l2)
    w0, w1, w2 = jnp.exp2(l0 - mx), jnp.exp2(l1 - mx), jnp.exp2(l2 - mx)
    inv = 1.0 / (w0 + w1 + w2)
    alphas = []
    for wg in (w0, w1, w2):
        a = wg * inv
        hi = a.astype(BF16)
        lo = (a - hi.astype(F32)).astype(BF16)
        alphas.append(_dot(hi, e_ref[...]) + _dot(lo, e_ref[...]))
    pieces = []
    for lc in range(nlc):
        ls = slice(lc * LANES, (lc + 1) * LANES)
        t = alphas[0][:, ls] * o0_ref[0, 0, lc]
        for g in range(1, len(DIL_PAIRS)):
            t = t + alphas[g][:, ls] * nat_o[g * nlc + lc]
        pieces.append(t.astype(BF16))
    comb = jnp.concatenate(pieces, axis=1)
    out_ref[...] = h_ref[...] + gate_ref[0] * _dot(comb, w_ref[...])


def _attn_out(o_parts, lse_parts, expand, w, h, gate, *, B, tm):
    R, Dm = h.shape
    seq_tiles = R // B // tm
    nlc = ATT_OUT // LANES
    ng = len(DIL_PAIRS)
    row = lambda i: (i, 0)
    o_specs = [pl.BlockSpec((1, d, nlc, tm // d, LANES),
                            lambda i: (i // seq_tiles, 0, 0, i % seq_tiles, 0)) for _, d in DIL_PAIRS]
    l_specs = [pl.BlockSpec((1, d, tm // d, LANES),
                            lambda i: (i // seq_tiles, 0, i % seq_tiles, 0)) for _, d in DIL_PAIRS]
    return pl.pallas_call(
        functools.partial(_attn_out_kernel, tm=tm),
        grid=(R // tm,),
        in_specs=o_specs + l_specs
        + [pl.BlockSpec((LANES, ATT_OUT), lambda i: (0, 0)),
           pl.BlockSpec((ATT_OUT, Dm), lambda i: (0, 0)),
           pl.BlockSpec((tm, Dm), row),
           _mod_spec(gate, R, tm)],
        out_specs=pl.BlockSpec((tm, Dm), row),
        out_shape=jax.ShapeDtypeStruct((R, Dm), F32),
        scratch_shapes=[pltpu.VMEM((ng * nlc, tm, LANES), F32), pltpu.VMEM((ng, tm, LANES), F32)],
        compiler_params=_cparams(("parallel",)),
        name="attn_out",
    )(*o_parts, *lse_parts, expand, w, h, gate[0])


def _ffn_kernel(*refs, tm, seq_tiles, step_mode, has_final):
    h_ref, g_ref, sh_ref, sc_ref, gate_ref, wup_ref, cw_ref, cb_ref, wd_ref = refs[:9]
    k = 9
    if step_mode:
        pp_ref = refs[k]
        k += 1
    if has_final:
        fg_ref = refs[k]
        k += 1
    out_ref, u_ref, act_ref = refs[k:k + 3]
    k += 3
    if not step_mode:
        carry_ref, fix_ref = refs[k:k + 2]
    i = pl.program_id(0)
    KC = FFN_CONV - 1
    tc = FFN_TC

    x = h_ref[...]
    ms = jnp.mean(x * x, axis=-1, keepdims=True)
    xn = (x * lax.rsqrt(ms + EPS) * g_ref[...] * (1.0 + sc_ref[0]) + sh_ref[0]).astype(BF16)

    if step_mode:
        t = lax.broadcasted_iota(jnp.int32, (tm, 1), 0) & (SUBLANES - 1)
    else:
        @pl.when(i % seq_tiles == 0)
        def _():
            carry_ref[...] = jnp.zeros_like(carry_ref)

    def up(c):
        return (_dot(xn, wup_ref[:, c * tc:(c + 1) * tc]),
                _dot(xn, wup_ref[:, D_FF + c * tc:D_FF + (c + 1) * tc]))

    def conv(u, cols, part):
        w = [cw_ref[kk:kk + 1, cols] for kk in range(FFN_CONV)]
        b = cb_ref[:, cols]
        if step_mode:
            u_ref[:, cols] = u
            pp = pp_ref[:, cols]
            acc = u * w[KC]
            for s in range(1, FFN_CONV):
                term = jnp.where(t >= s, pltpu.roll(u, s, 0), pltpu.roll(pp, tm + s - SUBLANES, 0))
                acc = acc + term * w[KC - s]
            return acc + b
        fix_ref[part, 0:SUBLANES, :] = carry_ref[:, cols]
        fix_ref[part, SUBLANES:2 * SUBLANES, :] = u[0:SUBLANES]
        last = u[tm - SUBLANES:tm]
        carry_ref[:, cols] = last
        u_ref[0, :, cols] = last
        acc = u * w[KC] + b
        fix = fix_ref[part, SUBLANES:2 * SUBLANES, :] * w[KC] + b
        for s in range(1, FFN_CONV):
            acc = acc + pltpu.roll(u, s, 0) * w[KC - s]
            fix = fix + fix_ref[part, pl.ds(SUBLANES - s, SUBLANES), :] * w[KC - s]
        return jnp.concatenate([fix, acc[SUBLANES:]], axis=0)

    for c in range(D_FF // tc):
        ug, uv = up(c)
        cg = conv(ug, slice(c * tc, (c + 1) * tc), 0)
        cv = conv(uv, slice(D_FF + c * tc, D_FF + (c + 1) * tc), 1)
        act_ref[:, c * tc:(c + 1) * tc] = (_silu(cg) * cv).astype(BF16)

    hn = x + gate_ref[0] * _dot(act_ref[...], wd_ref[...])
    if has_final:
        ms = jnp.mean(hn * hn, axis=-1, keepdims=True)
        hn = hn * lax.rsqrt(ms + EPS) * fg_ref[...]
    out_ref[...] = hn


def _conv_ffn(h, g, shift, scale, gate, w_up, cw, cb, w_down, *, tm, seq_rows,
              prev=None, final_g=None):
    R, Dm = h.shape
    step_mode = prev is not None
    seq_tiles = 1 if step_mode else seq_rows // tm
    ntiles = R // tm
    once = pl.Buffered(1)
    const = lambda i: (0, 0)
    row_spec = pl.BlockSpec((tm, Dm), lambda i: (i, 0))
    in_specs = [row_spec, pl.BlockSpec((1, Dm), const),
                _mod_spec(shift, R, tm), _mod_spec(scale, R, tm), _mod_spec(gate, R, tm),
                pl.BlockSpec((Dm, 2 * D_FF), const, pipeline_mode=once),
                pl.BlockSpec((FFN_CONV, 2 * D_FF), const, pipeline_mode=once),
                pl.BlockSpec((1, 2 * D_FF), const, pipeline_mode=once),
                pl.BlockSpec((D_FF, Dm), const, pipeline_mode=once)]
    args = [h, g, shift[0], scale[0], gate[0], w_up, cw, cb, w_down]
    scratch = [pltpu.VMEM((tm, D_FF), BF16)]
    if step_mode:
        in_specs.append(pl.BlockSpec((tm, 2 * D_FF), lambda i: (i, 0)))
        args.append(prev)
        u_spec = pl.BlockSpec((tm, 2 * D_FF), lambda i: (i, 0))
        u_shape = jax.ShapeDtypeStruct((R, 2 * D_FF), F32)
    else:
        u_spec = pl.BlockSpec((1, SUBLANES, 2 * D_FF), lambda i: (i, 0, 0))
        u_shape = jax.ShapeDtypeStruct((ntiles, SUBLANES, 2 * D_FF), F32)
        scratch += [pltpu.VMEM((SUBLANES, 2 * D_FF), F32),
                    pltpu.VMEM((2, 2 * SUBLANES, FFN_TC), F32)]
    if final_g is not None:
        in_specs.append(pl.BlockSpec((1, Dm), const))
        args.append(final_g)
    return pl.pallas_call(
        functools.partial(_ffn_kernel, tm=tm, seq_tiles=seq_tiles,
                          step_mode=step_mode, has_final=final_g is not None),
        grid=(ntiles,),
        in_specs=in_specs,
        out_specs=[row_spec, u_spec],
        out_shape=[jax.ShapeDtypeStruct((R, Dm), F32), u_shape],
        scratch_shapes=scratch,
        compiler_params=_cparams(("arbitrary",), vmem_mb=56),
        name="conv_ffn",
    )(*args)


def _band_attn_kernel(q_ref, kc_ref, kp_ref, vc_ref, vp_ref, bias_ref, o_ref, lse_ref, bias0_ref, *, nbk):
    c = pl.program_id(2)
    BL = ATT_BLOCK
    lane = lax.broadcasted_iota(jnp.int32, (1, LANES), 1)
    bias0_ref[...] = jnp.where(c > 0, bias_ref[0], NEG_INF)
    lane_lo = lane < HEAD_DIM
    keeps = (lane_lo, jnp.logical_not(lane_lo))
    zero = jnp.zeros((BL, LANES), BF16)
    units = [(sb, h) for sb in range(nbk) for h in range(HEADS_PER_GROUP)]

    def kv_prev(cur_ref, prev_ref, sb, ps):
        if sb == 0:
            return prev_ref[0, 0, :, ps]
        return cur_ref[0, 0, (sb - 1) * BL:sb * BL, ps]

    scores = {}
    for sb, h in units:
        ps = slice((h // 2) * LANES, (h // 2 + 1) * LANES)
        rows = slice(sb * BL, (sb + 1) * BL)
        qh = jnp.where(keeps[h % 2], q_ref[0, 0, rows, ps], zero)
        s_p = _dot_nt(qh, kv_prev(kc_ref, kp_ref, sb, ps)) + (bias0_ref[...] if sb == 0 else bias_ref[0])
        s_c = _dot_nt(qh, kc_ref[0, 0, rows, ps]) + bias_ref[1]
        scores[sb, h] = (s_p, s_c)
    probs = {}
    for sb in range(nbk):
        lse = jnp.zeros((BL, LANES), F32)
        for h in range(HEADS_PER_GROUP):
            s_p, s_c = scores[sb, h]
            m = jnp.max(jnp.maximum(s_p, s_c), axis=-1, keepdims=True)
            p_p = jnp.exp2(s_p - m)
            p_c = jnp.exp2(s_c - m)
            l = jnp.sum(p_p + p_c, axis=-1, keepdims=True)
            lse = jnp.where(lane == h, m + jnp.log(l) * LOG2E, lse)
            probs[sb, h] = (p_p.astype(BF16), p_c.astype(BF16), 1.0 / l)
        lse_ref[0, 0, sb * BL:(sb + 1) * BL, :] = lse
    for sb in range(nbk):
        rows = slice(sb * BL, (sb + 1) * BL)
        for p in range(HEADS_PER_GROUP // 2):
            ps = slice(p * LANES, (p + 1) * LANES)
            vc, vp = vc_ref[0, 0, rows, ps], kv_prev(vc_ref, vp_ref, sb, ps)
            o_pair = None
            for hh in range(2):
                p_p, p_c, _ = probs[sb, 2 * p + hh]
                o = _dot(p_p, jnp.where(keeps[hh], vp, zero)) + _dot(p_c, jnp.where(keeps[hh], vc, zero))
                o_pair = o if o_pair is None else o_pair + o
            scale = jnp.where(lane_lo, probs[sb, 2 * p][2], probs[sb, 2 * p + 1][2])
            o_ref[0, 0, p, rows, :] = o_pair * scale


def _band_attention(q, k, v):
    B, d, n, W = q.shape
    nbk = math.gcd(BAND_BLOCKS, n // ATT_BLOCK)
    blk = nbk * ATT_BLOCK
    cur = pl.BlockSpec((1, 1, blk, W), lambda b, r, c: (b, r, c, 0))
    prv = pl.BlockSpec((1, 1, ATT_BLOCK, W), lambda b, r, c: (b, r, jnp.maximum(c * nbk - 1, 0), 0))
    i = jnp.arange(ATT_BLOCK)[:, None]
    j = jnp.arange(ATT_BLOCK)[None, :]
    bias = jnp.where(jnp.stack([j >= i, j <= i]), 0.0, NEG_INF).astype(F32)
    return pl.pallas_call(
        functools.partial(_band_attn_kernel, nbk=nbk),
        grid=(B, d, n // blk),
        in_specs=[cur, cur, prv, cur, prv,
                  pl.BlockSpec((2, ATT_BLOCK, ATT_BLOCK), lambda b, r, c: (0, 0, 0))],
        scratch_shapes=[pltpu.VMEM((ATT_BLOCK, ATT_BLOCK), F32)],
        out_specs=[pl.BlockSpec((1, 1, W // LANES, blk, LANES), lambda b, r, c: (b, r, 0, c, 0)),
                   pl.BlockSpec((1, 1, blk, LANES), lambda b, r, c: (b, r, c, 0))],
        out_shape=[jax.ShapeDtypeStruct((B, d, W // LANES, n, LANES), F32),
                   jax.ShapeDtypeStruct((B, d, n, LANES), F32)],
        compiler_params=_cparams(("parallel", "parallel", "arbitrary")),
        name="band_attention",
    )(q, k, k, v, v, bias)


def _dec_attn_kernel(q_ref, kn_ref, vn_ref, k0_ref, v0_ref, k1_ref, v1_ref, k2_ref, v2_ref,
                     o_ref, *, T, HB):
    s = pl.program_id(1)
    caches = ((k0_ref, v0_ref), (k1_ref, v1_ref), (k2_ref, v2_ref))
    t = lax.broadcasted_iota(jnp.int32, (T, 1), 0)
    off = (pl.program_id(0) % (LANES // T)) * T
    u = lax.broadcasted_iota(jnp.int32, (1, LANES), 1) - off
    masks = []
    for g, (win, dil) in enumerate(DIL_PAIRS):
        p = lax.broadcasted_iota(jnp.int32, (1, win), 1)
        dist = win + t - p
        cmask = (dist <= win) & ((dist & (dil - 1)) == 0)
        nmask = (u >= 0) & (u <= t) & (((t - u) & (dil - 1)) == 0)
        masks.append((cmask, nmask))
    units = [(hh, g) for hh in range(HB) for g in range(len(DIL_PAIRS))]
    scores = {}
    for hh, g in units:
        head = g * HEADS_PER_GROUP + s * HB + hh
        cmask, nmask = masks[g]
        q = q_ref[0, head].astype(BF16)
        sc = jnp.where(cmask, _dot(q, caches[g][0][0, hh].astype(BF16)), NEG_INF)
        sn = jnp.where(nmask, _dot(q, kn_ref[head].astype(BF16)), NEG_INF)
        scores[hh, g] = (sc, sn)
    probs = {}
    for hh, g in units:
        sc, sn = scores[hh, g]
        m = jnp.maximum(jnp.max(sc, axis=-1, keepdims=True), jnp.max(sn, axis=-1, keepdims=True))
        pc = jnp.exp2(sc - m)
        pn = jnp.exp2(sn - m)
        l = jnp.sum(pc, axis=-1, keepdims=True) + jnp.sum(pn, axis=-1, keepdims=True)
        probs[hh, g] = (m, l, pc.astype(BF16), pn.astype(BF16))
    outs = {}
    for hh, g in units:
        head = g * HEADS_PER_GROUP + s * HB + hh
        m, l, pc, pn = probs[hh, g]
        outs[hh, g] = (_dot_nt(pc, caches[g][1][0, hh].astype(BF16))
                       + _dot_nt(pn, vn_ref[head].astype(BF16)))
    for hh in range(HB):
        ms = [probs[hh, g][0] for g in range(len(DIL_PAIRS))]
        mx = jnp.maximum(jnp.maximum(ms[0], ms[1]), ms[2])
        num = None
        den = None
        for g in range(len(DIL_PAIRS)):
            w = jnp.exp2(ms[g] - mx)
            num = w * outs[hh, g] if num is None else num + w * outs[hh, g]
            den = w * probs[hh, g][1] if den is None else den + w * probs[hh, g][1]
        o_ref[0, hh] = num / den


def _decode_attention(qh, knt, vnt, cache_kt, cache_vt):
    B, H, T, E = qh.shape
    per_tile = LANES // T
    P = cache_kt.shape[-1]
    HB = 8
    nhb = HEADS_PER_GROUP // HB
    specs = []
    for g, (win, dil) in enumerate(DIL_PAIRS):
        imap = functools.partial(lambda b, s, g, last: (b, g * nhb + s, 0, last),
                                 g=g, last=P // win - 1)
        specs += [pl.BlockSpec((1, HB, E, win), imap)] * 2
    full_q = pl.BlockSpec((1, H, T, E), lambda b, s: (b, 0, 0, 0))
    full_n = pl.BlockSpec((H, E, LANES), lambda b, s: (0, 0, b // per_tile))
    return pl.pallas_call(
        functools.partial(_dec_attn_kernel, T=T, HB=HB),
        grid=(B, nhb),
        in_specs=[full_q, full_n, full_n] + specs,
        out_specs=pl.BlockSpec((1, HB, T, E), lambda b, s: (b, s, 0, 0)),
        out_shape=jax.ShapeDtypeStruct((B, HEADS_PER_GROUP, T, E), F32),
        compiler_params=_cparams(("parallel", "arbitrary"), vmem_mb=56),
        name="decode_attention",
    )(qh, knt, vnt, cache_kt, cache_vt, cache_kt, cache_vt, cache_kt, cache_vt)


def _rope_tables(pos):
    half = HEAD_DIM // 2
    inv = ROPE_THETA ** (-jnp.arange(half, dtype=F32) * (2.0 / HEAD_DIM))
    ang = pos.astype(F32)[:, None] * inv[None, :]
    cos, sin = jnp.cos(ang), jnp.sin(ang)
    cos_t = jnp.concatenate([cos, cos, cos, cos], axis=1)
    sin_t = jnp.concatenate([-sin, sin, -sin, sin], axis=1)
    return cos_t, sin_t


def _prep_params(p):
    w = {}
    w_in = p['m_w_in']
    w['w_in_zx'] = w_in[:, :, :M_D_INNER + M_CONV_DIM].astype(BF16)
    w['w_in_dt'] = jnp.pad(w_in[:, :, M_D_INNER + M_CONV_DIM:],
                           ((0, 0), (0, 0), (0, LANES - M_HEADS))).astype(BF16)
    w['dt_bias'] = jnp.pad(p['m_dt_bias'], ((0, 0), (0, LANES - M_HEADS)))[:, None, :]
    w['a_log'] = jnp.pad(p['m_A_log'], ((0, 0), (0, LANES - M_HEADS)))[:, None, :]
    w['d_skip'] = jnp.repeat(p['m_D'], M_HEADDIM, axis=1)[:, None, :]
    w['m_norm'] = p['m_norm'][:, None, :]
    w['conv_w'] = p['m_conv_w']
    w['conv_b'] = p['m_conv_b'][:, None, :]
    w['w_out'] = p['m_w_out'].astype(BF16)
    w['w_q'] = (p['w_q'] * (HEAD_DIM ** -0.5 * LOG2E)).astype(BF16)
    w['w_kv'] = p['w_kv'].astype(BF16)
    w['w_kv_t'] = p['w_kv'].T.astype(BF16)
    w['w_o'] = p['w_o'].astype(BF16)
    w['ffn_up'] = p['ffn_w_up'].astype(BF16)
    w['ffn_down'] = p['ffn_w_down'].astype(BF16)
    w['ffn_cw'] = p['ffn_conv_w']
    w['ffn_cb'] = p['ffn_conv_b'][:, None, :]
    head = jnp.arange(ATT_OUT) // HEAD_DIM
    w['expand'] = (jnp.arange(LANES)[:, None] == head[None, :]).astype(BF16)
    return w


def _trunk(x, mods, kvmod, gate_b, pos, ssm0, conv0, ffn0, kv_past, p, w, *, tm, step):
    B, T, Dm = x.shape
    R = B * T
    h = x.reshape(R, Dm)
    cos_t, sin_t = _rope_tables(pos)
    cos_r = jnp.tile(cos_t, (B, 1))
    sin_r = jnp.tile(sin_t, (B, 1))
    ssm_out, conv_out, ffn_out = [], [], []
    k_new = v_new = new_t = kv_split = None
    for i in range(DEPTH):
        sh1, sc1, g1, sh2, sc2, g2 = [(mods[i], k) for k in range(6)]
        if i < N_A:
            tmi = min(tm, 512)
            inproj = functools.partial(_in_proj, h, p['norm_mix'][i][None], sh1, sc1, w['w_in_zx'][i],
                                       w['w_in_dt'][i], w['conv_w'][i], w['conv_b'][i],
                                       w['dt_bias'][i], w['a_log'][i], tm=tmi, seq_rows=T)
            if step:
                prev = jnp.pad(conv0[i], ((0, 0), (SUBLANES - (M_CONV - 1), 0), (0, 0)))
                zx, dtr, u = inproj(prev=prev.reshape(R, M_CONV_DIM))
                conv_out.append(u.reshape(B, T, M_CONV_DIM)[:, T - (M_CONV - 1):])
                decay = (w['dt_bias'][i], w['a_log'][i])
            else:
                zx, dtr, u, acum, acum_t = inproj()
                seq_tiles = T // tmi
                conv_out.append(u[seq_tiles - 1::seq_tiles, SUBLANES - (M_CONV - 1):])
                decay = (acum.reshape(B, T, LANES), acum_t)
            h3, h_t = _ssd_mixer(zx.reshape(B, T, -1), dtr.reshape(B, T, LANES), decay, ssm0, i,
                                 w['d_skip'][i], w['m_norm'][i], w['w_out'][i],
                                 h.reshape(B, T, Dm), gate_b[i])
            ssm_out.append(h_t)
            h = h3.reshape(R, Dm)
        else:
            jb = i - N_A
            if step:
                q = _norm_mod_matmul(h, p['norm_mix'][i][None], sh1, sc1, w['w_q'][jb],
                                     (cos_r, sin_r), tm=tm, tn=1024, out_dtype=F32,
                                     n_rope=ATT_WIDTH // 1024)
                qh = q.reshape(B, T, ATT_HEADS, HEAD_DIM).transpose(0, 2, 1, 3)
                o = _decode_attention(qh, *new_t, *kv_past)
                o = o.transpose(0, 2, 1, 3).reshape(R, ATT_OUT).astype(BF16)
                h = _proj_residual(o, w['w_o'][jb], h, g1, tm=tm)
            else:
                q_split = _norm_mod_matmul_split(h, p['norm_mix'][i][None], sh1, sc1, w['w_q'][jb],
                                                 cos_t, sin_t, B=B, tm=SPLIT_TM, n_rope=len(DIL_PAIRS))
                o_parts, lse_parts = [], []
                for g in range(len(DIL_PAIRS)):
                    og, lg = _band_attention(q_split[g], *kv_split[g])
                    o_parts.append(og)
                    lse_parts.append(lg)
                h = _attn_out(o_parts, lse_parts, w['expand'], w['w_o'][jb], h, g1, B=B, tm=SPLIT_TM)
        last = i == DEPTH - 1
        fin = p['final_norm'][None] if last else None
        if step:
            prev = jnp.pad(ffn0[i], ((0, 0), (SUBLANES - (FFN_CONV - 1), 0), (0, 0))).reshape(R, 2 * D_FF)
            h, u = _conv_ffn(h, p['norm_ffn'][i][None], sh2, sc2, g2, w['ffn_up'][i], w['ffn_cw'][i],
                             w['ffn_cb'][i], w['ffn_down'][i], tm=tm, seq_rows=T, prev=prev, final_g=fin)
            ffn_out.append(u.reshape(B, T, 2 * D_FF)[:, T - (FFN_CONV - 1):])
        else:
            tmf = min(tm, 512)
            h, u = _conv_ffn(h, p['norm_ffn'][i][None], sh2, sc2, g2, w['ffn_up'][i], w['ffn_cw'][i],
                             w['ffn_cb'][i], w['ffn_down'][i], tm=tmf, seq_rows=T, final_g=fin)
            seq_tiles = T // tmf
            ffn_out.append(u[seq_tiles - 1::seq_tiles, SUBLANES - (FFN_CONV - 1):])
        if i == N_A - 1:
            ksh, ksc = (kvmod, 0), (kvmod, 1)
            keep = min(WINDOW_MAX, T)
            if step:
                h_tail, nb_k, tm_k = h, 1, tm
                rope_k = (jnp.tile(cos_t[:, :HEAD_DIM].T, (1, B)), jnp.tile(sin_t[:, :HEAD_DIM].T, (1, B)))
            else:
                ng = len(DIL_PAIRS)
                kvs = _norm_mod_matmul_split(h, p['kv_norm'][None], ksh, ksc, w['w_kv'], cos_t, sin_t,
                                             B=B, tm=SPLIT_TM, n_rope=ng)
                kv_split = [(kvs[g], kvs[ng + g]) for g in range(ng)]
                h_tail = h.reshape(B, T, Dm)[:, T - keep:].reshape(B * keep, Dm)
                nb_k, tm_k = B, min(tm, keep)
                rope_k = (cos_t[T - keep:, :HEAD_DIM].T, sin_t[T - keep:, :HEAD_DIM].T)
            kvt = _norm_mod_matmul_t(h_tail, p['kv_norm'][None], ksh, ksc, w['w_kv_t'], rope_k,
                                     nb=nb_k, tm=tm_k, tn=1024, n_rope=ATT_WIDTH // 1024)
            if step:
                kt, vt = kvt[0][0], kvt[1][0]
                new_t = [a.reshape(ATT_HEADS, HEAD_DIM, R) for a in (kt, vt)]
                k_new, v_new = [a.reshape(ATT_HEADS, HEAD_DIM, B, T).transpose(2, 3, 0, 1)
                                for a in (kt, vt)]
            else:
                k_new, v_new = [a.reshape(B, ATT_HEADS, HEAD_DIM, keep).transpose(0, 3, 1, 2)
                                for a in (kvt[0], kvt[1])]
    return (h.reshape(B, T, Dm), jnp.stack(ssm_out), jnp.stack(conv_out), jnp.stack(ffn_out),
            k_new, v_new)


def kernel(x_prompt, x_sample, state_ssm, state_conv, state_ffn_conv, cache_k, cache_v, c_prompt, c_sample, ada_w, ada_b, norm_mix, norm_ffn, m_w_in, m_conv_w, m_conv_b, m_dt_bias, m_A_log, m_D, m_norm, m_w_out, kv_norm, kv_ada_w, kv_ada_b, w_kv, w_q, w_o, ffn_w_up, ffn_conv_w, ffn_conv_b, ffn_w_down, final_norm):
    p = dict(norm_mix=norm_mix, norm_ffn=norm_ffn, m_w_in=m_w_in, m_conv_w=m_conv_w,
             m_conv_b=m_conv_b, m_dt_bias=m_dt_bias, m_A_log=m_A_log, m_D=m_D, m_norm=m_norm,
             m_w_out=m_w_out, kv_norm=kv_norm, w_kv=w_kv, w_q=w_q, w_o=w_o, ffn_w_up=ffn_w_up,
             ffn_conv_w=ffn_conv_w, ffn_conv_b=ffn_conv_b, ffn_w_down=ffn_w_down,
             final_norm=final_norm)
    w = _prep_params(p)
    Bp, S, Dm = x_prompt.shape
    Bs, T, _ = x_sample.shape

    nrow = Bp + Bs
    npad = -(-nrow // SUBLANES) * SUBLANES
    c_all = jnp.pad(jnp.concatenate([c_prompt, c_sample], axis=0), ((0, npad - nrow), (0, 0)))
    mod = _ada_linear(c_all, ada_w, ada_b[:, None, :])
    kvm = _ada_linear(c_all, kv_ada_w[None], kv_ada_b[None, None, :])[0]

    def per_seq(m):
        return m[:, None, :]

    def per_token(m):
        return jnp.repeat(m, T, axis=0)[None]

    mods_p = [per_seq(mod[i, :Bp]) for i in range(DEPTH)]
    mods_s = [per_token(mod[i, Bp:nrow]) for i in range(DEPTH)]
    kvm_p = per_seq(kvm[:Bp])
    kvm_s = per_token(kvm[Bp:nrow])
    gate_p = [(mods_p[i], 2) for i in range(N_A)]
    gate_s = [(per_seq(mod[i, Bp:nrow]), 2) for i in range(N_A)]

    ssm0 = jnp.zeros((N_A, Bp, M_HEADS, M_HEADDIM, M_D_STATE), state_ssm.dtype)
    y_p, ssm_p, conv_p, ffn_p, k_p, v_p = _trunk(
        x_prompt, mods_p, kvm_p, gate_p, jnp.arange(S, dtype=jnp.int32), ssm0, None, None, None, p, w,
        tm=1024, step=False)
    cache_t = (cache_k.transpose(0, 2, 3, 1), cache_v.transpose(0, 2, 3, 1))
    y_s, ssm_s, conv_s, ffn_s, k_s, v_s = _trunk(
        x_sample, mods_s, kvm_s, gate_s, PAST_LEN + jnp.arange(T, dtype=jnp.int32), state_ssm,
        state_conv, state_ffn_conv, cache_t, p, w, tm=Bs * T, step=True)
    return (y_p, y_s, ssm_p, ssm_s, conv_p, conv_s, ffn_p, ffn_s, k_p, k_s, v_p, v_s)
```
